```python
import math
import jax
import jax.numpy as jnp
from jax import lax
import numpy as np

D_MODEL = 1024
BATCH = 8
SEQ = 2048
DEPTH = 4
DEC_BATCH = 128
DEC_SEQ = 8
PAST_LEN = 16384
PAGE_SIZE = 128

N_MIXERS = 4
N_A = len(range(0, DEPTH, N_MIXERS))
N_B = len(range(1, DEPTH, N_MIXERS))
N_C = len(range(2, DEPTH, N_MIXERS))
N_D = len(range(3, DEPTH, N_MIXERS))
ALPHA = (2.0 * DEPTH) ** 0.25
BETA = (8.0 * DEPTH) ** -0.25
LN_EPS = 1e-5
A_CHUNK = 128
A_FFN = 6 * D_MODEL
A_HALF = A_FFN // 2
A_GROUPS = 4
B_HEADS = 4
B_DK = D_MODEL // 2
B_DV = D_MODEL
B_HDK = B_DK // B_HEADS
B_HDV = B_DV // B_HEADS
B_GATE_RANK = 16
B_GATE_TAU = 16.0
B_IN = 2 * B_DK + 2 * B_DV + B_GATE_RANK
C_EXPAND = 128
C_HEADS = D_MODEL // C_EXPAND
C_HDV = D_MODEL // C_HEADS
D_CONV_W = 31
D_BUF = D_CONV_W - 1
LIN_CHUNK = 32
MEM_LEN = 256
MEM_HEADS = 4
MEM_HD = D_MODEL // MEM_HEADS
MOE_GROUPS = 4
MOE_PER_GROUP = 8
MOE_EXPERTS = MOE_GROUPS * MOE_PER_GROUP
MOE_TOPK = 2
MOE_HIDDEN = 512
MOE_BLOCK = 128

kernel_name = 'hybrid_gmlp_gla_hgrn2_conformer_hmoe_step'


def layer_norm(x, g, b):
    xf = x.astype(jnp.float32)
    mu = jnp.mean(xf, axis=-1, keepdims=True)
    var = jnp.mean(jnp.square(xf - mu), axis=-1, keepdims=True)
    return ((xf - mu) * lax.rsqrt(var + LN_EPS) * g.astype(jnp.float32) + b.astype(jnp.float32)).astype(x.dtype)


def rms_norm(x, g):
    xf = x.astype(jnp.float32)
    return xf * lax.rsqrt(jnp.mean(jnp.square(xf), axis=-1, keepdims=True) + LN_EPS) * g.astype(jnp.float32)


def gated_linear_recurrence(q, k, v, log_decay, s0):
    bsz, L, H, K = q.shape
    V = v.shape[-1]
    C = LIN_CHUNK if L % LIN_CHUNK == 0 else L
    n = L // C

    def chunks(a):
        return a.astype(jnp.float32).reshape(bsz, n, C, H, a.shape[-1]).transpose(1, 0, 3, 2, 4)

    causal = jnp.tril(jnp.ones((C, C), dtype=bool))

    def step(S, inp):
        qc, kc, vc, gc = inp
        b = jnp.cumsum(gc, axis=2)
        b_end = b[:, :, -1:, :]
        qd = qc * jnp.exp(b)
        kd = kc * jnp.exp(-b)
        a = jnp.where(causal, jnp.einsum('bhtk,bhsk->bhts', qd, kd), 0.0)
        o = jnp.einsum('bhts,bhsv->bhtv', a, vc) + jnp.einsum('bhtk,bhkv->bhtv', qd, S)
        S = jnp.exp(b_end[:, :, 0, :])[..., None] * S + jnp.einsum('bhsk,bhsv->bhkv', kc * jnp.exp(b_end - b), vc)
        return S, o

    S, o = lax.scan(step, s0.astype(jnp.float32), (chunks(q), chunks(k), chunks(v), chunks(log_decay)))
    o = o.transpose(1, 0, 3, 2, 4).reshape(bsz, L, H, V)
    return o, S


def chunk_gmlp(x, w_in, b_in, ln_g, ln_b, w_s, b_s, w_out, b_out):
    bsz, L, _ = x.shape
    z = jax.nn.gelu(x @ w_in + b_in, approximate=False)
    u, v = jnp.split(z, 2, axis=-1)
    v = layer_norm(v, ln_g, ln_b)
    C = A_CHUNK if L % A_CHUNK == 0 else L
    n = L // C
    w_causal = jnp.where(jnp.tril(jnp.ones((C, C), dtype=bool)), w_s[:, :C, :C], 0.0).astype(v.dtype)
    vg = v.reshape(bsz, n, C, A_GROUPS, A_HALF // A_GROUPS)
    mixed = jnp.einsum('gts,bnsgc->bntgc', w_causal, vg) + b_s[:, :C].T[None, None, :, :, None]
    return (u * mixed.reshape(bsz, L, A_HALF)) @ w_out + b_out, v


def gla(x, s0, w_in, w_g2, b_g, norm_g, w_out):
    bsz, L, _ = x.shape
    q, k, v, r, g_low = jnp.split(x @ w_in, [B_DK, 2 * B_DK, 2 * B_DK + B_DV, 2 * B_DK + 2 * B_DV], axis=-1)
    log_a = jax.nn.log_sigmoid((g_low @ w_g2 + b_g).astype(jnp.float32)) / B_GATE_TAU
    q = q.reshape(bsz, L, B_HEADS, B_HDK) * (B_HDK ** -0.5)
    k = k.reshape(bsz, L, B_HEADS, B_HDK)
    v = v.reshape(bsz, L, B_HEADS, B_HDV)
    log_a = log_a.reshape(bsz, L, B_HEADS, B_HDK)
    o, s = gated_linear_recurrence(q, k, v, log_a, s0)
    o = rms_norm(o, norm_g).reshape(bsz, L, B_DV).astype(x.dtype) * jax.nn.silu(r)
    return o @ w_out, s


def hgrn2(x, s0, lb, w_in, norm_g, w_out):
    bsz, L, _ = x.shape
    q, f, i_in, g = jnp.split(x @ w_in, 4, axis=-1)
    f = f.astype(jnp.float32)
    lb = lb.astype(jnp.float32)
    log_f = jnp.logaddexp(jnp.log(lb), jnp.log1p(-lb) + jax.nn.log_sigmoid(f))
    k = (1.0 - lb) * jax.nn.sigmoid(-f)
    q = jax.nn.silu(q).reshape(bsz, L, C_HEADS, C_EXPAND) * (C_EXPAND ** -0.5)
    k = k.reshape(bsz, L, C_HEADS, C_EXPAND)
    log_f = log_f.reshape(bsz, L, C_HEADS, C_EXPAND)
    i_in = i_in.reshape(bsz, L, C_HEADS, C_HDV)
    o, s = gated_linear_recurrence(q, k, i_in, log_f, s0)
    o = rms_norm(o, norm_g).reshape(bsz, L, D_MODEL).astype(x.dtype) * jax.nn.silu(g)
    return o @ w_out, s


def conformer_conv(x, buf, w_in, b_in, w_dw, b_dw, ln_g, ln_b, w_out, b_out):
    h = x @ w_in + b_in
    a, gate = jnp.split(h, 2, axis=-1)
    h = a * jax.nn.sigmoid(gate)
    hp = jnp.concatenate([buf.astype(h.dtype), h], axis=1)
    conv = lax.conv_general_dilated(hp, w_dw[:, None, :].astype(h.dtype), window_strides=(1,), padding='VALID',
                                    dimension_numbers=('NWC', 'WIO', 'NWC'), feature_group_count=D_MODEL) + b_dw
    conv = jax.nn.silu(layer_norm(conv, ln_g, ln_b))
    return conv @ w_out + b_out, hp[:, -D_BUF:]


def mem_cross_attention(x, mk, mv, w_q, w_o):
    bsz, L, _ = x.shape
    q = (x @ w_q).reshape(bsz, L, MEM_HEADS, MEM_HD).astype(jnp.float32)
    s = jnp.einsum('blhd,bmhd->bhlm', q, mk.astype(jnp.float32)) * (MEM_HD ** -0.5)
    p = jax.nn.softmax(s, axis=-1)
    o = jnp.einsum('bhlm,bmhd->blhd', p, mv.astype(jnp.float32)).reshape(bsz, L, D_MODEL).astype(x.dtype)
    return o @ w_o


def routed_experts(xf, e_idx, e_w, w_gate_up, w_down):
    T, D = xf.shape
    K = e_idx.shape[1]
    P = T * K
    flat_e = e_idx.reshape(P)
    flat_tok = jnp.repeat(jnp.arange(T, dtype=jnp.int32), K)
    flat_w = e_w.reshape(P)
    order = jnp.argsort(flat_e)
    se = flat_e[order]
    counts = jnp.bincount(flat_e, length=MOE_EXPERTS)
    padded = ((counts + MOE_BLOCK - 1) // MOE_BLOCK) * MOE_BLOCK
    pad_end = jnp.cumsum(padded)
    pad_start = pad_end - padded
    start = jnp.cumsum(counts) - counts
    dest = pad_start[se] + jnp.arange(P) - start[se]
    n_blocks = -(-P // MOE_BLOCK) + MOE_EXPERTS
    buf_tok = jnp.full((n_blocks * MOE_BLOCK,), T, dtype=jnp.int32).at[dest].set(flat_tok[order])
    buf_w = jnp.zeros((n_blocks * MOE_BLOCK,), jnp.float32).at[dest].set(flat_w[order])
    blk_e = jnp.minimum(jnp.searchsorted(pad_end, jnp.arange(n_blocks) * MOE_BLOCK, side='right'), MOE_EXPERTS - 1)
    x_pad = jnp.concatenate([xf, jnp.zeros((1, D), xf.dtype)], axis=0)
    xb = x_pad[buf_tok].reshape(n_blocks, MOE_BLOCK, D)

    def expert_block(args):
        xblk, e = args
        g, u = jnp.split(xblk @ w_gate_up[e], 2, axis=-1)
        return (jax.nn.silu(g) * u) @ w_down[e]

    yb = lax.map(expert_block, (xb, blk_e)).reshape(n_blocks * MOE_BLOCK, D)
    y = jax.ops.segment_sum(yb * buf_w[:, None].astype(yb.dtype), buf_tok, num_segments=T + 1)
    return y[:T]


def hierarchical_moe(x, w_grp, b_grp, w_exp, b_exp, w_gate_up, w_down):
    bsz, L, D = x.shape
    xf = x.reshape(bsz * L, D)
    gl = (xf @ w_grp).astype(jnp.float32) + b_grp.astype(jnp.float32)
    p_grp = jax.nn.softmax(gl, axis=-1)
    g_idx = jnp.argmax(gl, axis=-1)
    g_w = jnp.take_along_axis(p_grp, g_idx[:, None], axis=-1)[:, 0]
    el = ((xf @ w_exp).astype(jnp.float32) + b_exp.astype(jnp.float32)).reshape(-1, MOE_GROUPS, MOE_PER_GROUP)
    el_sel = jnp.take_along_axis(el, g_idx[:, None, None], axis=1)[:, 0]
    top_v, top_i = lax.top_k(el_sel, MOE_TOPK)
    e_w = jax.nn.softmax(top_v, axis=-1) * g_w[:, None]
    e_idx = (g_idx[:, None] * MOE_PER_GROUP + top_i).astype(jnp.int32)
    return routed_experts(xf, e_idx, e_w, w_gate_up, w_down).reshape(bsz, L, D)


def trunk(x, mem_k, mem_v, gla_s0, hgrn_s0, conv_b0, p):
    lb_all = jnp.cumsum(jax.nn.softmax(p['c_lb'].astype(jnp.float32), axis=0), axis=0)
    lb_all = lb_all - lb_all[:1]
    v_rows, gla_s, hgrn_s, conv_s = [], [], [], []
    for i in range(DEPTH):
        j = i // N_MIXERS
        kind = i % N_MIXERS
        if kind == 0:
            h, v = chunk_gmlp(x, p['a_w_in'][j], p['a_b_in'][j], p['a_ln_g'][j], p['a_ln_b'][j], p['a_w_s'][j],
                              p['a_b_s'][j], p['a_w_out'][j], p['a_b_out'][j])
            v_rows.append(v)
        elif kind == 1:
            h, s = gla(x, gla_s0[j], p['b_w_in'][j], p['b_w_g2'][j], p['b_b_g'][j], p['b_norm_g'][j], p['b_w_out'][j])
            gla_s.append(s)
        elif kind == 2:
            h, s = hgrn2(x, hgrn_s0[j], lb_all[i], p['c_w_in'][j], p['c_norm_g'][j], p['c_w_out'][j])
            hgrn_s.append(s)
        else:
            h, s = conformer_conv(x, conv_b0[j], p['d_w_in'][j], p['d_b_in'][j], p['d_w_dw'][j], p['d_b_dw'][j],
                                  p['d_ln_g'][j], p['d_ln_b'][j], p['d_w_out'][j], p['d_b_out'][j])
            conv_s.append(s)
        x = layer_norm(ALPHA * x + h, p['ln_g'][i, 0], p['ln_b'][i, 0])
        x = layer_norm(ALPHA * x + mem_cross_attention(x, mem_k[i], mem_v[i], p['m_w_q'][i], p['m_w_o'][i]),
                       p['ln_g'][i, 1], p['ln_b'][i, 1])
        x = layer_norm(ALPHA * x + hierarchical_moe(x, p['r_w_grp'][i], p['r_b_grp'][i], p['r_w_exp'][i],
                                                    p['r_b_exp'][i], p['e_w_gate_up'][i], p['e_w_down'][i]),
                       p['ln_g'][i, 2], p['ln_b'][i, 2])
    return x, v_rows, gla_s, hgrn_s, conv_s


def setup_inputs(seed: int = 0) -> dict:
    key = jax.random.key(seed)
    ks = iter(jax.random.split(key, 64))

    def nrm(shape, scale):
        return jax.random.normal(next(ks), shape, jnp.float32) * scale

    D = D_MODEL
    return {
        'x_prompt': nrm((BATCH, SEQ, D), 1.0),
        'x_sample': nrm((DEC_BATCH, DEC_SEQ, D), 1.0),
        'mem_prompt': nrm((BATCH, MEM_LEN, D), 1.0),
        'cache_mem_k': nrm((DEPTH, DEC_BATCH, MEM_LEN, MEM_HEADS, MEM_HD), 1.0),
        'cache_mem_v': nrm((DEPTH, DEC_BATCH, MEM_LEN, MEM_HEADS, MEM_HD), 0.5),
        'state_gla': nrm((N_B, DEC_BATCH, B_HEADS, B_HDK, B_HDV), 1.0),
        'state_hgrn': nrm((N_C, DEC_BATCH, C_HEADS, C_EXPAND, C_HDV), 1.0),
        'state_conv': nrm((N_D, DEC_BATCH, D_BUF, D), 0.5),
        'ln_g': 1.0 + nrm((DEPTH, 3, D), 0.02),
        'ln_b': nrm((DEPTH, 3, D), 0.02),
        'a_w_in': nrm((N_A, D, A_FFN), D ** -0.5),
        'a_b_in': nrm((N_A, A_FFN), 0.02),
        'a_ln_g': 1.0 + nrm((N_A, A_HALF), 0.02),
        'a_ln_b': nrm((N_A, A_HALF), 0.02),
        'a_w_s': nrm((N_A, A_GROUPS, A_CHUNK, A_CHUNK), A_CHUNK ** -0.5),
        'a_b_s': 1.0 + nrm((N_A, A_GROUPS, A_CHUNK), 0.02),
        'a_w_out': nrm((N_A, A_HALF, D), A_HALF ** -0.5 * BETA),
        'a_b_out': nrm((N_A, D), 0.02),
        'b_w_in': nrm((N_B, D, B_IN), D ** -0.5),
        'b_w_g2': nrm((N_B, B_GATE_RANK, B_DK), B_GATE_RANK ** -0.5),
        'b_b_g': nrm((N_B, B_DK), 0.1),
        'b_norm_g': 1.0 + nrm((N_B, B_HEADS, B_HDV), 0.02),
        'b_w_out': nrm((N_B, B_DV, D), B_DV ** -0.5 * BETA),
        'c_lb': nrm((DEPTH, C_HEADS * C_EXPAND), 0.5),
        'c_w_in': nrm((N_C, D, 4 * D), D ** -0.5),
        'c_norm_g': 1.0 + nrm((N_C, C_HEADS, C_HDV), 0.02),
        'c_w_out': nrm((N_C, D, D), D ** -0.5 * BETA),
        'd_w_in': nrm((N_D, D, 2 * D), D ** -0.5),
        'd_b_in': nrm((N_D, 2 * D), 0.02),
        'd_w_dw': nrm((N_D, D_CONV_W, D), D_CONV_W ** -0.5),
        'd_b_dw': nrm((N_D, D), 0.02),
        'd_ln_g': 1.0 + nrm((N_D, D), 0.02),
        'd_ln_b': nrm((N_D, D), 0.02),
        'd_w_out': nrm((N_D, D, D), D ** -0.5 * BETA),
        'd_b_out': nrm((N_D, D), 0.02),
        'm_w_q': nrm((DEPTH, D, D), D ** -0.5),
        'm_w_k': nrm((DEPTH, D, D), D ** -0.5),
        'm_w_v': nrm((DEPTH, D, D), D ** -0.5 * BETA),
        'm_w_o': nrm((DEPTH, D, D), D ** -0.5 * BETA),
        'r_w_grp': nrm((DEPTH, D, MOE_GROUPS), D ** -0.5),
        'r_b_grp': nrm((DEPTH, MOE_GROUPS), 0.01),
        'r_w_exp': nrm((DEPTH, D, MOE_EXPERTS), D ** -0.5),
        'r_b_exp': nrm((DEPTH, MOE_EXPERTS), 0.01),
        'e_w_gate_up': nrm((DEPTH, MOE_EXPERTS, D, 2 * MOE_HIDDEN), D ** -0.5),
        'e_w_down': nrm((DEPTH, MOE_EXPERTS, MOE_HIDDEN, D), MOE_HIDDEN ** -0.5 * BETA),
    }


def reference(x_prompt, x_sample, mem_prompt, cache_mem_k, cache_mem_v, state_gla, state_hgrn, state_conv,
              ln_g, ln_b, a_w_in, a_b_in, a_ln_g, a_ln_b, a_w_s, a_b_s, a_w_out, a_b_out,
              b_w_in, b_w_g2, b_b_g, b_norm_g, b_w_out, c_lb, c_w_in, c_norm_g, c_w_out,
              d_w_in, d_b_in, d_w_dw, d_b_dw, d_ln_g, d_ln_b, d_w_out, d_b_out,
              m_w_q, m_w_k, m_w_v, m_w_o, r_w_grp, r_b_grp, r_w_exp, r_b_exp, e_w_gate_up, e_w_down):
    p = dict(ln_g=ln_g, ln_b=ln_b, a_w_in=a_w_in, a_b_in=a_b_in, a_ln_g=a_ln_g, a_ln_b=a_ln_b, a_w_s=a_w_s,
             a_b_s=a_b_s, a_w_out=a_w_out, a_b_out=a_b_out, b_w_in=b_w_in, b_w_g2=b_w_g2, b_b_g=b_b_g,
             b_norm_g=b_norm_g, b_w_out=b_w_out, c_lb=c_lb, c_w_in=c_w_in, c_norm_g=c_norm_g, c_w_out=c_w_out,
             d_w_in=d_w_in, d_b_in=d_b_in, d_w_dw=d_w_dw, d_b_dw=d_b_dw, d_ln_g=d_ln_g, d_ln_b=d_ln_b,
             d_w_out=d_w_out, d_b_out=d_b_out, m_w_q=m_w_q, m_w_o=m_w_o, r_w_grp=r_w_grp, r_b_grp=r_b_grp,
             r_w_exp=r_w_exp, r_b_exp=r_b_exp, e_w_gate_up=e_w_gate_up, e_w_down=e_w_down)
    bsz_p = x_prompt.shape[0]
    mem_k_p = jnp.einsum('bmd,ldf->lbmf', mem_prompt, m_w_k).reshape(DEPTH, bsz_p, MEM_LEN, MEM_HEADS, MEM_HD)
    mem_v_p = jnp.einsum('bmd,ldf->lbmf', mem_prompt, m_w_v).reshape(DEPTH, bsz_p, MEM_LEN, MEM_HEADS, MEM_HD)
    gla0 = jnp.zeros((N_B, bsz_p, B_HEADS, B_HDK, B_HDV), jnp.float32)
    hgrn0 = jnp.zeros((N_C, bsz_p, C_HEADS, C_EXPAND, C_HDV), jnp.float32)
    conv0 = jnp.zeros((N_D, bsz_p, D_BUF, D_MODEL), x_prompt.dtype)
    y_prompt, _, gla_p, hgrn_p, conv_p = trunk(x_prompt, mem_k_p, mem_v_p, gla0, hgrn0, conv0, p)
    y_sample, v_s, gla_s, hgrn_s, conv_s = trunk(x_sample, cache_mem_k, cache_mem_v, state_gla, state_hgrn,
                                                 state_conv, p)
    return (y_prompt, y_sample, mem_k_p, mem_v_p, jnp.stack(gla_p), jnp.stack(hgrn_p), jnp.stack(conv_p),
            jnp.stack(v_s), jnp.stack(gla_s), jnp.stack(hgrn_s), jnp.stack(conv_s))
```

```python
import functools
import math

import jax
import jax.numpy as jnp
from jax import lax
from jax.experimental import pallas as pl
from jax.experimental.pallas import tpu as pltpu

F32 = jnp.float32
BF16 = jnp.bfloat16

D_MODEL = 1024
DEPTH = 4
N_MIXERS = 4
ALPHA = (2.0 * DEPTH) ** 0.25
LN_EPS = 1e-5
A_CHUNK = 128
A_GROUPS = 4
B_HEADS = 4
B_GATE_RANK = 16
B_GATE_TAU = 16.0
C_EXPAND = 128
D_CONV_W = 31
D_BUF = D_CONV_W - 1
LIN_CHUNK = 32
MEM_HEADS = 4
MOE_GROUPS = 4
MOE_PER_GROUP = 8
MOE_EXPERTS = MOE_GROUPS * MOE_PER_GROUP
MOE_HIDDEN = 512

LANES = 128
ROW_TILE = 512
MOE_BLOCK_ROWS = 256
ROUTE_LANE0 = MOE_GROUPS
VMEM_LIMIT = 56 * 1024 * 1024
INV_SQRT2 = 1.0 / math.sqrt(2.0)


def _params(*sem, vmem=VMEM_LIMIT):
    return pltpu.CompilerParams(dimension_semantics=sem, vmem_limit_bytes=vmem)


def _dot(a, b):
    return jnp.dot(a, b, preferred_element_type=F32)


def _dot_nt(a, b):
    return lax.dot_general(a, b, (((1,), (1,)), ((), ())), preferred_element_type=F32)


def _split3(a):
    hi = a.astype(BF16)
    r = a - hi.astype(F32)
    mid = r.astype(BF16)
    lo = (r - mid.astype(F32)).astype(BF16)
    return hi, mid, lo


def _dot_exact_lhs01(m01, a):
    hi, mid, lo = _split3(a)
    return _dot(m01, hi) + _dot(m01, mid) + _dot(m01, lo)


def _dot_hi(a, w_hi, w_lo):
    a_hi = a.astype(BF16)
    a_lo = (a - a_hi.astype(F32)).astype(BF16)
    return _dot(a_hi, w_hi) + _dot(a_lo, w_hi) + _dot(a_hi, w_lo)


def _layer_norm(x, g, b):
    mu = jnp.mean(x, axis=-1, keepdims=True)
    xc = x - mu
    var = jnp.mean(xc * xc, axis=-1, keepdims=True)
    return xc * lax.rsqrt(var + LN_EPS) * g + b


def _sigmoid(x):
    return 1.0 / (1.0 + jnp.exp(-x))


def _silu(x):
    return x * _sigmoid(x)


def _gelu(x):
    return 0.5 * x * (1.0 + lax.erf(x * INV_SQRT2))


def _log_sigmoid(x):
    return jnp.minimum(x, 0.0) - jnp.log(1.0 + jnp.exp(-jnp.abs(x)))


def _full(shape):
    return pl.BlockSpec(shape, lambda *_: (0,) * len(shape))


def _rows(tm, n, off=0):
    return pl.BlockSpec((tm, n), lambda i: (i + off, 0))


def _mm_kernel(a_ref, w_ref, o_ref):
    o_ref[...] = _dot(a_ref[...].astype(BF16), w_ref[...])


def _mm_layers(a, w):
    m, k = a.shape
    nl, _, n = w.shape
    tm = min(m, ROW_TILE)
    return pl.pallas_call(
        _mm_kernel,
        grid=(nl, m // tm),
        in_specs=[pl.BlockSpec((tm, k), lambda l, i: (i, 0)),
                  pl.BlockSpec((None, k, n), lambda l, i: (l, 0, 0))],
        out_specs=pl.BlockSpec((None, tm, n), lambda l, i: (l, i, 0)),
        out_shape=jax.ShapeDtypeStruct((nl, m, n), F32),
        compiler_params=_params("parallel", "parallel"),
        name="mem_kv_proj",
    )(a, w)


def _mm_rows(a, w, name):
    m, k = a.shape
    n = w.shape[1]
    tm = ROW_TILE
    return pl.pallas_call(
        _mm_kernel,
        grid=(m // tm,),
        in_specs=[_rows(tm, k), _full((k, n))],
        out_specs=_rows(tm, n),
        out_shape=jax.ShapeDtypeStruct((m, n), F32),
        compiler_params=_params("parallel"),
        name=name,
    )(a, w)


def _mm_ln_kernel(a_ref, w_ref, bias_ref, x_ref, g_ref, b_ref, *rest, with_logits):
    h = _dot(a_ref[...].astype(BF16), w_ref[...]) + bias_ref[...]
    y = _layer_norm(ALPHA * x_ref[...] + h, g_ref[...], b_ref[...])
    if with_logits:
        wr_hi_ref, wr_lo_ref, o_ref, lg_ref = rest
        lg_ref[...] = _dot_hi(y, wr_hi_ref[...], wr_lo_ref[...])
    else:
        (o_ref,) = rest
    o_ref[...] = y


def _mm_ln(a, w, bias, x, g, b, name, router=None):
    m, k = a.shape
    d = x.shape[1]
    tm = ROW_TILE
    in_specs = [_rows(tm, k), _full((k, d)), _full((1, d)), _rows(tm, d), _full((1, d)), _full((1, d))]
    args = [a, w, bias, x, g, b]
    out_specs = _rows(tm, d)
    out_shape = jax.ShapeDtypeStruct((m, d), F32)
    if router is not None:
        in_specs += [_full((d, LANES)), _full((d, LANES))]
        args += list(router)
        out_specs = [out_specs, _rows(tm, LANES)]
        out_shape = [out_shape, jax.ShapeDtypeStruct((m, LANES), F32)]
    return pl.pallas_call(
        functools.partial(_mm_ln_kernel, with_logits=router is not None),
        grid=(m // tm,),
        in_specs=in_specs,
        out_specs=out_specs,
        out_shape=out_shape,
        compiler_params=_params("parallel"),
        name=name,
    )(*args)


def _gmlp_kernel(x_ref, w_in_ref, b_in_ref, lng_ref, lnb_ref, wc_ref, bc_ref, w_out_ref, b_out_ref,
                 g_ref, b_ref, *rest, emit_v, n_chunks):
    if emit_v:
        _alias_ref, o_ref, v_ref, vn_ref = rest
    else:
        o_ref, vn_ref = rest
    half = w_out_ref.shape[0]
    gw = half // A_GROUPS
    x = x_ref[...]
    xb = x.astype(BF16)
    v = _gelu(_dot(xb, w_in_ref[:, half:]) + b_in_ref[:, half:])
    vn = _layer_norm(v, lng_ref[...], lnb_ref[...])
    vn_ref[...] = vn
    if emit_v:
        v_ref[...] = vn
    acc = jnp.zeros(x.shape, F32)
    for grp in range(A_GROUPS):
        cols = slice(grp * gw, (grp + 1) * gw)
        u = _gelu(_dot(xb, w_in_ref[:, cols]) + b_in_ref[:, cols])
        mixed = []
        for c in range(n_chunks):
            vc = vn_ref[c * A_CHUNK:(c + 1) * A_CHUNK, cols].astype(BF16)
            mixed.append(_dot(wc_ref[grp], vc) + bc_ref[grp])
        mixed = mixed[0] if n_chunks == 1 else jnp.concatenate(mixed, axis=0)
        acc = acc + _dot((u * mixed).astype(BF16), w_out_ref[cols, :])
    h = acc + b_out_ref[...]
    o_ref[...] = _layer_norm(ALPHA * x + h, g_ref[...], b_ref[...])


def _gmlp(x, row_off, n_rows, w_in, b_in, lng, lnb, wc, bc, w_out, b_out, g, b, alias=None):
    t, d = x.shape
    ffn = w_in.shape[1]
    half = ffn // 2
    tm = 2 * A_CHUNK
    emit_v = alias is not None
    off = row_off // tm
    in_specs = [_rows(tm, d, off), _full((d, ffn)), _full((1, ffn)), _full((1, half)), _full((1, half)),
                _full((A_GROUPS, A_CHUNK, A_CHUNK)), _full((A_GROUPS, A_CHUNK, 1)), _full((half, d)),
                _full((1, d)), _full((1, d)), _full((1, d))]
    args = [x, w_in, b_in, lng, lnb, wc, bc, w_out, b_out, g, b]
    out_specs = _rows(tm, d, off)
    out_shape = jax.ShapeDtypeStruct((t, d), F32)
    aliases = {}
    if emit_v:
        in_specs.append(pl.BlockSpec(memory_space=pl.ANY))
        args.append(alias)
        aliases = {len(args) - 1: 0}
        out_specs = [out_specs, _rows(tm, half)]
        out_shape = [out_shape, jax.ShapeDtypeStruct((n_rows, half), F32)]
    return pl.pallas_call(
        functools.partial(_gmlp_kernel, emit_v=emit_v, n_chunks=tm // A_CHUNK),
        grid=(n_rows // tm,),
        in_specs=in_specs,
        out_specs=out_specs,
        out_shape=out_shape,
        scratch_shapes=[pltpu.VMEM((tm, half), F32)],
        input_output_aliases=aliases,
        compiler_params=_params("parallel"),
        name="gmlp_sample" if emit_v else "gmlp_prompt",
    )(*args)


def _gla_proj_kernel(x_ref, w_ref, wlow_ref, g2_hi_ref, g2_lo_ref, bg_ref,
                     q_ref, k_ref, v_ref, r_ref, la_ref, *, dk, dv, q_scale):
    xb = x_ref[...].astype(BF16)
    q_ref[...] = _dot(xb, w_ref[:, 0:dk]) * q_scale
    k_ref[...] = _dot(xb, w_ref[:, dk:2 * dk])
    v_ref[...] = _dot(xb, w_ref[:, 2 * dk:2 * dk + dv])
    r_ref[...] = _dot(xb, w_ref[:, 2 * dk + dv:2 * dk + 2 * dv])
    g_low = _dot(xb, wlow_ref[...])
    pre = _dot_hi(g_low, g2_hi_ref[...], g2_lo_ref[...]) + bg_ref[...]
    la_ref[...] = _log_sigmoid(pre) * (1.0 / B_GATE_TAU)


def _gla_proj(x, w_main, w_low, g2_hi, g2_lo, bg, dk, dv):
    t, d = x.shape
    tm = ROW_TILE
    shapes = [dk, dk, dv, dv, dk]
    return pl.pallas_call(
        functools.partial(_gla_proj_kernel, dk=dk, dv=dv, q_scale=(dk // B_HEADS) ** -0.5),
        grid=(t // tm,),
        in_specs=[_rows(tm, d), _full(w_main.shape), _full(w_low.shape), _full(g2_hi.shape),
                  _full(g2_lo.shape), _full((1, dk))],
        out_specs=[_rows(tm, n) for n in shapes],
        out_shape=[jax.ShapeDtypeStruct((t, n), F32) for n in shapes],
        compiler_params=_params("parallel"),
        name="gla_proj",
    )(x, w_main, w_low, g2_hi, g2_lo, bg)


def _hgrn_proj_kernel(x_ref, w_ref, lb_ref, q_ref, lf_ref, k_ref, v_ref, gt_ref, *, d, q_scale):
    xb = x_ref[...].astype(BF16)
    lb = lb_ref[...]
    q_ref[...] = _silu(_dot(xb, w_ref[:, 0:d])) * q_scale
    f = _dot(xb, w_ref[:, d:2 * d])
    lf_ref[...] = jnp.log(lb + (1.0 - lb) * _sigmoid(f))
    k_ref[...] = (1.0 - lb) * _sigmoid(-f)
    v_ref[...] = _dot(xb, w_ref[:, 2 * d:3 * d])
    gt_ref[...] = _dot(xb, w_ref[:, 3 * d:4 * d])


def _hgrn_proj(x, w, lb):
    t, d = x.shape
    tm = ROW_TILE
    return pl.pallas_call(
        functools.partial(_hgrn_proj_kernel, d=d, q_scale=C_EXPAND ** -0.5),
        grid=(t // tm,),
        in_specs=[_rows(tm, d), _full(w.shape), _full((1, d))],
        out_specs=[_rows(tm, d)] * 5,
        out_shape=[jax.ShapeDtypeStruct((t, d), F32)] * 5,
        compiler_params=_params("parallel"),
        name="hgrn_proj",
    )(x, w, lb)


def _linrec_kernel(q_ref, k_ref, g_ref, v_ref, gate_ref, ng_ref, s0_ref, *rest,
                   heads, dk, dv, chunk, n_chunks, aliased):
    if aliased:
        _alias_ref, o_ref, sout_ref, st_ref = rest
    else:
        o_ref, sout_ref, st_ref = rest
    li = pl.program_id(1)

    @pl.when(li == 0)
    def _():
        for h in range(heads):
            st_ref[h] = s0_ref[h].T

    r_id = lax.broadcasted_iota(jnp.int32, (chunk, chunk), 0)
    c_id = lax.broadcasted_iota(jnp.int32, (chunk, chunk), 1)
    causal = r_id >= c_id
    tri = jnp.where(causal, 1.0, 0.0).astype(BF16)

    def one_chunk(c, carry):
        rows = pl.ds(pl.multiple_of(c * chunk, chunk), chunk)
        for h in range(heads):
            ks = slice(h * dk, (h + 1) * dk)
            vs = slice(h * dv, (h + 1) * dv)
            gc = g_ref[rows, ks]
            bcum = _dot_exact_lhs01(tri, gc)
            b_end = bcum[chunk - 1:chunk, :]
            qd = (q_ref[rows, ks] * jnp.exp(bcum)).astype(BF16)
            kc = k_ref[rows, ks]
            kd = (kc * jnp.exp(-bcum)).astype(BF16)
            kk = (kc * jnp.exp(b_end - bcum)).astype(BF16)
            vc = v_ref[rows, vs]
            vb = vc.astype(BF16)
            a = jnp.where(causal, _dot_nt(qd, kd), 0.0).astype(BF16)
            st = st_ref[h]
            o = _dot(a, vb) + _dot_nt(qd, st.astype(BF16))
            st_ref[h] = st * jnp.exp(b_end) + _dot(vc.T.astype(BF16), kk)
            o = o * lax.rsqrt(jnp.mean(o * o, axis=-1, keepdims=True) + LN_EPS) * ng_ref[:, vs]
            o_ref[rows, vs] = o * _silu(gate_ref[rows, vs])
        return carry

    lax.fori_loop(0, n_chunks, one_chunk, 0)

    @pl.when(li == pl.num_programs(1) - 1)
    def _():
        for h in range(heads):
            sout_ref[h] = st_ref[h].T


def _linrec(q, k, g, v, gate, ng, s0, *, n_batch, seq, row_off, heads, alias=None):
    t, hk = q.shape
    hv = v.shape[1]
    dk, dv = hk // heads, hv // heads
    chunk = LIN_CHUNK if seq % LIN_CHUNK == 0 else seq
    tl = min(seq, 8 * chunk)
    n_l = seq // tl
    off = row_off // tl

    def rows(n):
        return pl.BlockSpec((tl, n), lambda b, l: (off + b * n_l + l, 0))

    state_spec = pl.BlockSpec((None, heads, dk, dv), lambda b, l: (b, 0, 0, 0))
    in_specs = [rows(hk), rows(hk), rows(hk), rows(hv), rows(hv), pl.BlockSpec((1, hv), lambda b, l: (0, 0)),
                state_spec]
    args = [q, k, g, v, gate, ng, s0]
    aliases = {}
    if alias is not None:
        in_specs.append(pl.BlockSpec(memory_space=pl.ANY))
        args.append(alias)
        aliases = {len(args) - 1: 0}
    return pl.pallas_call(
        functools.partial(_linrec_kernel, heads=heads, dk=dk, dv=dv, chunk=chunk, n_chunks=tl // chunk,
                          aliased=alias is not None),
        grid=(n_batch, n_l),
        in_specs=in_specs,
        out_specs=[rows(hv), state_spec],
        out_shape=[jax.ShapeDtypeStruct((t, hv), F32), jax.ShapeDtypeStruct((n_batch, heads, dk, dv), F32)],
        scratch_shapes=[pltpu.VMEM((heads, dv, dk), F32)],
        input_output_aliases=aliases,
        compiler_params=_params("parallel", "arbitrary"),
        name="linrec_h%d_%s" % (heads, "sample" if alias is not None else "prompt"),
    )(*args)


def _glu_kernel(x_ref, w_ref, b_ref, o_ref, *, d):
    xb = x_ref[...].astype(BF16)
    a = _dot(xb, w_ref[:, 0:d]) + b_ref[:, 0:d]
    gate = _dot(xb, w_ref[:, d:2 * d]) + b_ref[:, d:2 * d]
    o_ref[...] = a * _sigmoid(gate)


def _glu(x, w, b):
    t, d = x.shape
    tm = ROW_TILE
    return pl.pallas_call(
        functools.partial(_glu_kernel, d=d),
        grid=(t // tm,),
        in_specs=[_rows(tm, d), _full(w.shape), _full(b.shape)],
        out_specs=_rows(tm, d),
        out_shape=jax.ShapeDtypeStruct((t, d), F32),
        compiler_params=_params("parallel"),
        name="conf_glu",
    )(x, w, b)


CONV_PAD = 32


def _conv_kernel(h_ref, st_ref, wdw_ref, bdw_ref, lg_ref, lb_ref, *rest, nb, tl, aliased):
    if aliased:
        _alias_ref, o_ref, sout_ref, buf_ref, conv_ref = rest
    else:
        o_ref, sout_ref, buf_ref, conv_ref = rest
    li = pl.program_id(1)
    d = h_ref.shape[-1]
    lead = CONV_PAD - D_BUF
    rb = min(tl, 32)
    cw = 512
    for n in range(nb):
        @pl.when(li == 0)
        def _():
            buf_ref[lead:CONV_PAD, :] = st_ref[n]

        buf_ref[CONV_PAD:CONV_PAD + tl, :] = h_ref[n * tl:(n + 1) * tl, :]
        for r0 in range(0, tl, rb):
            for c0 in range(0, d, cw):
                cols = slice(c0, c0 + cw)
                acc = jnp.zeros((rb, cw), F32)
                for j in range(D_CONV_W):
                    acc = acc + wdw_ref[j:j + 1, cols] * buf_ref[r0 + j + lead:r0 + j + lead + rb, cols]
                conv_ref[r0:r0 + rb, cols] = acc + bdw_ref[:, cols]
        y = _layer_norm(conv_ref[...], lg_ref[...], lb_ref[...])
        o_ref[n * tl:(n + 1) * tl, :] = _silu(y)
        sout_ref[n] = buf_ref[tl + lead:tl + CONV_PAD, :]
        buf_ref[0:CONV_PAD, :] = buf_ref[tl:tl + CONV_PAD, :]


def _conv(h, state, wdw, bdw, lg, lb, *, n_batch, seq, row_off, alias=None):
    t, d = h.shape
    tl = min(seq, 256)
    n_l = seq // tl
    nb = 8 if n_l == 1 else 1
    off = row_off // (nb * tl)
    rows = pl.BlockSpec((nb * tl, d), lambda b, l: (off + b * n_l + l, 0))
    state_spec = pl.BlockSpec((nb, D_BUF, d), lambda b, l: (b, 0, 0))
    in_specs = [rows, state_spec, _full(wdw.shape), _full((1, d)), _full((1, d)), _full((1, d))]
    args = [h, state, wdw, bdw, lg, lb]
    aliases = {}
    if alias is not None:
        in_specs.append(pl.BlockSpec(memory_space=pl.ANY))
        args.append(alias)
        aliases = {len(args) - 1: 0}
    return pl.pallas_call(
        functools.partial(_conv_kernel, nb=nb, tl=tl, aliased=alias is not None),
        grid=(n_batch // nb, n_l),
        in_specs=in_specs,
        out_specs=[rows, state_spec],
        out_shape=[jax.ShapeDtypeStruct((t, d), F32), jax.ShapeDtypeStruct((n_batch, D_BUF, d), F32)],
        scratch_shapes=[pltpu.VMEM((CONV_PAD + tl, d), F32), pltpu.VMEM((tl, d), F32)],
        input_output_aliases=aliases,
        compiler_params=_params("parallel", "arbitrary"),
        name="conf_conv_sample" if alias is not None else "conf_conv_prompt",
    )(*args)


def _attn_heads(q, k_of, v_of, hd):
    outs = []
    for h in range(MEM_HEADS):
        hs = slice(h * hd, (h + 1) * hd)
        s = _dot_nt(q[:, hs].astype(BF16), k_of(hs).astype(BF16)) * (hd ** -0.5)
        p = jnp.exp(s - jnp.max(s, axis=-1, keepdims=True))
        denom = jnp.sum(p, axis=-1, keepdims=True)
        outs.append(_dot(p.astype(BF16), v_of(hs).astype(BF16)) / denom)
    return outs


def _attn_prompt_kernel(q_ref, k_ref, v_ref, o_ref):
    hd = q_ref.shape[-1] // MEM_HEADS
    outs = _attn_heads(q_ref[...], lambda hs: k_ref[:, hs], lambda hs: v_ref[:, hs], hd)
    for h, o in enumerate(outs):
        o_ref[:, h * hd:(h + 1) * hd] = o


def _attn_prompt(q, mem_k, mem_v, layer, n_batch, seq):
    t, d = q.shape
    m = mem_k.shape[1] // n_batch
    tl = ROW_TILE
    n_l = seq // tl
    rows = pl.BlockSpec((tl, d), lambda b, l: (b * n_l + l, 0))
    kv = pl.BlockSpec((None, m, d), lambda b, l: (layer, b, 0))
    return pl.pallas_call(
        _attn_prompt_kernel,
        grid=(n_batch, n_l),
        in_specs=[rows, kv, kv],
        out_specs=rows,
        out_shape=jax.ShapeDtypeStruct((t, d), F32),
        compiler_params=_params("parallel", "parallel"),
        name="attn_prompt",
    )(q, mem_k, mem_v)


ATTN_SAMPLE_NB = 4


def _attn_sample_kernel(q_ref, k_ref, v_ref, _alias_ref, o_ref, *, nb, seq):
    hd = q_ref.shape[-1] // MEM_HEADS
    for n in range(nb):
        rs = slice(n * seq, (n + 1) * seq)
        outs = _attn_heads(q_ref[rs, :], lambda hs: k_ref[n, :, hs], lambda hs: v_ref[n, :, hs], hd)
        for h, o in enumerate(outs):
            o_ref[rs, h * hd:(h + 1) * hd] = o


def _attn_sample(q, cache_k, cache_v, layer, n_batch, seq, row_off, alias):
    t, d = q.shape
    m = cache_k.shape[2]
    nb = ATTN_SAMPLE_NB
    off = row_off // (nb * seq)
    rows = pl.BlockSpec((nb * seq, d), lambda i: (off + i, 0))
    kv = pl.BlockSpec((None, nb, m, d), lambda i: (layer, i, 0, 0))
    return pl.pallas_call(
        functools.partial(_attn_sample_kernel, nb=nb, seq=seq),
        grid=(n_batch // nb,),
        in_specs=[rows, kv, kv, pl.BlockSpec(memory_space=pl.ANY)],
        out_specs=rows,
        out_shape=jax.ShapeDtypeStruct((t, d), F32),
        input_output_aliases={3: 0},
        compiler_params=_params("parallel"),
        name="attn_sample",
    )(q, cache_k, cache_v, alias)


def _route_kernel(lg_ref, bias_ref, ltri_ref, route_ref, counts_ref, carry_ref):
    i = pl.program_id(0)

    @pl.when(i == 0)
    def _():
        carry_ref[...] = jnp.zeros(carry_ref.shape, F32)

    z = lg_ref[...] + bias_ref[...]
    lane = lax.broadcasted_iota(jnp.int32, z.shape, 1).astype(F32)
    neg = -jnp.inf
    far = float(LANES)

    def first_max(mask):
        vmax = jnp.max(jnp.where(mask, z, neg), axis=-1, keepdims=True)
        idx = jnp.min(jnp.where(mask & (z == vmax), lane, far), axis=-1, keepdims=True)
        return vmax, idx

    gmask = lane < float(MOE_GROUPS)
    gmax, gidx = first_max(gmask)
    gsum = jnp.sum(jnp.where(gmask, jnp.exp(z - gmax), 0.0), axis=-1, keepdims=True)
    g_w = 1.0 / gsum
    lo = float(ROUTE_LANE0) + float(MOE_PER_GROUP) * gidx
    emask = (lane >= lo) & (lane < lo + float(MOE_PER_GROUP))
    v1, i1 = first_max(emask)
    v2, i2 = first_max(emask & (lane != i1))
    tt = jnp.exp(v2 - v1)
    w0 = g_w / (1.0 + tt)
    w1 = g_w * tt / (1.0 + tt)
    sel1 = lane == i1
    sel2 = lane == i2
    onehot = jnp.where(sel1 | sel2, 1.0, 0.0)
    before = _dot(ltri_ref[...], onehot.astype(BF16)) + carry_ref[...]
    rank0 = jnp.sum(jnp.where(sel1, before, 0.0), axis=-1, keepdims=True)
    rank1 = jnp.sum(jnp.where(sel2, before, 0.0), axis=-1, keepdims=True)
    carry = carry_ref[...] + jnp.sum(onehot, axis=0, keepdims=True)
    carry_ref[...] = carry
    counts_ref[...] = carry
    e_off = float(ROUTE_LANE0)
    out = jnp.zeros(z.shape, F32)
    for ln, val in enumerate((i1 - e_off, i2 - e_off, w0, w1, rank0, rank1)):
        out = jnp.where(lane == float(ln), val, out)
    route_ref[...] = out


def _route(logits, bias, ltri):
    t = logits.shape[0]
    tm = ROW_TILE
    return pl.pallas_call(
        _route_kernel,
        grid=(t // tm,),
        in_specs=[_rows(tm, LANES), _full((1, LANES)), _full((tm, tm))],
        out_specs=[_rows(tm, LANES), _full((1, LANES))],
        out_shape=[jax.ShapeDtypeStruct((t, LANES), F32), jax.ShapeDtypeStruct((1, LANES), F32)],
        scratch_shapes=[pltpu.VMEM((1, LANES), F32)],
        compiler_params=_params("arbitrary"),
        name="moe_route",
    )(logits, bias, ltri)


def _row_copy_wait(src_rows, dst_rows, sem):
    pltpu.make_async_copy(src_rows, dst_rows, sem).wait()


def _dispatch_kernel(dest_ref, x_ref, _xs_in_ref, xs_ref, sem, *, tm):
    def issue(t, carry):
        for j in range(2):
            d = dest_ref[2 * t + j]
            pltpu.make_async_copy(x_ref.at[pl.ds(t, 1)], xs_ref.at[pl.ds(d, 1)], sem).start()
        return carry

    lax.fori_loop(0, tm, issue, 0, unroll=8)
    for _ in range(2):
        _row_copy_wait(x_ref, xs_ref.at[pl.ds(0, tm)], sem)


def _dispatch(dest_flat, x, xs_init):
    t, d = x.shape
    tm = ROW_TILE
    return pl.pallas_call(
        functools.partial(_dispatch_kernel, tm=tm),
        grid=(t // tm,),
        in_specs=[pl.BlockSpec((2 * tm,), lambda i: (i,), memory_space=pltpu.SMEM), _rows(tm, d),
                  pl.BlockSpec(memory_space=pl.ANY)],
        out_specs=pl.BlockSpec(memory_space=pl.ANY),
        out_shape=jax.ShapeDtypeStruct(xs_init.shape, F32),
        scratch_shapes=[pltpu.SemaphoreType.DMA(())],
        input_output_aliases={2: 0},
        compiler_params=_params("arbitrary"),
        name="moe_dispatch",
    )(dest_flat, x, xs_init)


def _expert_kernel(blk_e_ref, nused_ref, xs_ref, wgu_ref, wd_ref, y_ref, wgu_bf, wd_bf):
    i = pl.program_id(0)
    prev = blk_e_ref[jnp.maximum(i - 1, 0)]
    new_expert = (i == 0) | (blk_e_ref[i] != prev)

    @pl.when(new_expert)
    def _():
        wgu_bf[...] = wgu_ref[...].astype(BF16)
        wd_bf[...] = wd_ref[...].astype(BF16)

    @pl.when(i < nused_ref[0])
    def _():
        hid = wd_bf.shape[0]
        xb = xs_ref[...].astype(BF16)
        gate = _dot(xb, wgu_bf[:, 0:hid])
        up = _dot(xb, wgu_bf[:, hid:2 * hid])
        y_ref[...] = _dot((_silu(gate) * up).astype(BF16), wd_bf[...])

    @pl.when(i >= nused_ref[0])
    def _():
        y_ref[...] = jnp.zeros(y_ref.shape, F32)


def _experts(blk_e, nused, xs, w_gate_up, w_down, layer):
    nr, d = xs.shape
    bm = MOE_BLOCK_ROWS
    hid2 = w_gate_up.shape[-1]
    hid = w_down.shape[-2]
    grid_spec = pltpu.PrefetchScalarGridSpec(
        num_scalar_prefetch=2,
        grid=(nr // bm,),
        in_specs=[pl.BlockSpec((bm, d), lambda i, be, nu: (jnp.minimum(i, nu[0] - 1), 0)),
                  pl.BlockSpec((None, None, d, hid2), lambda i, be, nu: (layer, be[i], 0, 0)),
                  pl.BlockSpec((None, None, hid, d), lambda i, be, nu: (layer, be[i], 0, 0))],
        out_specs=pl.BlockSpec((bm, d), lambda i, be, nu: (i, 0)),
        scratch_shapes=[pltpu.VMEM((d, hid2), BF16), pltpu.VMEM((hid, d), BF16)],
    )
    return pl.pallas_call(
        _expert_kernel,
        grid_spec=grid_spec,
        out_shape=jax.ShapeDtypeStruct((nr, d), F32),
        compiler_params=_params("arbitrary"),
        name="moe_experts",
    )(blk_e, nused, xs, w_gate_up, w_down)


def _combine_kernel(dest_ref, x_ref, route_ref, g_ref, b_ref, y_hbm, o_ref, ybuf, sem, *, tm):
    def issue(t, carry):
        for j in range(2):
            d = dest_ref[2 * t + j]
            pltpu.make_async_copy(y_hbm.at[pl.ds(d, 1)], ybuf.at[j, pl.ds(t, 1)], sem).start()
        return carry

    lax.fori_loop(0, tm, issue, 0, unroll=8)
    for j in range(2):
        _row_copy_wait(y_hbm.at[pl.ds(0, tm)], ybuf.at[j], sem)
    moe = route_ref[:, 2:3] * ybuf[0] + route_ref[:, 3:4] * ybuf[1]
    o_ref[...] = _layer_norm(ALPHA * x_ref[...] + moe, g_ref[...], b_ref[...])


def _combine(dest_flat, x, route, g, b, yb):
    t, d = x.shape
    tm = ROW_TILE
    return pl.pallas_call(
        functools.partial(_combine_kernel, tm=tm),
        grid=(t // tm,),
        in_specs=[pl.BlockSpec((2 * tm,), lambda i: (i,), memory_space=pltpu.SMEM), _rows(tm, d),
                  _rows(tm, LANES), _full((1, d)), _full((1, d)), pl.BlockSpec(memory_space=pl.ANY)],
        out_specs=_rows(tm, d),
        out_shape=jax.ShapeDtypeStruct((t, d), F32),
        scratch_shapes=[pltpu.VMEM((2, tm, d), F32), pltpu.SemaphoreType.DMA(())],
        compiler_params=_params("arbitrary"),
        name="moe_combine",
    )(dest_flat, x, route, g, b, yb)


def _moe_plan(route, counts):
    bm = MOE_BLOCK_ROWS
    t = route.shape[0]
    n_blocks = -(-2 * t // bm) + MOE_EXPERTS
    e_idx = route[:, 0:2].astype(jnp.int32)
    rank = route[:, 4:6].astype(jnp.int32)
    cnt = counts[0, ROUTE_LANE0:ROUTE_LANE0 + MOE_EXPERTS].astype(jnp.int32)
    padded = ((cnt + bm - 1) // bm) * bm
    pad_end = jnp.cumsum(padded)
    pad_start = pad_end - padded
    dest = (pad_start[e_idx] + rank).reshape(-1)
    nused = pad_end[-1] // bm
    blk = jnp.arange(n_blocks, dtype=jnp.int32)
    blk_e = jnp.minimum(jnp.searchsorted(pad_end, blk * bm, side="right"), MOE_EXPERTS - 1).astype(jnp.int32)
    last_e = blk_e[jnp.maximum(nused - 1, 0)]
    blk_e = jnp.where(blk < nused, blk_e, last_e)
    return dest.astype(jnp.int32), blk_e, nused.reshape(1).astype(jnp.int32), n_blocks


def _hi_lo(w):
    hi = w.astype(BF16)
    return hi, (w - hi.astype(F32)).astype(BF16)


def kernel(x_prompt, x_sample, mem_prompt, cache_mem_k, cache_mem_v, state_gla, state_hgrn, state_conv,
           ln_g, ln_b, a_w_in, a_b_in, a_ln_g, a_ln_b, a_w_s, a_b_s, a_w_out, a_b_out,
           b_w_in, b_w_g2, b_b_g, b_norm_g, b_w_out, c_lb, c_w_in, c_norm_g, c_w_out,
           d_w_in, d_b_in, d_w_dw, d_b_dw, d_ln_g, d_ln_b, d_w_out, d_b_out,
           m_w_q, m_w_k, m_w_v, m_w_o, r_w_grp, r_b_grp, r_w_exp, r_b_exp, e_w_gate_up, e_w_down):
    bp, lp, d = x_prompt.shape
    bs, ls, _ = x_sample.shape
    tp, ts = bp * lp, bs * ls
    t = tp + ts
    mem_len = mem_prompt.shape[1]
    hd = d // MEM_HEADS

    x = jnp.concatenate([x_prompt.reshape(tp, d), x_sample.reshape(ts, d)], axis=0)
    row = lambda a: a.reshape(1, -1)

    mem2d = mem_prompt.reshape(bp * mem_len, d)
    mem_k = _mm_layers(mem2d, m_w_k.astype(BF16))
    mem_v = _mm_layers(mem2d, m_w_v.astype(BF16))
    cache_k = cache_mem_k.reshape(DEPTH, bs, mem_len, d)
    cache_v = cache_mem_v.reshape(DEPTH, bs, mem_len, d)

    lb_all = jnp.cumsum(jax.nn.softmax(c_lb.astype(F32), axis=0), axis=0)
    lb_all = lb_all - lb_all[:1]
    ltri = jnp.tril(jnp.ones((ROW_TILE, ROW_TILE), F32), -1).astype(BF16)
    zero_bias = jnp.zeros((1, d), F32)

    outs = {"v": [], "gla_p": [], "gla_s": [], "hgrn_p": [], "hgrn_s": [], "conv_p": [], "conv_s": []}
    for i in range(DEPTH):
        j = i // N_MIXERS
        kind = i % N_MIXERS
        g1, b1 = row(ln_g[i, 0]), row(ln_b[i, 0])
        if kind == 0:
            tril = jnp.tril(jnp.ones((A_CHUNK, A_CHUNK), bool))
            wc_p = jnp.where(tril, a_w_s[j], 0.0).astype(BF16)
            bc_p = a_b_s[j][:, :, None]
            reps = A_CHUNK // ls
            small = jnp.where(jnp.tril(jnp.ones((ls, ls), bool)), a_w_s[j][:, :ls, :ls], 0.0)
            wc_s = jax.vmap(lambda m: jnp.kron(jnp.eye(reps, dtype=F32), m))(small).astype(BF16)
            bc_s = jnp.tile(a_b_s[j][:, :ls], (1, reps))[:, :, None]
            common = (a_w_in[j].astype(BF16), row(a_b_in[j]), row(a_ln_g[j]), row(a_ln_b[j]))
            tail = (a_w_out[j].astype(BF16), row(a_b_out[j]), g1, b1)
            x1 = _gmlp(x, 0, tp, *common, wc_p, bc_p, *tail)
            x1, v_s = _gmlp(x, tp, ts, *common, wc_s, bc_s, *tail, alias=x1)
            outs["v"].append(v_s.reshape(bs, ls, -1))
        elif kind == 1:
            dk, dv = b_w_g2.shape[-1], b_w_out.shape[1]
            w_in = b_w_in[j]
            w_main = w_in[:, :2 * dk + 2 * dv].astype(BF16)
            w_low = jnp.pad(w_in[:, 2 * dk + 2 * dv:], ((0, 0), (0, LANES - B_GATE_RANK))).astype(BF16)
            g2_hi, g2_lo = _hi_lo(jnp.pad(b_w_g2[j], ((0, LANES - B_GATE_RANK), (0, 0))))
            q, k, v, r, la = _gla_proj(x, w_main, w_low, g2_hi, g2_lo, row(b_b_g[j]), dk, dv)
            ng = row(b_norm_g[j])
            s0_p = jnp.zeros((bp,) + state_gla.shape[2:], F32)
            o, s_p = _linrec(q, k, la, v, r, ng, s0_p, n_batch=bp, seq=lp, row_off=0, heads=B_HEADS)
            o, s_s = _linrec(q, k, la, v, r, ng, state_gla[j], n_batch=bs, seq=ls, row_off=tp,
                             heads=B_HEADS, alias=o)
            outs["gla_p"].append(s_p)
            outs["gla_s"].append(s_s)
            x1 = _mm_ln(o, b_w_out[j].astype(BF16), zero_bias, x, g1, b1, "gla_out")
        elif kind == 2:
            heads = state_hgrn.shape[2]
            q, lf, k, v, gt = _hgrn_proj(x, c_w_in[j].astype(BF16), row(lb_all[i]))
            ng = row(c_norm_g[j])
            s0_p = jnp.zeros((bp,) + state_hgrn.shape[2:], F32)
            o, s_p = _linrec(q, k, lf, v, gt, ng, s0_p, n_batch=bp, seq=lp, row_off=0, heads=heads)
            o, s_s = _linrec(q, k, lf, v, gt, ng, state_hgrn[j], n_batch=bs, seq=ls, row_off=tp,
                             heads=heads, alias=o)
            outs["hgrn_p"].append(s_p)
            outs["hgrn_s"].append(s_s)
            x1 = _mm_ln(o, c_w_out[j].astype(BF16), zero_bias, x, g1, b1, "hgrn_out")
        else:
            h = _glu(x, d_w_in[j].astype(BF16), row(d_b_in[j]))
            cargs = (d_w_dw[j], row(d_b_dw[j]), row(d_ln_g[j]), row(d_ln_b[j]))
            conv0 = jnp.zeros((bp, D_BUF, d), F32)
            c, s_p = _conv(h, conv0, *cargs, n_batch=bp, seq=lp, row_off=0)
            c, s_s = _conv(h, state_conv[j], *cargs, n_batch=bs, seq=ls, row_off=tp, alias=c)
            outs["conv_p"].append(s_p)
            outs["conv_s"].append(s_s)
            x1 = _mm_ln(c, d_w_out[j].astype(BF16), row(d_b_out[j]), x, g1, b1, "conf_out")

        q = _mm_rows(x1, m_w_q[i].astype(BF16), "attn_q")
        att = _attn_prompt(q, mem_k, mem_v, i, bp, lp)
        att = _attn_sample(q, cache_k, cache_v, i, bs, ls, tp, att)
        w_route = jnp.concatenate([r_w_grp[i], r_w_exp[i]], axis=1)
        w_route = jnp.pad(w_route, ((0, 0), (0, LANES - w_route.shape[1])))
        b_route = jnp.pad(jnp.concatenate([r_b_grp[i], r_b_exp[i]]), (0, LANES - MOE_GROUPS - MOE_EXPERTS))
        x2, logits = _mm_ln(att, m_w_o[i].astype(BF16), zero_bias, x1, row(ln_g[i, 1]), row(ln_b[i, 1]),
                            "attn_out", router=_hi_lo(w_route))

        route, counts = _route(logits, row(b_route), ltri)
        dest, blk_e, nused, n_blocks = _moe_plan(route, counts)
        xs = _dispatch(dest, x2, jnp.zeros((n_blocks * MOE_BLOCK_ROWS, d), F32))
        yb = _experts(blk_e, nused, xs, e_w_gate_up, e_w_down, i)
        x = _combine(dest, x2, route, row(ln_g[i, 2]), row(ln_b[i, 2]), yb)

    y_prompt = x[:tp].reshape(bp, lp, d)
    y_sample = x[tp:].reshape(bs, ls, d)
    mem_k_p = mem_k.reshape(DEPTH, bp, mem_len, MEM_HEADS, hd)
    mem_v_p = mem_v.reshape(DEPTH, bp, mem_len, MEM_HEADS, hd)
    return (y_prompt, y_sample, mem_k_p, mem_v_p, jnp.stack(outs["gla_p"]), jnp.stack(outs["hgrn_p"]),
            jnp.stack(outs["conv_p"]), jnp.stack(outs["v"]), jnp.stack(outs["gla_s"]),
            jnp.stack(outs["hgrn_s"]), jnp.stack(outs["conv_s"]))
```

```python
import functools
import math

import jax
import jax.numpy as jnp
from jax import lax
from jax.experimental import pallas as pl
from jax.experimental.pallas import tpu as pltpu

F32 = jnp.float32
BF16 = jnp.bfloat16

D_MODEL = 1024
DEPTH = 4
N_MIXERS = 4
ALPHA = (2.0 * DEPTH) ** 0.25
LN_EPS = 1e-5
A_CHUNK = 128
A_GROUPS = 4
B_HEADS = 4
B_GATE_RANK = 16
B_GATE_TAU = 16.0
C_EXPAND = 128
D_CONV_W = 31
D_BUF = D_CONV_W - 1
LIN_CHUNK = 32
MEM_HEADS = 4
MOE_GROUPS = 4
MOE_PER_GROUP = 8
MOE_EXPERTS = MOE_GROUPS * MOE_PER_GROUP
MOE_HIDDEN = 512

LANES = 128
ROW_TILE = 512
MOE_BLOCK_ROWS = 256
ROUTE_LANE0 = MOE_GROUPS
VMEM_LIMIT = 56 * 1024 * 1024
INV_SQRT2 = 1.0 / math.sqrt(2.0)


def _params(*sem, vmem=VMEM_LIMIT):
    return pltpu.CompilerParams(dimension_semantics=sem, vmem_limit_bytes=vmem)


def _dot(a, b):
    return jnp.dot(a, b, preferred_element_type=F32)


def _dot_nt(a, b):
    return lax.dot_general(a, b, (((1,), (1,)), ((), ())), preferred_element_type=F32)


def _split3(a):
    hi = a.astype(BF16)
    r = a - hi.astype(F32)
    mid = r.astype(BF16)
    lo = (r - mid.astype(F32)).astype(BF16)
    return hi, mid, lo


def _dot_exact_lhs01(m01, a):
    hi, mid, lo = _split3(a)
    return _dot(m01, hi) + _dot(m01, mid) + _dot(m01, lo)


def _dot_hi(a, w_hi, w_lo):
    a_hi = a.astype(BF16)
    a_lo = (a - a_hi.astype(F32)).astype(BF16)
    return _dot(a_hi, w_hi) + _dot(a_lo, w_hi) + _dot(a_hi, w_lo)


def _layer_norm(x, g, b):
    mu = jnp.mean(x, axis=-1, keepdims=True)
    xc = x - mu
    var = jnp.mean(xc * xc, axis=-1, keepdims=True)
    return xc * lax.rsqrt(var + LN_EPS) * g + b


def _sigmoid(x):
    return 1.0 / (1.0 + jnp.exp(-x))


def _silu(x):
    return x * _sigmoid(x)


def _gelu(x):
    return 0.5 * x * (1.0 + lax.erf(x * INV_SQRT2))


def _log_sigmoid(x):
    return jnp.minimum(x, 0.0) - jnp.log(1.0 + jnp.exp(-jnp.abs(x)))


def _full(shape):
    return pl.BlockSpec(shape, lambda *_: (0,) * len(shape))


def _rows(tm, n, off=0):
    return pl.BlockSpec((tm, n), lambda i: (i + off, 0))


def _mm_kernel(a_ref, w_ref, o_ref):
    o_ref[...] = _dot(a_ref[...].astype(BF16), w_ref[...])


def _mm_layers(a, w):
    m, k = a.shape
    nl, _, n = w.shape
    tm = min(m, ROW_TILE)
    return pl.pallas_call(
        _mm_kernel,
        grid=(nl, m // tm),
        in_specs=[pl.BlockSpec((tm, k), lambda l, i: (i, 0)),
                  pl.BlockSpec((None, k, n), lambda l, i: (l, 0, 0))],
        out_specs=pl.BlockSpec((None, tm, n), lambda l, i: (l, i, 0)),
        out_shape=jax.ShapeDtypeStruct((nl, m, n), F32),
        compiler_params=_params("parallel", "parallel"),
        name="mem_kv_proj",
    )(a, w)


def _mm_rows(a, w, name):
    m, k = a.shape
    n = w.shape[1]
    tm = ROW_TILE
    return pl.pallas_call(
        _mm_kernel,
        grid=(m // tm,),
        in_specs=[_rows(tm, k), _full((k, n))],
        out_specs=_rows(tm, n),
        out_shape=jax.ShapeDtypeStruct((m, n), F32),
        compiler_params=_params("parallel"),
        name=name,
    )(a, w)


def _mm_ln_kernel(a_ref, w_ref, bias_ref, x_ref, g_ref, b_ref, *rest, with_logits):
    h = _dot(a_ref[...].astype(BF16), w_ref[...]) + bias_ref[...]
    y = _layer_norm(ALPHA * x_ref[...] + h, g_ref[...], b_ref[...])
    if with_logits:
        wr_hi_ref, wr_lo_ref, o_ref, lg_ref = rest
        lg_ref[...] = _dot_hi(y, wr_hi_ref[...], wr_lo_ref[...])
    else:
        (o_ref,) = rest
    o_ref[...] = y


def _mm_ln(a, w, bias, x, g, b, name, router=None):
    m, k = a.shape
    d = x.shape[1]
    tm = ROW_TILE
    in_specs = [_rows(tm, k), _full((k, d)), _full((1, d)), _rows(tm, d), _full((1, d)), _full((1, d))]
    args = [a, w, bias, x, g, b]
    out_specs = _rows(tm, d)
    out_shape = jax.ShapeDtypeStruct((m, d), F32)
    if router is not None:
        in_specs += [_full((d, LANES)), _full((d, LANES))]
        args += list(router)
        out_specs = [out_specs, _rows(tm, LANES)]
        out_shape = [out_shape, jax.ShapeDtypeStruct((m, LANES), F32)]
    return pl.pallas_call(
        functools.partial(_mm_ln_kernel, with_logits=router is not None),
        grid=(m // tm,),
        in_specs=in_specs,
        out_specs=out_specs,
        out_shape=out_shape,
        compiler_params=_params("parallel"),
        name=name,
    )(*args)


def _gmlp_kernel(x_ref, w_in_ref, b_in_ref, lng_ref, lnb_ref, wc_ref, bc_ref, w_out_ref, b_out_ref,
                 g_ref, b_ref, *rest, emit_v, n_chunks):
    if emit_v:
        _alias_ref, o_ref, v_ref, vn_ref = rest
    else:
        o_ref, vn_ref = rest
    half = w_out_ref.shape[0]
    gw = half // A_GROUPS
    x = x_ref[...]
    xb = x.astype(BF16)
    v = _gelu(_dot(xb, w_in_ref[:, half:]) + b_in_ref[:, half:])
    vn = _layer_norm(v, lng_ref[...], lnb_ref[...])
    vn_ref[...] = vn
    if emit_v:
        v_ref[...] = vn
    acc = jnp.zeros(x.shape, F32)
    for grp in range(A_GROUPS):
        cols = slice(grp * gw, (grp + 1) * gw)
        u = _gelu(_dot(xb, w_in_ref[:, cols]) + b_in_ref[:, cols])
        mixed = []
        for c in range(n_chunks):
            vc = vn_ref[c * A_CHUNK:(c + 1) * A_CHUNK, cols].astype(BF16)
            mixed.append(_dot(wc_ref[grp], vc) + bc_ref[grp])
        mixed = mixed[0] if n_chunks == 1 else jnp.concatenate(mixed, axis=0)
        acc = acc + _dot((u * mixed).astype(BF16), w_out_ref[cols, :])
    h = acc + b_out_ref[...]
    o_ref[...] = _layer_norm(ALPHA * x + h, g_ref[...], b_ref[...])


def _gmlp(x, row_off, n_rows, w_in, b_in, lng, lnb, wc, bc, w_out, b_out, g, b, alias=None):
    t, d = x.shape
    ffn = w_in.shape[1]
    half = ffn // 2
    tm = 2 * A_CHUNK
    emit_v = alias is not None
    off = row_off // tm
    in_specs = [_rows(tm, d, off), _full((d, ffn)), _full((1, ffn)), _full((1, half)), _full((1, half)),
                _full((A_GROUPS, A_CHUNK, A_CHUNK)), _full((A_GROUPS, A_CHUNK, 1)), _full((half, d)),
                _full((1, d)), _full((1, d)), _full((1, d))]
    args = [x, w_in, b_in, lng, lnb, wc, bc, w_out, b_out, g, b]
    out_specs = _rows(tm, d, off)
    out_shape = jax.ShapeDtypeStruct((t, d), F32)
    aliases = {}
    if emit_v:
        in_specs.append(pl.BlockSpec(memory_space=pl.ANY))
        args.append(alias)
        aliases = {len(args) - 1: 0}
        out_specs = [out_specs, _rows(tm, half)]
        out_shape = [out_shape, jax.ShapeDtypeStruct((n_rows, half), F32)]
    return pl.pallas_call(
        functools.partial(_gmlp_kernel, emit_v=emit_v, n_chunks=tm // A_CHUNK),
        grid=(n_rows // tm,),
        in_specs=in_specs,
        out_specs=out_specs,
        out_shape=out_shape,
        scratch_shapes=[pltpu.VMEM((tm, half), F32)],
        input_output_aliases=aliases,
        compiler_params=_params("parallel"),
        name="gmlp_sample" if emit_v else "gmlp_prompt",
    )(*args)


def _gla_proj_kernel(x_ref, w_ref, wlow_ref, g2_hi_ref, g2_lo_ref, bg_ref,
                     q_ref, k_ref, v_ref, r_ref, la_ref, *, dk, dv, q_scale):
    xb = x_ref[...].astype(BF16)
    q_ref[...] = _dot(xb, w_ref[:, 0:dk]) * q_scale
    k_ref[...] = _dot(xb, w_ref[:, dk:2 * dk])
    v_ref[...] = _dot(xb, w_ref[:, 2 * dk:2 * dk + dv])
    r_ref[...] = _dot(xb, w_ref[:, 2 * dk + dv:2 * dk + 2 * dv])
    g_low = _dot(xb, wlow_ref[...])
    pre = _dot_hi(g_low, g2_hi_ref[...], g2_lo_ref[...]) + bg_ref[...]
    la_ref[...] = _log_sigmoid(pre) * (1.0 / B_GATE_TAU)


def _gla_proj(x, w_main, w_low, g2_hi, g2_lo, bg, dk, dv):
    t, d = x.shape
    tm = ROW_TILE
    shapes = [dk, dk, dv, dv, dk]
    return pl.pallas_call(
        functools.partial(_gla_proj_kernel, dk=dk, dv=dv, q_scale=(dk // B_HEADS) ** -0.5),
        grid=(t // tm,),
        in_specs=[_rows(tm, d), _full(w_main.shape), _full(w_low.shape), _full(g2_hi.shape),
                  _full(g2_lo.shape), _full((1, dk))],
        out_specs=[_rows(tm, n) for n in shapes],
        out_shape=[jax.ShapeDtypeStruct((t, n), F32) for n in shapes],
        compiler_params=_params("parallel"),
        name="gla_proj",
    )(x, w_main, w_low, g2_hi, g2_lo, bg)


def _hgrn_proj_kernel(x_ref, w_ref, lb_ref, q_ref, lf_ref, k_ref, v_ref, gt_ref, *, d, q_scale):
    xb = x_ref[...].astype(BF16)
    lb = lb_ref[...]
    q_ref[...] = _silu(_dot(xb, w_ref[:, 0:d])) * q_scale
    f = _dot(xb, w_ref[:, d:2 * d])
    lf_ref[...] = jnp.log(lb + (1.0 - lb) * _sigmoid(f))
    k_ref[...] = (1.0 - lb) * _sigmoid(-f)
    v_ref[...] = _dot(xb, w_ref[:, 2 * d:3 * d])
    gt_ref[...] = _dot(xb, w_ref[:, 3 * d:4 * d])


def _hgrn_proj(x, w, lb):
    t, d = x.shape
    tm = ROW_TILE
    return pl.pallas_call(
        functools.partial(_hgrn_proj_kernel, d=d, q_scale=C_EXPAND ** -0.5),
        grid=(t // tm,),
        in_specs=[_rows(tm, d), _full(w.shape), _full((1, d))],
        out_specs=[_rows(tm, d)] * 5,
        out_shape=[jax.ShapeDtypeStruct((t, d), F32)] * 5,
        compiler_params=_params("parallel"),
        name="hgrn_proj",
    )(x, w, lb)


def _cumsum_rows(x, chunk):
    pos = lax.broadcasted_iota(jnp.int32, x.shape, 0) & (chunk - 1)
    step = 1
    while step < chunk:
        x = x + jnp.where(pos >= step, pltpu.roll(x, step, axis=0), 0.0)
        step *= 2
    return x


def _chunk_rows(x, row, chunk, n_chunks):
    parts = [jnp.broadcast_to(x[c * chunk + row:c * chunk + row + 1, :], (chunk, x.shape[1]))
             for c in range(n_chunks)]
    return parts[0] if n_chunks == 1 else jnp.concatenate(parts, axis=0)


def _linrec_kernel(q_ref, k_ref, g_ref, v_ref, gate_ref, ng_ref, s0_ref, *rest,
                   heads, dk, dv, chunk, n_chunks, nb, aliased):
    if aliased:
        _alias_ref, o_ref, sout_ref, st_ref = rest
    else:
        o_ref, sout_ref, st_ref = rest
    li = pl.program_id(1)
    tl = chunk * n_chunks
    r_id = lax.broadcasted_iota(jnp.int32, (tl, tl), 0)
    c_id = lax.broadcasted_iota(jnp.int32, (tl, tl), 1)
    mask = (r_id >= c_id) & ((r_id // chunk) == (c_id // chunk))

    for n in range(nb):
        @pl.when(li == 0)
        def _():
            st_ref[...] = s0_ref[n]

        rs = slice(n * tl, (n + 1) * tl)
        for h in range(heads):
            ks = slice(h * dk, (h + 1) * dk)
            vs = slice(h * dv, (h + 1) * dv)
            bcum = _cumsum_rows(g_ref[rs, ks], chunk)
            b_mid = _chunk_rows(bcum, chunk // 2 - 1, chunk, n_chunks)
            b_end = _chunk_rows(bcum, chunk - 1, chunk, n_chunks)
            q = q_ref[rs, ks]
            k = k_ref[rs, ks]
            vb = v_ref[rs, vs].astype(BF16)
            q_in = (q * jnp.exp(bcum)).astype(BF16)
            q_a = (q * jnp.exp(bcum - b_mid)).astype(BF16)
            k_a = (k * jnp.exp(b_mid - bcum)).astype(BF16)
            k_end = k * jnp.exp(b_end - bcum)
            a = jnp.where(mask, _dot_nt(q_a, k_a), 0.0).astype(BF16)
            o_intra = _dot(a, vb)
            st = st_ref[h]
            outs = []
            for c in range(n_chunks):
                cr = slice(c * chunk, (c + 1) * chunk)
                outs.append(o_intra[cr] + _dot(q_in[cr], st.astype(BF16)))
                decay = jnp.exp(bcum[cr].T[:, chunk - 1:chunk])
                st = st * decay + _dot(k_end[cr].T.astype(BF16), vb[cr])
            st_ref[h] = st
            o = outs[0] if n_chunks == 1 else jnp.concatenate(outs, axis=0)
            o = o * lax.rsqrt(jnp.mean(o * o, axis=-1, keepdims=True) + LN_EPS) * ng_ref[:, vs]
            o_ref[rs, vs] = o * _silu(gate_ref[rs, vs])
        sout_ref[n] = st_ref[...]


LINREC_CHUNK = 2 * LIN_CHUNK
LINREC_SAMPLE_NB = 4


def _linrec(q, k, g, v, gate, ng, s0, *, n_batch, seq, row_off, heads, alias=None):
    t, hk = q.shape
    hv = v.shape[1]
    dk, dv = hk // heads, hv // heads
    chunk = LINREC_CHUNK if seq % LINREC_CHUNK == 0 else seq
    tl = min(seq, 4 * chunk)
    n_l = seq // tl
    nb = LINREC_SAMPLE_NB if n_l == 1 else 1
    off = row_off // (nb * tl)

    def rows(n):
        return pl.BlockSpec((nb * tl, n), lambda b, l: (off + b * n_l + l, 0))

    state_spec = pl.BlockSpec((nb, heads, dk, dv), lambda b, l: (b, 0, 0, 0))
    in_specs = [rows(hk), rows(hk), rows(hk), rows(hv), rows(hv), pl.BlockSpec((1, hv), lambda b, l: (0, 0)),
                state_spec]
    args = [q, k, g, v, gate, ng, s0]
    aliases = {}
    if alias is not None:
        in_specs.append(pl.BlockSpec(memory_space=pl.ANY))
        args.append(alias)
        aliases = {len(args) - 1: 0}
    return pl.pallas_call(
        functools.partial(_linrec_kernel, heads=heads, dk=dk, dv=dv, chunk=chunk, n_chunks=tl // chunk, nb=nb,
                          aliased=alias is not None),
        grid=(n_batch // nb, n_l),
        in_specs=in_specs,
        out_specs=[rows(hv), state_spec],
        out_shape=[jax.ShapeDtypeStruct((t, hv), F32), jax.ShapeDtypeStruct((n_batch, heads, dk, dv), F32)],
        scratch_shapes=[pltpu.VMEM((heads, dk, dv), F32)],
        input_output_aliases=aliases,
        compiler_params=_params("parallel", "arbitrary"),
        name="linrec_h%d_%s" % (heads, "sample" if alias is not None else "prompt"),
    )(*args)


def _glu_kernel(x_ref, w_ref, b_ref, o_ref, *, d):
    xb = x_ref[...].astype(BF16)
    a = _dot(xb, w_ref[:, 0:d]) + b_ref[:, 0:d]
    gate = _dot(xb, w_ref[:, d:2 * d]) + b_ref[:, d:2 * d]
    o_ref[...] = a * _sigmoid(gate)


def _glu(x, w, b):
    t, d = x.shape
    tm = ROW_TILE
    return pl.pallas_call(
        functools.partial(_glu_kernel, d=d),
        grid=(t // tm,),
        in_specs=[_rows(tm, d), _full(w.shape), _full(b.shape)],
        out_specs=_rows(tm, d),
        out_shape=jax.ShapeDtypeStruct((t, d), F32),
        compiler_params=_params("parallel"),
        name="conf_glu",
    )(x, w, b)


CONV_PAD = 32


def _conv_kernel(h_ref, st_ref, wdw_ref, bdw_ref, lg_ref, lb_ref, *rest, nb, tl, aliased):
    if aliased:
        _alias_ref, o_ref, sout_ref, buf_ref, conv_ref = rest
    else:
        o_ref, sout_ref, buf_ref, conv_ref = rest
    li = pl.program_id(1)
    d = h_ref.shape[-1]
    lead = CONV_PAD - D_BUF
    rb = min(tl, 32)
    cw = 512
    for n in range(nb):
        @pl.when(li == 0)
        def _():
            buf_ref[lead:CONV_PAD, :] = st_ref[n]

        buf_ref[CONV_PAD:CONV_PAD + tl, :] = h_ref[n * tl:(n + 1) * tl, :]
        for r0 in range(0, tl, rb):
            for c0 in range(0, d, cw):
                cols = slice(c0, c0 + cw)
                acc = jnp.zeros((rb, cw), F32)
                for j in range(D_CONV_W):
                    acc = acc + wdw_ref[j:j + 1, cols] * buf_ref[r0 + j + lead:r0 + j + lead + rb, cols]
                conv_ref[r0:r0 + rb, cols] = acc + bdw_ref[:, cols]
        y = _layer_norm(conv_ref[...], lg_ref[...], lb_ref[...])
        o_ref[n * tl:(n + 1) * tl, :] = _silu(y)
        sout_ref[n] = buf_ref[tl + lead:tl + CONV_PAD, :]
        buf_ref[0:CONV_PAD, :] = buf_ref[tl:tl + CONV_PAD, :]


def _conv(h, state, wdw, bdw, lg, lb, *, n_batch, seq, row_off, alias=None):
    t, d = h.shape
    tl = min(seq, 256)
    n_l = seq // tl
    nb = 8 if n_l == 1 else 1
    off = row_off // (nb * tl)
    rows = pl.BlockSpec((nb * tl, d), lambda b, l: (off + b * n_l + l, 0))
    state_spec = pl.BlockSpec((nb, D_BUF, d), lambda b, l: (b, 0, 0))
    in_specs = [rows, state_spec, _full(wdw.shape), _full((1, d)), _full((1, d)), _full((1, d))]
    args = [h, state, wdw, bdw, lg, lb]
    aliases = {}
    if alias is not None:
        in_specs.append(pl.BlockSpec(memory_space=pl.ANY))
        args.append(alias)
        aliases = {len(args) - 1: 0}
    return pl.pallas_call(
        functools.partial(_conv_kernel, nb=nb, tl=tl, aliased=alias is not None),
        grid=(n_batch // nb, n_l),
        in_specs=in_specs,
        out_specs=[rows, state_spec],
        out_shape=[jax.ShapeDtypeStruct((t, d), F32), jax.ShapeDtypeStruct((n_batch, D_BUF, d), F32)],
        scratch_shapes=[pltpu.VMEM((CONV_PAD + tl, d), F32), pltpu.VMEM((tl, d), F32)],
        input_output_aliases=aliases,
        compiler_params=_params("parallel", "arbitrary"),
        name="conf_conv_sample" if alias is not None else "conf_conv_prompt",
    )(*args)


def _attn_heads(q, k_of, v_of, hd):
    outs = []
    for h in range(MEM_HEADS):
        hs = slice(h * hd, (h + 1) * hd)
        s = _dot_nt(q[:, hs].astype(BF16), k_of(hs).astype(BF16)) * (hd ** -0.5)
        p = jnp.exp(s - jnp.max(s, axis=-1, keepdims=True))
        denom = jnp.sum(p, axis=-1, keepdims=True)
        outs.append(_dot(p.astype(BF16), v_of(hs).astype(BF16)) / denom)
    return outs


def _attn_prompt_kernel(q_ref, k_ref, v_ref, o_ref):
    hd = q_ref.shape[-1] // MEM_HEADS
    outs = _attn_heads(q_ref[...], lambda hs: k_ref[:, hs], lambda hs: v_ref[:, hs], hd)
    for h, o in enumerate(outs):
        o_ref[:, h * hd:(h + 1) * hd] = o


def _attn_prompt(q, mem_k, mem_v, layer, n_batch, seq):
    t, d = q.shape
    m = mem_k.shape[1] // n_batch
    tl = ROW_TILE
    n_l = seq // tl
    rows = pl.BlockSpec((tl, d), lambda b, l: (b * n_l + l, 0))
    kv = pl.BlockSpec((None, m, d), lambda b, l: (layer, b, 0))
    return pl.pallas_call(
        _attn_prompt_kernel,
        grid=(n_batch, n_l),
        in_specs=[rows, kv, kv],
        out_specs=rows,
        out_shape=jax.ShapeDtypeStruct((t, d), F32),
        compiler_params=_params("parallel", "parallel"),
        name="attn_prompt",
    )(q, mem_k, mem_v)


ATTN_SAMPLE_NB = 4


def _attn_sample_kernel(q_ref, k_ref, v_ref, _alias_ref, o_ref, *, nb, seq):
    hd = q_ref.shape[-1] // MEM_HEADS
    for n in range(nb):
        rs = slice(n * seq, (n + 1) * seq)
        m = k_ref.shape[1] // (MEM_HEADS * (hd // LANES))

        def head_of(ref):
            def get(hs):
                h = hs.start // hd
                tiles = [ref[n, pl.ds(lt * MEM_HEADS + h, m, stride=MEM_HEADS * (hd // LANES)), :]
                         for lt in range(hd // LANES)]
                return jnp.concatenate(tiles, axis=1)
            return get

        outs = _attn_heads(q_ref[rs, :], head_of(k_ref), head_of(v_ref), hd)
        for h, o in enumerate(outs):
            o_ref[rs, h * hd:(h + 1) * hd] = o


def _cache_rows(cache):
    nl, nbat, m, heads, hd = cache.shape
    c = cache.reshape(nl, nbat, m, heads, hd // LANES, LANES)
    return c.transpose(0, 1, 2, 4, 3, 5).reshape(nl, nbat, m * (hd // LANES) * heads, LANES)


def _attn_sample(q, cache_k, cache_v, layer, n_batch, seq, row_off, alias):
    t, d = q.shape
    nb = ATTN_SAMPLE_NB
    off = row_off // (nb * seq)
    rows = pl.BlockSpec((nb * seq, d), lambda i: (off + i, 0))
    kv = pl.BlockSpec((None, nb) + cache_k.shape[2:], lambda i: (layer, i, 0, 0))
    return pl.pallas_call(
        functools.partial(_attn_sample_kernel, nb=nb, seq=seq),
        grid=(n_batch // nb,),
        in_specs=[rows, kv, kv, pl.BlockSpec(memory_space=pl.ANY)],
        out_specs=rows,
        out_shape=jax.ShapeDtypeStruct((t, d), F32),
        input_output_aliases={3: 0},
        compiler_params=_params("parallel"),
        name="attn_sample",
    )(q, cache_k, cache_v, alias)


def _route_kernel(lg_ref, bias_ref, ltri_ref, route_ref, counts_ref, carry_ref):
    i = pl.program_id(0)

    @pl.when(i == 0)
    def _():
        carry_ref[...] = jnp.zeros(carry_ref.shape, F32)

    z = lg_ref[...] + bias_ref[...]
    lane = lax.broadcasted_iota(jnp.int32, z.shape, 1).astype(F32)
    neg = -jnp.inf
    far = float(LANES)

    def first_max(mask):
        vmax = jnp.max(jnp.where(mask, z, neg), axis=-1, keepdims=True)
        idx = jnp.min(jnp.where(mask & (z == vmax), lane, far), axis=-1, keepdims=True)
        return vmax, idx

    gmask = lane < float(MOE_GROUPS)
    gmax, gidx = first_max(gmask)
    gsum = jnp.sum(jnp.where(gmask, jnp.exp(z - gmax), 0.0), axis=-1, keepdims=True)
    g_w = 1.0 / gsum
    lo = float(ROUTE_LANE0) + float(MOE_PER_GROUP) * gidx
    emask = (lane >= lo) & (lane < lo + float(MOE_PER_GROUP))
    v1, i1 = first_max(emask)
    v2, i2 = first_max(emask & (lane != i1))
    tt = jnp.exp(v2 - v1)
    w0 = g_w / (1.0 + tt)
    w1 = g_w * tt / (1.0 + tt)
    sel1 = lane == i1
    sel2 = lane == i2
    onehot = jnp.where(sel1 | sel2, 1.0, 0.0)
    before = _dot(ltri_ref[...], onehot.astype(BF16)) + carry_ref[...]
    rank0 = jnp.sum(jnp.where(sel1, before, 0.0), axis=-1, keepdims=True)
    rank1 = jnp.sum(jnp.where(sel2, before, 0.0), axis=-1, keepdims=True)
    carry = carry_ref[...] + jnp.sum(onehot, axis=0, keepdims=True)
    carry_ref[...] = carry
    counts_ref[...] = carry
    e_off = float(ROUTE_LANE0)
    out = jnp.zeros(z.shape, F32)
    for ln, val in enumerate((i1 - e_off, i2 - e_off, w0, w1, rank0, rank1)):
        out = jnp.where(lane == float(ln), val, out)
    route_ref[...] = out


def _route(logits, bias, ltri):
    t = logits.shape[0]
    tm = ROW_TILE
    return pl.pallas_call(
        _route_kernel,
        grid=(t // tm,),
        in_specs=[_rows(tm, LANES), _full((1, LANES)), _full((tm, tm))],
        out_specs=[_rows(tm, LANES), _full((1, LANES))],
        out_shape=[jax.ShapeDtypeStruct((t, LANES), F32), jax.ShapeDtypeStruct((1, LANES), F32)],
        scratch_shapes=[pltpu.VMEM((1, LANES), F32)],
        compiler_params=_params("arbitrary"),
        name="moe_route",
    )(logits, bias, ltri)


def _row_copy_wait(src_rows, dst_rows, sem):
    pltpu.make_async_copy(src_rows, dst_rows, sem).wait()


def _dispatch_kernel(tail_ref, dest_ref, x_ref, xs_ref, zeros_ref, sem, *, tm):
    @pl.when(pl.program_id(0) == 0)
    def _():
        zeros_ref[...] = jnp.zeros(zeros_ref.shape, F32)
        bm = zeros_ref.shape[0]
        tails = [xs_ref.at[pl.ds(pl.multiple_of(tail_ref[e], bm), bm)] for e in range(MOE_EXPERTS)]
        for dst in tails:
            pltpu.make_async_copy(zeros_ref, dst, sem).start()
        for dst in tails:
            pltpu.make_async_copy(zeros_ref, dst, sem).wait()

    def issue(t, carry):
        for j in range(2):
            d = dest_ref[2 * t + j]
            pltpu.make_async_copy(x_ref.at[pl.ds(t, 1)], xs_ref.at[pl.ds(d, 1)], sem).start(priority=j)
        return carry

    lax.fori_loop(0, tm, issue, 0, unroll=8)
    for _ in range(2):
        _row_copy_wait(x_ref, xs_ref.at[pl.ds(0, tm)], sem)


def _dispatch(tail_rows, dest_flat, x, n_rows):
    t, d = x.shape
    tm = ROW_TILE
    grid_spec = pltpu.PrefetchScalarGridSpec(
        num_scalar_prefetch=1,
        grid=(t // tm,),
        in_specs=[pl.BlockSpec((2 * tm,), lambda i, tail: (i,), memory_space=pltpu.SMEM),
                  pl.BlockSpec((tm, d), lambda i, tail: (i, 0))],
        out_specs=pl.BlockSpec(memory_space=pl.ANY),
        scratch_shapes=[pltpu.VMEM((MOE_BLOCK_ROWS, d), F32), pltpu.SemaphoreType.DMA(())],
    )
    return pl.pallas_call(
        functools.partial(_dispatch_kernel, tm=tm),
        grid_spec=grid_spec,
        out_shape=jax.ShapeDtypeStruct((n_rows, d), F32),
        compiler_params=_params("arbitrary"),
        name="moe_dispatch",
    )(tail_rows, dest_flat, x)


def _expert_kernel(blk_e_ref, nused_ref, xs_ref, wgu_ref, wd_ref, y_ref, wgu_bf, wd_bf):
    i = pl.program_id(0)
    prev = blk_e_ref[jnp.maximum(i - 1, 0)]
    new_expert = (i == 0) | (blk_e_ref[i] != prev)

    @pl.when(new_expert)
    def _():
        wgu_bf[...] = wgu_ref[...].astype(BF16)
        wd_bf[...] = wd_ref[...].astype(BF16)

    @pl.when(i < nused_ref[0])
    def _():
        hid = wd_bf.shape[0]
        xb = xs_ref[...].astype(BF16)
        gate = _dot(xb, wgu_bf[:, 0:hid])
        up = _dot(xb, wgu_bf[:, hid:2 * hid])
        y_ref[...] = _dot((_silu(gate) * up).astype(BF16), wd_bf[...])

    @pl.when(i >= nused_ref[0])
    def _():
        y_ref[...] = jnp.zeros(y_ref.shape, F32)


def _experts(blk_e, nused, xs, w_gate_up, w_down, layer):
    nr, d = xs.shape
    bm = MOE_BLOCK_ROWS
    hid2 = w_gate_up.shape[-1]
    hid = w_down.shape[-2]
    grid_spec = pltpu.PrefetchScalarGridSpec(
        num_scalar_prefetch=2,
        grid=(nr // bm,),
        in_specs=[pl.BlockSpec((bm, d), lambda i, be, nu: (jnp.minimum(i, nu[0] - 1), 0)),
                  pl.BlockSpec((None, None, d, hid2), lambda i, be, nu: (layer, be[i], 0, 0)),
                  pl.BlockSpec((None, None, hid, d), lambda i, be, nu: (layer, be[i], 0, 0))],
        out_specs=pl.BlockSpec((bm, d), lambda i, be, nu: (i, 0)),
        scratch_shapes=[pltpu.VMEM((d, hid2), BF16), pltpu.VMEM((hid, d), BF16)],
    )
    return pl.pallas_call(
        _expert_kernel,
        grid_spec=grid_spec,
        out_shape=jax.ShapeDtypeStruct((nr, d), F32),
        compiler_params=_params("arbitrary"),
        name="moe_experts",
    )(blk_e, nused, xs, w_gate_up, w_down)


def _combine_kernel(dest_ref, x_ref, route_ref, g_ref, b_ref, y_hbm, o_ref, ybuf, sem, *, tm):
    def issue(t, carry):
        for j in range(2):
            d = dest_ref[2 * t + j]
            pltpu.make_async_copy(y_hbm.at[pl.ds(d, 1)], ybuf.at[j, pl.ds(t, 1)], sem).start(priority=j)
        return carry

    lax.fori_loop(0, tm, issue, 0, unroll=8)
    for j in range(2):
        _row_copy_wait(y_hbm.at[pl.ds(0, tm)], ybuf.at[j], sem)
    moe = route_ref[:, 2:3] * ybuf[0] + route_ref[:, 3:4] * ybuf[1]
    o_ref[...] = _layer_norm(ALPHA * x_ref[...] + moe, g_ref[...], b_ref[...])


def _combine(dest_flat, x, route, g, b, yb):
    t, d = x.shape
    tm = ROW_TILE
    return pl.pallas_call(
        functools.partial(_combine_kernel, tm=tm),
        grid=(t // tm,),
        in_specs=[pl.BlockSpec((2 * tm,), lambda i: (i,), memory_space=pltpu.SMEM), _rows(tm, d),
                  _rows(tm, LANES), _full((1, d)), _full((1, d)), pl.BlockSpec(memory_space=pl.ANY)],
        out_specs=_rows(tm, d),
        out_shape=jax.ShapeDtypeStruct((t, d), F32),
        scratch_shapes=[pltpu.VMEM((2, tm, d), F32), pltpu.SemaphoreType.DMA(())],
        compiler_params=_params("arbitrary"),
        name="moe_combine",
    )(dest_flat, x, route, g, b, yb)


def _moe_plan(route, counts):
    bm = MOE_BLOCK_ROWS
    t = route.shape[0]
    n_blocks = -(-2 * t // bm) + MOE_EXPERTS
    e_idx = route[:, 0:2].astype(jnp.int32)
    rank = route[:, 4:6].astype(jnp.int32)
    cnt = counts[0, ROUTE_LANE0:ROUTE_LANE0 + MOE_EXPERTS].astype(jnp.int32)
    padded = ((cnt + bm - 1) // bm) * bm
    pad_end = jnp.cumsum(padded)
    pad_start = pad_end - padded
    experts = jnp.arange(MOE_EXPERTS, dtype=jnp.int32)
    dest = (rank + jnp.sum(jnp.where(e_idx[..., None] == experts, pad_start, 0), axis=-1)).reshape(-1)
    nused = pad_end[-1] // bm
    blk = jnp.arange(n_blocks, dtype=jnp.int32)
    blk_e = jnp.sum((pad_end[None, :] <= (blk * bm)[:, None]).astype(jnp.int32), axis=1)
    blk_e = jnp.minimum(blk_e, MOE_EXPERTS - 1)
    last_e = jnp.max(jnp.where(cnt > 0, experts, 0))
    blk_e = jnp.where(blk < nused, blk_e, last_e)
    empty = cnt == 0
    tail_rows = jnp.where(empty, (nused + jnp.cumsum(empty.astype(jnp.int32)) - 1) * bm, pad_end - bm)
    return (dest.astype(jnp.int32), blk_e.astype(jnp.int32), nused.reshape(1).astype(jnp.int32),
            tail_rows.astype(jnp.int32), n_blocks)


def _hi_lo(w):
    hi = w.astype(BF16)
    return hi, (w - hi.astype(F32)).astype(BF16)


def kernel(x_prompt, x_sample, mem_prompt, cache_mem_k, cache_mem_v, state_gla, state_hgrn, state_conv,
           ln_g, ln_b, a_w_in, a_b_in, a_ln_g, a_ln_b, a_w_s, a_b_s, a_w_out, a_b_out,
           b_w_in, b_w_g2, b_b_g, b_norm_g, b_w_out, c_lb, c_w_in, c_norm_g, c_w_out,
           d_w_in, d_b_in, d_w_dw, d_b_dw, d_ln_g, d_ln_b, d_w_out, d_b_out,
           m_w_q, m_w_k, m_w_v, m_w_o, r_w_grp, r_b_grp, r_w_exp, r_b_exp, e_w_gate_up, e_w_down):
    bp, lp, d = x_prompt.shape
    bs, ls, _ = x_sample.shape
    tp, ts = bp * lp, bs * ls
    t = tp + ts
    mem_len = mem_prompt.shape[1]
    hd = d // MEM_HEADS

    x = jnp.concatenate([x_prompt.reshape(tp, d), x_sample.reshape(ts, d)], axis=0)
    row = lambda a: a.reshape(1, -1)

    mem2d = mem_prompt.reshape(bp * mem_len, d)
    mem_k = _mm_layers(mem2d, m_w_k.astype(BF16))
    mem_v = _mm_layers(mem2d, m_w_v.astype(BF16))

    cache_k = _cache_rows(cache_mem_k)
    cache_v = _cache_rows(cache_mem_v)

    lb_all = jnp.cumsum(jax.nn.softmax(c_lb.astype(F32), axis=0), axis=0)
    lb_all = lb_all - lb_all[:1]
    ltri = jnp.tril(jnp.ones((ROW_TILE, ROW_TILE), F32), -1).astype(BF16)
    zero_bias = jnp.zeros((1, d), F32)

    outs = {"v": [], "gla_p": [], "gla_s": [], "hgrn_p": [], "hgrn_s": [], "conv_p": [], "conv_s": []}
    for i in range(DEPTH):
        j = i // N_MIXERS
        kind = i % N_MIXERS
        g1, b1 = row(ln_g[i, 0]), row(ln_b[i, 0])
        if kind == 0:
            tril = jnp.tril(jnp.ones((A_CHUNK, A_CHUNK), bool))
            wc_p = jnp.where(tril, a_w_s[j], 0.0).astype(BF16)
            bc_p = a_b_s[j][:, :, None]
            reps = A_CHUNK // ls
            small = jnp.where(jnp.tril(jnp.ones((ls, ls), bool)), a_w_s[j][:, :ls, :ls], 0.0)
            wc_s = jax.vmap(lambda m: jnp.kron(jnp.eye(reps, dtype=F32), m))(small).astype(BF16)
            bc_s = jnp.tile(a_b_s[j][:, :ls], (1, reps))[:, :, None]
            common = (a_w_in[j].astype(BF16), row(a_b_in[j]), row(a_ln_g[j]), row(a_ln_b[j]))
            tail = (a_w_out[j].astype(BF16), row(a_b_out[j]), g1, b1)
            x1 = _gmlp(x, 0, tp, *common, wc_p, bc_p, *tail)
            x1, v_s = _gmlp(x, tp, ts, *common, wc_s, bc_s, *tail, alias=x1)
            outs["v"].append(v_s.reshape(bs, ls, -1))
        elif kind == 1:
            dk, dv = b_w_g2.shape[-1], b_w_out.shape[1]
            w_in = b_w_in[j]
            w_main = w_in[:, :2 * dk + 2 * dv].astype(BF16)
            w_low = jnp.pad(w_in[:, 2 * dk + 2 * dv:], ((0, 0), (0, LANES - B_GATE_RANK))).astype(BF16)
            g2_hi, g2_lo = _hi_lo(jnp.pad(b_w_g2[j], ((0, LANES - B_GATE_RANK), (0, 0))))
            q, k, v, r, la = _gla_proj(x, w_main, w_low, g2_hi, g2_lo, row(b_b_g[j]), dk, dv)
            ng = row(b_norm_g[j])
            s0_p = jnp.zeros((bp,) + state_gla.shape[2:], F32)
            o, s_p = _linrec(q, k, la, v, r, ng, s0_p, n_batch=bp, seq=lp, row_off=0, heads=B_HEADS)
            o, s_s = _linrec(q, k, la, v, r, ng, state_gla[j], n_batch=bs, seq=ls, row_off=tp,
                             heads=B_HEADS, alias=o)
            outs["gla_p"].append(s_p)
            outs["gla_s"].append(s_s)
            x1 = _mm_ln(o, b_w_out[j].astype(BF16), zero_bias, x, g1, b1, "gla_out")
        elif kind == 2:
            heads = state_hgrn.shape[2]
            q, lf, k, v, gt = _hgrn_proj(x, c_w_in[j].astype(BF16), row(lb_all[i]))
            ng = row(c_norm_g[j])
            s0_p = jnp.zeros((bp,) + state_hgrn.shape[2:], F32)
            o, s_p = _linrec(q, k, lf, v, gt, ng, s0_p, n_batch=bp, seq=lp, row_off=0, heads=heads)
            o, s_s = _linrec(q, k, lf, v, gt, ng, state_hgrn[j], n_batch=bs, seq=ls, row_off=tp,
                             heads=heads, alias=o)
            outs["hgrn_p"].append(s_p)
            outs["hgrn_s"].append(s_s)
            x1 = _mm_ln(o, c_w_out[j].astype(BF16), zero_bias, x, g1, b1, "hgrn_out")
        else:
            h = _glu(x, d_w_in[j].astype(BF16), row(d_b_in[j]))
            cargs = (d_w_dw[j], row(d_b_dw[j]), row(d_ln_g[j]), row(d_ln_b[j]))
            conv0 = jnp.zeros((bp, D_BUF, d), F32)
            c, s_p = _conv(h, conv0, *cargs, n_batch=bp, seq=lp, row_off=0)
            c, s_s = _conv(h, state_conv[j], *cargs, n_batch=bs, seq=ls, row_off=tp, alias=c)
            outs["conv_p"].append(s_p)
            outs["conv_s"].append(s_s)
            x1 = _mm_ln(c, d_w_out[j].astype(BF16), row(d_b_out[j]), x, g1, b1, "conf_out")

        q = _mm_rows(x1, m_w_q[i].astype(BF16), "attn_q")
        att = _attn_prompt(q, mem_k, mem_v, i, bp, lp)
        att = _attn_sample(q, cache_k, cache_v, i, bs, ls, tp, att)
        w_route = jnp.concatenate([r_w_grp[i], r_w_exp[i]], axis=1)
        w_route = jnp.pad(w_route, ((0, 0), (0, LANES - w_route.shape[1])))
        b_route = jnp.pad(jnp.concatenate([r_b_grp[i], r_b_exp[i]]), (0, LANES - MOE_GROUPS - MOE_EXPERTS))
        x2, logits = _mm_ln(att, m_w_o[i].astype(BF16), zero_bias, x1, row(ln_g[i, 1]), row(ln_b[i, 1]),
                            "attn_out", router=_hi_lo(w_route))

        route, counts = _route(logits, row(b_route), ltri)
        dest, blk_e, nused, tail_rows, n_blocks = _moe_plan(route, counts)
        xs = _dispatch(tail_rows, dest, x2, n_blocks * MOE_BLOCK_ROWS)
        yb = _experts(blk_e, nused, xs, e_w_gate_up, e_w_down, i)
        x = _combine(dest, x2, route, row(ln_g[i, 2]), row(ln_b[i, 2]), yb)

    y_prompt = x[:tp].reshape(bp, lp, d)
    y_sample = x[tp:].reshape(bs, ls, d)
    mem_k_p = mem_k.reshape(DEPTH, bp, mem_len, MEM_HEADS, hd)
    mem_v_p = mem_v.reshape(DEPTH, bp, mem_len, MEM_HEADS, hd)
    return (y_prompt, y_sample, mem_k_p, mem_v_p, jnp.stack(outs["gla_p"]), jnp.stack(outs["hgrn_p"]),
            jnp.stack(outs["conv_p"]), jnp.stack(outs["v"]), jnp.stack(outs["gla_s"]),
            jnp.stack(outs["hgrn_s"]), jnp.stack(outs["conv_s"]))
```

```python
import functools
import math

import jax
import jax.numpy as jnp
from jax import lax
from jax.experimental import pallas as pl
from jax.experimental.pallas import tpu as pltpu

F32 = jnp.float32
BF16 = jnp.bfloat16

D_MODEL = 1024
DEPTH = 4
N_MIXERS = 4
ALPHA = (2.0 * DEPTH) ** 0.25
LN_EPS = 1e-5
A_CHUNK = 128
A_GROUPS = 4
B_HEADS = 4
B_GATE_RANK = 16
B_GATE_TAU = 16.0
C_EXPAND = 128
D_CONV_W = 31
D_BUF = D_CONV_W - 1
LIN_CHUNK = 32
MEM_HEADS = 4
MOE_GROUPS = 4
MOE_PER_GROUP = 8
MOE_EXPERTS = MOE_GROUPS * MOE_PER_GROUP
MOE_HIDDEN = 512

LANES = 128
SUBLANES = 8
ROW_TILE = 512
MOE_BLOCK_ROWS = 256
ROUTE_LANE0 = MOE_GROUPS
VMEM_LIMIT = 56 * 1024 * 1024
INV_SQRT2 = 1.0 / math.sqrt(2.0)


def _params(*sem, vmem=VMEM_LIMIT):
    return pltpu.CompilerParams(dimension_semantics=sem, vmem_limit_bytes=vmem)


def _dot(a, b):
    return jnp.dot(a, b, preferred_element_type=F32)


def _dot_nt(a, b):
    return lax.dot_general(a, b, (((1,), (1,)), ((), ())), preferred_element_type=F32)


def _split3(a):
    hi = a.astype(BF16)
    r = a - hi.astype(F32)
    mid = r.astype(BF16)
    lo = (r - mid.astype(F32)).astype(BF16)
    return hi, mid, lo


def _dot_exact_lhs01(m01, a):
    hi, mid, lo = _split3(a)
    return _dot(m01, hi) + _dot(m01, mid) + _dot(m01, lo)


def _dot_hi(a, w_hi, w_lo):
    a_hi = a.astype(BF16)
    a_lo = (a - a_hi.astype(F32)).astype(BF16)
    return _dot(a_hi, w_hi) + _dot(a_lo, w_hi) + _dot(a_hi, w_lo)


def _layer_norm(x, g, b):
    mu = jnp.mean(x, axis=-1, keepdims=True)
    xc = x - mu
    var = jnp.mean(xc * xc, axis=-1, keepdims=True)
    return xc * lax.rsqrt(var + LN_EPS) * g + b


def _sigmoid(x):
    return 1.0 / (1.0 + jnp.exp(-x))


def _silu(x):
    return x * _sigmoid(x)


def _gelu(x):
    return 0.5 * x * (1.0 + lax.erf(x * INV_SQRT2))


def _log_sigmoid(x):
    return jnp.minimum(x, 0.0) - jnp.log(1.0 + jnp.exp(-jnp.abs(x)))


def _full(shape):
    return pl.BlockSpec(shape, lambda *_: (0,) * len(shape))


def _rows(tm, n, off=0):
    return pl.BlockSpec((tm, n), lambda i: (i + off, 0))


def _mm_kernel(a_ref, w_ref, o_ref):
    o_ref[...] = _dot(a_ref[...].astype(BF16), w_ref[...])


def _mem_proj_kernel(a_ref, w_ref, nat_ref, flat_ref, *, mem_len):
    res = _dot(a_ref[...].astype(BF16), w_ref[...])
    nat_ref[...] = res
    n_lt = res.shape[1] // (MEM_HEADS * LANES)
    for b in range(res.shape[0] // mem_len):
        for h in range(MEM_HEADS):
            for lt in range(n_lt):
                c0 = (h * n_lt + lt) * LANES
                flat_ref[b, pl.ds(lt * MEM_HEADS + h, mem_len, stride=MEM_HEADS * n_lt), :] = (
                    res[b * mem_len:(b + 1) * mem_len, c0:c0 + LANES])


def _mem_proj(a, w, mem_len):
    m, k = a.shape
    nl, _, n = w.shape
    tm = min(m, ROW_TILE)
    nbat = tm // mem_len
    return pl.pallas_call(
        functools.partial(_mem_proj_kernel, mem_len=mem_len),
        grid=(nl, m // tm),
        in_specs=[pl.BlockSpec((tm, k), lambda l, i: (i, 0)),
                  pl.BlockSpec((None, k, n), lambda l, i: (l, 0, 0))],
        out_specs=[pl.BlockSpec((None, tm, n), lambda l, i: (l, i, 0)),
                   pl.BlockSpec((None, nbat, mem_len * n // LANES, LANES), lambda l, i: (l, i, 0, 0))],
        out_shape=[jax.ShapeDtypeStruct((nl, m, n), F32),
                   jax.ShapeDtypeStruct((nl, m // mem_len, mem_len * n // LANES, LANES), F32)],
        compiler_params=_params("parallel", "parallel"),
        name="mem_kv_proj",
    )(a, w)


def _mm_rows(a, w, name):
    m, k = a.shape
    n = w.shape[1]
    tm = ROW_TILE
    return pl.pallas_call(
        _mm_kernel,
        grid=(m // tm,),
        in_specs=[_rows(tm, k), _full((k, n))],
        out_specs=_rows(tm, n),
        out_shape=jax.ShapeDtypeStruct((m, n), F32),
        compiler_params=_params("parallel"),
        name=name,
    )(a, w)


def _mm_ln_kernel(a_ref, w_ref, bias_ref, x_ref, g_ref, b_ref, *rest, with_logits):
    h = _dot(a_ref[...].astype(BF16), w_ref[...]) + bias_ref[...]
    y = _layer_norm(ALPHA * x_ref[...] + h, g_ref[...], b_ref[...])
    if with_logits:
        wr_hi_ref, wr_lo_ref, o_ref, lg_ref = rest
        lg_ref[...] = _dot_hi(y, wr_hi_ref[...], wr_lo_ref[...])
    else:
        (o_ref,) = rest
    o_ref[...] = y


def _mm_ln(a, w, bias, x, g, b, name, router=None):
    m, k = a.shape
    d = x.shape[1]
    tm = ROW_TILE
    in_specs = [_rows(tm, k), _full((k, d)), _full((1, d)), _rows(tm, d), _full((1, d)), _full((1, d))]
    args = [a, w, bias, x, g, b]
    out_specs = _rows(tm, d)
    out_shape = jax.ShapeDtypeStruct((m, d), F32)
    if router is not None:
        in_specs += [_full((d, LANES)), _full((d, LANES))]
        args += list(router)
        out_specs = [out_specs, _rows(tm, LANES)]
        out_shape = [out_shape, jax.ShapeDtypeStruct((m, LANES), F32)]
    return pl.pallas_call(
        functools.partial(_mm_ln_kernel, with_logits=router is not None),
        grid=(m // tm,),
        in_specs=in_specs,
        out_specs=out_specs,
        out_shape=out_shape,
        compiler_params=_params("parallel"),
        name=name,
    )(*args)


def _gmlp_kernel(x_ref, w_in_ref, b_in_ref, lng_ref, lnb_ref, wc_ref, bc_ref, w_out_ref, b_out_ref,
                 g_ref, b_ref, *rest, emit_v, n_chunks):
    if emit_v:
        _alias_ref, o_ref, v_ref, vn_ref = rest
    else:
        o_ref, vn_ref = rest
    half = w_out_ref.shape[0]
    gw = half // A_GROUPS
    x = x_ref[...]
    xb = x.astype(BF16)
    v = _gelu(_dot(xb, w_in_ref[:, half:]) + b_in_ref[:, half:])
    vn = _layer_norm(v, lng_ref[...], lnb_ref[...])
    vn_ref[...] = vn
    if emit_v:
        v_ref[...] = vn
    acc = jnp.zeros(x.shape, F32)
    for grp in range(A_GROUPS):
        cols = slice(grp * gw, (grp + 1) * gw)
        u = _gelu(_dot(xb, w_in_ref[:, cols]) + b_in_ref[:, cols])
        mixed = []
        for c in range(n_chunks):
            vc = vn_ref[c * A_CHUNK:(c + 1) * A_CHUNK, cols].astype(BF16)
            mixed.append(_dot(wc_ref[grp], vc) + bc_ref[grp])
        mixed = mixed[0] if n_chunks == 1 else jnp.concatenate(mixed, axis=0)
        acc = acc + _dot((u * mixed).astype(BF16), w_out_ref[cols, :])
    h = acc + b_out_ref[...]
    o_ref[...] = _layer_norm(ALPHA * x + h, g_ref[...], b_ref[...])


def _gmlp(x, row_off, t, w_in, b_in, lng, lnb, wc, bc, w_out, b_out, g, b, alias=None):
    n_rows, d = x.shape
    ffn = w_in.shape[1]
    half = ffn // 2
    tm = 2 * A_CHUNK
    emit_v = alias is not None
    off = row_off // tm
    in_specs = [_rows(tm, d), _full((d, ffn)), _full((1, ffn)), _full((1, half)), _full((1, half)),
                _full((A_GROUPS, A_CHUNK, A_CHUNK)), _full((A_GROUPS, A_CHUNK, 1)), _full((half, d)),
                _full((1, d)), _full((1, d)), _full((1, d))]
    args = [x, w_in, b_in, lng, lnb, wc, bc, w_out, b_out, g, b]
    out_specs = _rows(tm, d, off)
    out_shape = jax.ShapeDtypeStruct((t, d), F32)
    aliases = {}
    if emit_v:
        in_specs.append(pl.BlockSpec(memory_space=pl.ANY))
        args.append(alias)
        aliases = {len(args) - 1: 0}
        out_specs = [out_specs, _rows(tm, half)]
        out_shape = [out_shape, jax.ShapeDtypeStruct((n_rows, half), F32)]
    return pl.pallas_call(
        functools.partial(_gmlp_kernel, emit_v=emit_v, n_chunks=tm // A_CHUNK),
        grid=(n_rows // tm,),
        in_specs=in_specs,
        out_specs=out_specs,
        out_shape=out_shape,
        scratch_shapes=[pltpu.VMEM((tm, half), F32)],
        input_output_aliases=aliases,
        compiler_params=_params("parallel"),
        name="gmlp_sample" if emit_v else "gmlp_prompt",
    )(*args)


def _gla_proj_kernel(x_ref, w_ref, wlow_ref, g2_hi_ref, g2_lo_ref, bg_ref,
                     q_ref, k_ref, v_ref, r_ref, la_ref, *, dk, dv, q_scale):
    xb = x_ref[...].astype(BF16)
    q_ref[...] = _dot(xb, w_ref[:, 0:dk]) * q_scale
    k_ref[...] = _dot(xb, w_ref[:, dk:2 * dk])
    v_ref[...] = _dot(xb, w_ref[:, 2 * dk:2 * dk + dv])
    r_ref[...] = _dot(xb, w_ref[:, 2 * dk + dv:2 * dk + 2 * dv])
    g_low = _dot(xb, wlow_ref[...])
    pre = _dot_hi(g_low, g2_hi_ref[...], g2_lo_ref[...]) + bg_ref[...]
    la_ref[...] = _log_sigmoid(pre) * (1.0 / B_GATE_TAU)


def _gla_proj(x, w_main, w_low, g2_hi, g2_lo, bg, dk, dv):
    t, d = x.shape
    tm = ROW_TILE
    shapes = [dk, dk, dv, dv, dk]
    return pl.pallas_call(
        functools.partial(_gla_proj_kernel, dk=dk, dv=dv, q_scale=(dk // B_HEADS) ** -0.5),
        grid=(t // tm,),
        in_specs=[_rows(tm, d), _full(w_main.shape), _full(w_low.shape), _full(g2_hi.shape),
                  _full(g2_lo.shape), _full((1, dk))],
        out_specs=[_rows(tm, n) for n in shapes],
        out_shape=[jax.ShapeDtypeStruct((t, n), F32) for n in shapes],
        compiler_params=_params("parallel"),
        name="gla_proj",
    )(x, w_main, w_low, g2_hi, g2_lo, bg)


def _hgrn_proj_kernel(x_ref, w_ref, lb_ref, q_ref, lf_ref, k_ref, v_ref, gt_ref, *, d, q_scale):
    xb = x_ref[...].astype(BF16)
    lb = lb_ref[...]
    q_ref[...] = _silu(_dot(xb, w_ref[:, 0:d])) * q_scale
    f = _dot(xb, w_ref[:, d:2 * d])
    lf_ref[...] = jnp.log(lb + (1.0 - lb) * _sigmoid(f))
    k_ref[...] = (1.0 - lb) * _sigmoid(-f)
    v_ref[...] = _dot(xb, w_ref[:, 2 * d:3 * d])
    gt_ref[...] = _dot(xb, w_ref[:, 3 * d:4 * d])


def _hgrn_proj(x, w, lb):
    t, d = x.shape
    tm = ROW_TILE
    return pl.pallas_call(
        functools.partial(_hgrn_proj_kernel, d=d, q_scale=C_EXPAND ** -0.5),
        grid=(t // tm,),
        in_specs=[_rows(tm, d), _full(w.shape), _full((1, d))],
        out_specs=[_rows(tm, d)] * 5,
        out_shape=[jax.ShapeDtypeStruct((t, d), F32)] * 5,
        compiler_params=_params("parallel"),
        name="hgrn_proj",
    )(x, w, lb)


def _cumsum_rows(x, chunk):
    pos = lax.broadcasted_iota(jnp.int32, x.shape, 0) & (chunk - 1)
    step = 1
    while step < chunk:
        x = x + jnp.where(pos >= step, pltpu.roll(x, step, axis=0), 0.0)
        step *= 2
    return x


def _chunk_rows(x, row, chunk, n_chunks):
    parts = [jnp.broadcast_to(x[c * chunk + row:c * chunk + row + 1, :], (chunk, x.shape[1]))
             for c in range(n_chunks)]
    return parts[0] if n_chunks == 1 else jnp.concatenate(parts, axis=0)


def _linrec_kernel(q_ref, k_ref, g_ref, v_ref, gate_ref, ng_ref, s0_ref, *rest,
                   heads, dk, dv, chunk, n_chunks, nb, aliased):
    if aliased:
        _alias_ref, o_ref, sout_ref, st_ref = rest
    else:
        o_ref, sout_ref, st_ref = rest
    li = pl.program_id(1)
    tl = chunk * n_chunks
    r_id = lax.broadcasted_iota(jnp.int32, (tl, tl), 0)
    c_id = lax.broadcasted_iota(jnp.int32, (tl, tl), 1)
    mask = (r_id >= c_id) & ((r_id // chunk) == (c_id // chunk))

    for n in range(nb):
        @pl.when(li == 0)
        def _():
            st_ref[...] = s0_ref[n]

        rs = slice(n * tl, (n + 1) * tl)
        for h in range(heads):
            ks = slice(h * dk, (h + 1) * dk)
            vs = slice(h * dv, (h + 1) * dv)
            bcum = _cumsum_rows(g_ref[rs, ks], chunk)
            b_mid = _chunk_rows(bcum, chunk // 2 - 1, chunk, n_chunks)
            b_end = _chunk_rows(bcum, chunk - 1, chunk, n_chunks)
            q = q_ref[rs, ks]
            k = k_ref[rs, ks]
            vb = v_ref[rs, vs].astype(BF16)
            q_in = (q * jnp.exp(bcum)).astype(BF16)
            q_a = (q * jnp.exp(bcum - b_mid)).astype(BF16)
            k_a = (k * jnp.exp(b_mid - bcum)).astype(BF16)
            k_end = k * jnp.exp(b_end - bcum)
            a = jnp.where(mask, _dot_nt(q_a, k_a), 0.0).astype(BF16)
            o_intra = _dot(a, vb)
            st = st_ref[h]
            outs = []
            for c in range(n_chunks):
                cr = slice(c * chunk, (c + 1) * chunk)
                outs.append(o_intra[cr] + _dot(q_in[cr], st.astype(BF16)))
                decay = jnp.exp(bcum[cr].T[:, chunk - 1:chunk])
                st = st * decay + _dot(k_end[cr].T.astype(BF16), vb[cr])
            st_ref[h] = st
            o = outs[0] if n_chunks == 1 else jnp.concatenate(outs, axis=0)
            o = o * lax.rsqrt(jnp.mean(o * o, axis=-1, keepdims=True) + LN_EPS) * ng_ref[:, vs]
            o_ref[rs, vs] = o * _silu(gate_ref[rs, vs])
        sout_ref[n] = st_ref[...]


LINREC_CHUNK = 2 * LIN_CHUNK
LINREC_SAMPLE_NB = 4


def _linrec(q, k, g, v, gate, ng, s0, *, n_batch, seq, row_off, heads, alias=None):
    t, hk = q.shape
    hv = v.shape[1]
    dk, dv = hk // heads, hv // heads
    chunk = LINREC_CHUNK if seq % LINREC_CHUNK == 0 else seq
    tl = min(seq, 4 * chunk)
    n_l = seq // tl
    nb = LINREC_SAMPLE_NB if n_l == 1 else 1
    off = row_off // (nb * tl)

    def rows(n):
        return pl.BlockSpec((nb * tl, n), lambda b, l: (off + b * n_l + l, 0))

    state_spec = pl.BlockSpec((nb, heads, dk, dv), lambda b, l: (b, 0, 0, 0))
    in_specs = [rows(hk), rows(hk), rows(hk), rows(hv), rows(hv), pl.BlockSpec((1, hv), lambda b, l: (0, 0)),
                state_spec]
    args = [q, k, g, v, gate, ng, s0]
    aliases = {}
    if alias is not None:
        in_specs.append(pl.BlockSpec(memory_space=pl.ANY))
        args.append(alias)
        aliases = {len(args) - 1: 0}
    return pl.pallas_call(
        functools.partial(_linrec_kernel, heads=heads, dk=dk, dv=dv, chunk=chunk, n_chunks=tl // chunk, nb=nb,
                          aliased=alias is not None),
        grid=(n_batch // nb, n_l),
        in_specs=in_specs,
        out_specs=[rows(hv), state_spec],
        out_shape=[jax.ShapeDtypeStruct((t, hv), F32), jax.ShapeDtypeStruct((n_batch, heads, dk, dv), F32)],
        scratch_shapes=[pltpu.VMEM((heads, dk, dv), F32)],
        input_output_aliases=aliases,
        compiler_params=_params("parallel", "arbitrary"),
        name="linrec_h%d_%s" % (heads, "sample" if alias is not None else "prompt"),
    )(*args)


def _glu_kernel(x_ref, w_ref, b_ref, o_ref, *, d):
    xb = x_ref[...].astype(BF16)
    a = _dot(xb, w_ref[:, 0:d]) + b_ref[:, 0:d]
    gate = _dot(xb, w_ref[:, d:2 * d]) + b_ref[:, d:2 * d]
    o_ref[...] = a * _sigmoid(gate)


def _glu(x, w, b):
    t, d = x.shape
    tm = ROW_TILE
    return pl.pallas_call(
        functools.partial(_glu_kernel, d=d),
        grid=(t // tm,),
        in_specs=[_rows(tm, d), _full(w.shape), _full(b.shape)],
        out_specs=_rows(tm, d),
        out_shape=jax.ShapeDtypeStruct((t, d), F32),
        compiler_params=_params("parallel"),
        name="conf_glu",
    )(x, w, b)


CONV_PAD = 32


def _conv_kernel(h_ref, st_ref, wdw_ref, bdw_ref, lg_ref, lb_ref, *rest, nb, tl, aliased):
    if aliased:
        _alias_ref, o_ref, sout_ref, buf_ref, conv_ref = rest
    else:
        o_ref, sout_ref, buf_ref, conv_ref = rest
    li = pl.program_id(1)
    d = h_ref.shape[-1]
    lead = CONV_PAD - D_BUF
    rb = min(tl, 64)
    cw = LANES
    buf_ref[CONV_PAD + tl:CONV_PAD + tl + SUBLANES, :] = jnp.zeros((SUBLANES, d), F32)
    for n in range(nb):
        @pl.when(li == 0)
        def _():
            buf_ref[lead:CONV_PAD, :] = st_ref[n]

        buf_ref[CONV_PAD:CONV_PAD + tl, :] = h_ref[n * tl:(n + 1) * tl, :]
        for r0 in range(0, tl, rb):
            for c0 in range(0, d, cw):
                cols = slice(c0, c0 + cw)
                acc = jnp.zeros((rb, cw), F32)
                for s in range(SUBLANES):
                    part = None
                    for a in range((CONV_PAD + SUBLANES) // SUBLANES):
                        j = SUBLANES * a + s - lead
                        if 0 <= j < D_CONV_W:
                            rows = slice(r0 + SUBLANES * a, r0 + SUBLANES * a + rb + SUBLANES)
                            term = wdw_ref[j:j + 1, cols] * buf_ref[rows, cols]
                            part = term if part is None else part + term
                    acc = acc + part[s:s + rb, :]
                conv_ref[r0:r0 + rb, cols] = acc + bdw_ref[:, cols]
        y = _layer_norm(conv_ref[...], lg_ref[...], lb_ref[...])
        o_ref[n * tl:(n + 1) * tl, :] = _silu(y)
        sout_ref[n] = buf_ref[tl + lead:tl + CONV_PAD, :]
        buf_ref[0:CONV_PAD, :] = buf_ref[tl:tl + CONV_PAD, :]


def _conv(h, state, wdw, bdw, lg, lb, *, n_batch, seq, row_off, alias=None):
    t, d = h.shape
    tl = min(seq, 256)
    n_l = seq // tl
    nb = 8 if n_l == 1 else 1
    off = row_off // (nb * tl)
    rows = pl.BlockSpec((nb * tl, d), lambda b, l: (off + b * n_l + l, 0))
    state_spec = pl.BlockSpec((nb, D_BUF, d), lambda b, l: (b, 0, 0))
    in_specs = [rows, state_spec, _full(wdw.shape), _full((1, d)), _full((1, d)), _full((1, d))]
    args = [h, state, wdw, bdw, lg, lb]
    aliases = {}
    if alias is not None:
        in_specs.append(pl.BlockSpec(memory_space=pl.ANY))
        args.append(alias)
        aliases = {len(args) - 1: 0}
    return pl.pallas_call(
        functools.partial(_conv_kernel, nb=nb, tl=tl, aliased=alias is not None),
        grid=(n_batch // nb, n_l),
        in_specs=in_specs,
        out_specs=[rows, state_spec],
        out_shape=[jax.ShapeDtypeStruct((t, d), F32), jax.ShapeDtypeStruct((n_batch, D_BUF, d), F32)],
        scratch_shapes=[pltpu.VMEM((CONV_PAD + tl + SUBLANES, d), F32), pltpu.VMEM((tl, d), F32)],
        input_output_aliases=aliases,
        compiler_params=_params("parallel", "arbitrary"),
        name="conf_conv_sample" if alias is not None else "conf_conv_prompt",
    )(*args)


def _attn_heads(q, k_of, v_of, hd):
    outs = []
    for h in range(MEM_HEADS):
        hs = slice(h * hd, (h + 1) * hd)
        s = _dot_nt(q[:, hs].astype(BF16), k_of(hs).astype(BF16)) * (hd ** -0.5)
        p = jnp.exp(s - jnp.max(s, axis=-1, keepdims=True))
        denom = jnp.sum(p, axis=-1, keepdims=True)
        outs.append(_dot(p.astype(BF16), v_of(hs).astype(BF16)) / denom)
    return outs


def _attn_prompt_kernel(q_ref, k_ref, v_ref, o_ref):
    hd = q_ref.shape[-1] // MEM_HEADS
    outs = _attn_heads(q_ref[...], lambda hs: k_ref[:, hs], lambda hs: v_ref[:, hs], hd)
    for h, o in enumerate(outs):
        o_ref[:, h * hd:(h + 1) * hd] = o


def _attn_prompt(q, mem_k, mem_v, layer, n_batch, seq):
    t, d = q.shape
    m = mem_k.shape[1] // n_batch
    tl = ROW_TILE
    n_l = seq // tl
    rows = pl.BlockSpec((tl, d), lambda b, l: (b * n_l + l, 0))
    kv = pl.BlockSpec((None, m, d), lambda b, l: (layer, b, 0))
    return pl.pallas_call(
        _attn_prompt_kernel,
        grid=(n_batch, n_l),
        in_specs=[rows, kv, kv],
        out_specs=rows,
        out_shape=jax.ShapeDtypeStruct((t, d), F32),
        compiler_params=_params("parallel", "parallel"),
        name="attn_prompt",
    )(q, mem_k, mem_v)


ATTN_SAMPLE_NB = 4


def _attn_sample_kernel(q_ref, k_ref, v_ref, _alias_ref, o_ref, *, nb, seq):
    hd = q_ref.shape[-1] // MEM_HEADS
    n_lt = hd // LANES
    m = k_ref.shape[1] // (MEM_HEADS * n_lt)

    def head(ref, n, h):
        tiles = [ref[n, pl.ds(lt * MEM_HEADS + h, m, stride=MEM_HEADS * n_lt), :] for lt in range(n_lt)]
        return jnp.concatenate(tiles, axis=1).astype(BF16)

    pairs = [(n, h) for n in range(nb) for h in range(MEM_HEADS)]
    s = jnp.concatenate(
        [_dot_nt(q_ref[n * seq:(n + 1) * seq, h * hd:(h + 1) * hd].astype(BF16), head(k_ref, n, h))
         for n, h in pairs], axis=0) * (hd ** -0.5)
    p = jnp.exp(s - jnp.max(s, axis=-1, keepdims=True))
    inv = 1.0 / jnp.sum(p, axis=-1, keepdims=True)
    for idx, (n, h) in enumerate(pairs):
        rs = slice(idx * seq, (idx + 1) * seq)
        o = _dot(p[rs].astype(BF16), head(v_ref, n, h)) * inv[rs]
        o_ref[n * seq:(n + 1) * seq, h * hd:(h + 1) * hd] = o


def _cache_rows(cache):
    nl, nbat, m, heads, hd = cache.shape
    c = cache.reshape(nl, nbat, m, heads, hd // LANES, LANES)
    return c.transpose(0, 1, 2, 4, 3, 5).reshape(nl, nbat, m * (hd // LANES) * heads, LANES)


def _cache_unrows(flat, heads):
    nl, nbat, rows, _ = flat.shape
    n_lt = D_MODEL // (heads * LANES)
    m = rows // (heads * n_lt)
    c = flat.reshape(nl, nbat, m, n_lt, heads, LANES).transpose(0, 1, 2, 4, 3, 5)
    return c.reshape(nl, nbat, m, heads, n_lt * LANES)


def _attn_sample(q, cache_k, cache_v, layer, n_batch, seq, row_off, alias):
    t, d = q.shape
    nb = ATTN_SAMPLE_NB
    off = row_off // (nb * seq)
    rows = pl.BlockSpec((nb * seq, d), lambda i: (off + i, 0))
    kv = pl.BlockSpec((None, nb) + cache_k.shape[2:], lambda i: (layer, i, 0, 0))
    return pl.pallas_call(
        functools.partial(_attn_sample_kernel, nb=nb, seq=seq),
        grid=(n_batch // nb,),
        in_specs=[rows, kv, kv, pl.BlockSpec(memory_space=pl.ANY)],
        out_specs=rows,
        out_shape=jax.ShapeDtypeStruct((t, d), F32),
        input_output_aliases={3: 0},
        compiler_params=_params("parallel"),
        name="attn_sample",
    )(q, cache_k, cache_v, alias)


def _route_kernel(lg_ref, bias_ref, ltri_ref, route_ref, counts_ref, carry_ref):
    i = pl.program_id(0)

    @pl.when(i == 0)
    def _():
        carry_ref[...] = jnp.zeros(carry_ref.shape, F32)

    z = lg_ref[...] + bias_ref[...]
    lane = lax.broadcasted_iota(jnp.int32, z.shape, 1).astype(F32)
    neg = -jnp.inf
    far = float(LANES)

    def first_max(mask):
        vmax = jnp.max(jnp.where(mask, z, neg), axis=-1, keepdims=True)
        idx = jnp.min(jnp.where(mask & (z == vmax), lane, far), axis=-1, keepdims=True)
        return vmax, idx

    gmask = lane < float(MOE_GROUPS)
    gmax, gidx = first_max(gmask)
    gsum = jnp.sum(jnp.where(gmask, jnp.exp(z - gmax), 0.0), axis=-1, keepdims=True)
    g_w = 1.0 / gsum
    lo = float(ROUTE_LANE0) + float(MOE_PER_GROUP) * gidx
    emask = (lane >= lo) & (lane < lo + float(MOE_PER_GROUP))
    v1, i1 = first_max(emask)
    v2, i2 = first_max(emask & (lane != i1))
    tt = jnp.exp(v2 - v1)
    w0 = g_w / (1.0 + tt)
    w1 = g_w * tt / (1.0 + tt)
    sel1 = lane == i1
    sel2 = lane == i2
    onehot = jnp.where(sel1 | sel2, 1.0, 0.0)
    before = _dot(ltri_ref[...], onehot.astype(BF16)) + carry_ref[...]
    rank0 = jnp.sum(jnp.where(sel1, before, 0.0), axis=-1, keepdims=True)
    rank1 = jnp.sum(jnp.where(sel2, before, 0.0), axis=-1, keepdims=True)
    carry = carry_ref[...] + jnp.sum(onehot, axis=0, keepdims=True)
    carry_ref[...] = carry
    counts_ref[...] = carry
    e_off = float(ROUTE_LANE0)
    out = jnp.zeros(z.shape, F32)
    for ln, val in enumerate((i1 - e_off, i2 - e_off, w0, w1, rank0, rank1)):
        out = jnp.where(lane == float(ln), val, out)
    route_ref[...] = out


def _route(logits, bias, ltri):
    t = logits.shape[0]
    tm = ROW_TILE
    return pl.pallas_call(
        _route_kernel,
        grid=(t // tm,),
        in_specs=[_rows(tm, LANES), _full((1, LANES)), _full((tm, tm))],
        out_specs=[_rows(tm, LANES), _full((1, LANES))],
        out_shape=[jax.ShapeDtypeStruct((t, LANES), F32), jax.ShapeDtypeStruct((1, LANES), F32)],
        scratch_shapes=[pltpu.VMEM((1, LANES), F32)],
        compiler_params=_params("arbitrary"),
        name="moe_route",
    )(logits, bias, ltri)


def _row_copy_wait(src_rows, dst_rows, sem):
    pltpu.make_async_copy(src_rows, dst_rows, sem).wait()


def _dispatch_kernel(tail_ref, dest_ref, x_ref, xs_ref, zeros_ref, sem, *, tm):
    @pl.when(pl.program_id(0) == 0)
    def _():
        zeros_ref[...] = jnp.zeros(zeros_ref.shape, F32)
        bm = zeros_ref.shape[0]
        tails = [xs_ref.at[pl.ds(pl.multiple_of(tail_ref[e], bm), bm)] for e in range(MOE_EXPERTS)]
        for dst in tails:
            pltpu.make_async_copy(zeros_ref, dst, sem).start()
        for dst in tails:
            pltpu.make_async_copy(zeros_ref, dst, sem).wait()

    def issue(t, carry):
        for j in range(2):
            d = dest_ref[2 * t + j]
            pltpu.make_async_copy(x_ref.at[pl.ds(t, 1)], xs_ref.at[pl.ds(d, 1)], sem).start(priority=j)
        return carry

    lax.fori_loop(0, tm, issue, 0, unroll=8)
    for _ in range(2):
        _row_copy_wait(x_ref, xs_ref.at[pl.ds(0, tm)], sem)


def _dispatch(tail_rows, dest_flat, x, n_rows):
    t, d = x.shape
    tm = ROW_TILE
    grid_spec = pltpu.PrefetchScalarGridSpec(
        num_scalar_prefetch=1,
        grid=(t // tm,),
        in_specs=[pl.BlockSpec((2 * tm,), lambda i, tail: (i,), memory_space=pltpu.SMEM),
                  pl.BlockSpec((tm, d), lambda i, tail: (i, 0))],
        out_specs=pl.BlockSpec(memory_space=pl.ANY),
        scratch_shapes=[pltpu.VMEM((MOE_BLOCK_ROWS, d), F32), pltpu.SemaphoreType.DMA(())],
    )
    return pl.pallas_call(
        functools.partial(_dispatch_kernel, tm=tm),
        grid_spec=grid_spec,
        out_shape=jax.ShapeDtypeStruct((n_rows, d), F32),
        compiler_params=_params("arbitrary"),
        name="moe_dispatch",
    )(tail_rows, dest_flat, x)


def _expert_kernel(blk_e_ref, nused_ref, xs_ref, wgu_ref, wd_ref, y_ref, wgu_bf, wd_bf):
    i = pl.program_id(0)
    prev = blk_e_ref[jnp.maximum(i - 1, 0)]
    new_expert = (i == 0) | (blk_e_ref[i] != prev)

    @pl.when(new_expert)
    def _():
        wgu_bf[...] = wgu_ref[...].astype(BF16)
        wd_bf[...] = wd_ref[...].astype(BF16)

    @pl.when(i < nused_ref[0])
    def _():
        hid = wd_bf.shape[0]
        xb = xs_ref[...].astype(BF16)
        gate = _dot(xb, wgu_bf[:, 0:hid])
        up = _dot(xb, wgu_bf[:, hid:2 * hid])
        y_ref[...] = _dot((_silu(gate) * up).astype(BF16), wd_bf[...])

    @pl.when(i >= nused_ref[0])
    def _():
        y_ref[...] = jnp.zeros(y_ref.shape, F32)


def _experts(blk_e, nused, xs, w_gate_up, w_down, layer):
    nr, d = xs.shape
    bm = MOE_BLOCK_ROWS
    hid2 = w_gate_up.shape[-1]
    hid = w_down.shape[-2]
    grid_spec = pltpu.PrefetchScalarGridSpec(
        num_scalar_prefetch=2,
        grid=(nr // bm,),
        in_specs=[pl.BlockSpec((bm, d), lambda i, be, nu: (jnp.minimum(i, nu[0] - 1), 0)),
                  pl.BlockSpec((None, None, d, hid2), lambda i, be, nu: (layer, be[i], 0, 0)),
                  pl.BlockSpec((None, None, hid, d), lambda i, be, nu: (layer, be[i], 0, 0))],
        out_specs=pl.BlockSpec((bm, d), lambda i, be, nu: (i, 0)),
        scratch_shapes=[pltpu.VMEM((d, hid2), BF16), pltpu.VMEM((hid, d), BF16)],
    )
    return pl.pallas_call(
        _expert_kernel,
        grid_spec=grid_spec,
        out_shape=jax.ShapeDtypeStruct((nr, d), F32),
        compiler_params=_params("arbitrary"),
        name="moe_experts",
    )(blk_e, nused, xs, w_gate_up, w_down)


def _combine_kernel(dest_ref, dest_next_ref, x_ref, route_ref, g_ref, b_ref, y_hbm, *rest, tm, n_first):
    if n_first is None:
        o_ref, ybuf, sems = rest
    else:
        o_ref, o2_ref, ybuf, sems = rest
    i = pl.program_id(0)
    n_tiles = pl.num_programs(0)

    def gather(d_ref, slot):
        def issue(t, carry):
            for j in range(2):
                d = d_ref[2 * t + j]
                pltpu.make_async_copy(y_hbm.at[pl.ds(d, 1)], ybuf.at[slot, j, pl.ds(t, 1)],
                                      sems.at[slot]).start(priority=j)
            return carry

        lax.fori_loop(0, tm, issue, 0, unroll=8)

    @pl.when(i == 0)
    def _():
        gather(dest_ref, 0)

    @pl.when(i + 1 < n_tiles)
    def _():
        gather(dest_next_ref, lax.rem(i + 1, 2))

    slot = lax.rem(i, 2)
    for j in range(2):
        _row_copy_wait(y_hbm.at[pl.ds(0, tm)], ybuf.at[slot, j], sems.at[slot])
    moe = route_ref[:, 2:3] * ybuf[slot, 0] + route_ref[:, 3:4] * ybuf[slot, 1]
    y = _layer_norm(ALPHA * x_ref[...] + moe, g_ref[...], b_ref[...])
    if n_first is None:
        o_ref[...] = y
    else:
        @pl.when(i < n_first)
        def _():
            o_ref[...] = y

        @pl.when(i >= n_first)
        def _():
            o2_ref[...] = y


def _combine(dest_flat, x, route, g, b, yb, split_rows=None):
    t, d = x.shape
    tm = ROW_TILE
    n_tiles = t // tm
    out_specs = _rows(tm, d)
    out_shape = jax.ShapeDtypeStruct((t, d), F32)
    n_first = None
    if split_rows is not None:
        n_first = split_rows // tm
        out_specs = [pl.BlockSpec((tm, d), lambda i: (jnp.minimum(i, n_first - 1), 0)),
                     pl.BlockSpec((tm, d), lambda i: (jnp.maximum(i - n_first, 0), 0))]
        out_shape = [jax.ShapeDtypeStruct((split_rows, d), F32), jax.ShapeDtypeStruct((t - split_rows, d), F32)]
    return pl.pallas_call(
        functools.partial(_combine_kernel, tm=tm, n_first=n_first),
        grid=(n_tiles,),
        in_specs=[pl.BlockSpec((2 * tm,), lambda i: (i,), memory_space=pltpu.SMEM),
                  pl.BlockSpec((2 * tm,), lambda i: (jnp.minimum(i + 1, n_tiles - 1),), memory_space=pltpu.SMEM),
                  _rows(tm, d), _rows(tm, LANES), _full((1, d)), _full((1, d)),
                  pl.BlockSpec(memory_space=pl.ANY)],
        out_specs=out_specs,
        out_shape=out_shape,
        scratch_shapes=[pltpu.VMEM((2, 2, tm, d), F32), pltpu.SemaphoreType.DMA((2,))],
        compiler_params=_params("arbitrary"),
        name="moe_combine",
    )(dest_flat, dest_flat, x, route, g, b, yb)


def _moe_plan(route, counts):
    bm = MOE_BLOCK_ROWS
    t = route.shape[0]
    n_blocks = -(-2 * t // bm) + MOE_EXPERTS
    e_idx = route[:, 0:2].astype(jnp.int32)
    rank = route[:, 4:6].astype(jnp.int32)
    cnt = counts[0, ROUTE_LANE0:ROUTE_LANE0 + MOE_EXPERTS].astype(jnp.int32)
    padded = ((cnt + bm - 1) // bm) * bm
    pad_end = jnp.cumsum(padded)
    pad_start = pad_end - padded
    experts = jnp.arange(MOE_EXPERTS, dtype=jnp.int32)
    dest = (rank + jnp.sum(jnp.where(e_idx[..., None] == experts, pad_start, 0), axis=-1)).reshape(-1)
    nused = pad_end[-1] // bm
    blk = jnp.arange(n_blocks, dtype=jnp.int32)
    blk_e = jnp.sum((pad_end[None, :] <= (blk * bm)[:, None]).astype(jnp.int32), axis=1)
    blk_e = jnp.minimum(blk_e, MOE_EXPERTS - 1)
    last_e = jnp.max(jnp.where(cnt > 0, experts, 0))
    blk_e = jnp.where(blk < nused, blk_e, last_e)
    empty = cnt == 0
    tail_rows = jnp.where(empty, (nused + jnp.cumsum(empty.astype(jnp.int32)) - 1) * bm, pad_end - bm)
    return (dest.astype(jnp.int32), blk_e.astype(jnp.int32), nused.reshape(1).astype(jnp.int32),
            tail_rows.astype(jnp.int32), n_blocks)


def _hi_lo(w):
    hi = w.astype(BF16)
    return hi, (w - hi.astype(F32)).astype(BF16)


def kernel(x_prompt, x_sample, mem_prompt, cache_mem_k, cache_mem_v, state_gla, state_hgrn, state_conv,
           ln_g, ln_b, a_w_in, a_b_in, a_ln_g, a_ln_b, a_w_s, a_b_s, a_w_out, a_b_out,
           b_w_in, b_w_g2, b_b_g, b_norm_g, b_w_out, c_lb, c_w_in, c_norm_g, c_w_out,
           d_w_in, d_b_in, d_w_dw, d_b_dw, d_ln_g, d_ln_b, d_w_out, d_b_out,
           m_w_q, m_w_k, m_w_v, m_w_o, r_w_grp, r_b_grp, r_w_exp, r_b_exp, e_w_gate_up, e_w_down):
    bp, lp, d = x_prompt.shape
    bs, ls, _ = x_sample.shape
    tp, ts = bp * lp, bs * ls
    t = tp + ts
    mem_len = mem_prompt.shape[1]
    hd = d // MEM_HEADS

    row = lambda a: a.reshape(1, -1)
    x = None

    mem2d = mem_prompt.reshape(bp * mem_len, d)
    mem_k, mem_k_flat = _mem_proj(mem2d, m_w_k.astype(BF16), mem_len)
    mem_v, mem_v_flat = _mem_proj(mem2d, m_w_v.astype(BF16), mem_len)

    cache_k = _cache_rows(cache_mem_k)
    cache_v = _cache_rows(cache_mem_v)

    lb_all = jnp.cumsum(jax.nn.softmax(c_lb.astype(F32), axis=0), axis=0)
    lb_all = lb_all - lb_all[:1]
    ltri = jnp.tril(jnp.ones((ROW_TILE, ROW_TILE), F32), -1).astype(BF16)
    zero_bias = jnp.zeros((1, d), F32)

    outs = {"v": [], "gla_p": [], "gla_s": [], "hgrn_p": [], "hgrn_s": [], "conv_p": [], "conv_s": []}
    for i in range(DEPTH):
        j = i // N_MIXERS
        kind = i % N_MIXERS
        g1, b1 = row(ln_g[i, 0]), row(ln_b[i, 0])
        if kind == 0:
            tril = jnp.tril(jnp.ones((A_CHUNK, A_CHUNK), bool))
            wc_p = jnp.where(tril, a_w_s[j], 0.0).astype(BF16)
            bc_p = a_b_s[j][:, :, None]
            reps = A_CHUNK // ls
            small = jnp.where(jnp.tril(jnp.ones((ls, ls), bool)), a_w_s[j][:, :ls, :ls], 0.0)
            wc_s = jax.vmap(lambda m: jnp.kron(jnp.eye(reps, dtype=F32), m))(small).astype(BF16)
            bc_s = jnp.tile(a_b_s[j][:, :ls], (1, reps))[:, :, None]
            common = (a_w_in[j].astype(BF16), row(a_b_in[j]), row(a_ln_g[j]), row(a_ln_b[j]))
            tail = (a_w_out[j].astype(BF16), row(a_b_out[j]), g1, b1)
            xp_in = x_prompt.reshape(tp, d) if x is None else x[:tp]
            xs_in = x_sample.reshape(ts, d) if x is None else x[tp:]
            x1 = _gmlp(xp_in, 0, t, *common, wc_p, bc_p, *tail)
            x1, v_s = _gmlp(xs_in, tp, t, *common, wc_s, bc_s, *tail, alias=x1)
            outs["v"].append(v_s.reshape(bs, ls, -1))
        elif kind == 1:
            dk, dv = b_w_g2.shape[-1], b_w_out.shape[1]
            w_in = b_w_in[j]
            w_main = w_in[:, :2 * dk + 2 * dv].astype(BF16)
            w_low = jnp.pad(w_in[:, 2 * dk + 2 * dv:], ((0, 0), (0, LANES - B_GATE_RANK))).astype(BF16)
            g2_hi, g2_lo = _hi_lo(jnp.pad(b_w_g2[j], ((0, LANES - B_GATE_RANK), (0, 0))))
            q, k, v, r, la = _gla_proj(x, w_main, w_low, g2_hi, g2_lo, row(b_b_g[j]), dk, dv)
            ng = row(b_norm_g[j])
            s0_p = jnp.zeros((bp,) + state_gla.shape[2:], F32)
            o, s_p = _linrec(q, k, la, v, r, ng, s0_p, n_batch=bp, seq=lp, row_off=0, heads=B_HEADS)
            o, s_s = _linrec(q, k, la, v, r, ng, state_gla[j], n_batch=bs, seq=ls, row_off=tp,
                             heads=B_HEADS, alias=o)
            outs["gla_p"].append(s_p)
            outs["gla_s"].append(s_s)
            x1 = _mm_ln(o, b_w_out[j].astype(BF16), zero_bias, x, g1, b1, "gla_out")
        elif kind == 2:
            heads = state_hgrn.shape[2]
            q, lf, k, v, gt = _hgrn_proj(x, c_w_in[j].astype(BF16), row(lb_all[i]))
            ng = row(c_norm_g[j])
            s0_p = jnp.zeros((bp,) + state_hgrn.shape[2:], F32)
            o, s_p = _linrec(q, k, lf, v, gt, ng, s0_p, n_batch=bp, seq=lp, row_off=0, heads=heads)
            o, s_s = _linrec(q, k, lf, v, gt, ng, state_hgrn[j], n_batch=bs, seq=ls, row_off=tp,
                             heads=heads, alias=o)
            outs["hgrn_p"].append(s_p)
            outs["hgrn_s"].append(s_s)
            x1 = _mm_ln(o, c_w_out[j].astype(BF16), zero_bias, x, g1, b1, "hgrn_out")
        else:
            h = _glu(x, d_w_in[j].astype(BF16), row(d_b_in[j]))
            cargs = (d_w_dw[j], row(d_b_dw[j]), row(d_ln_g[j]), row(d_ln_b[j]))
            conv0 = jnp.zeros((bp, D_BUF, d), F32)
            c, s_p = _conv(h, conv0, *cargs, n_batch=bp, seq=lp, row_off=0)
            c, s_s = _conv(h, state_conv[j], *cargs, n_batch=bs, seq=ls, row_off=tp, alias=c)
            outs["conv_p"].append(s_p)
            outs["conv_s"].append(s_s)
            x1 = _mm_ln(c, d_w_out[j].astype(BF16), row(d_b_out[j]), x, g1, b1, "conf_out")

        q = _mm_rows(x1, m_w_q[i].astype(BF16), "attn_q")
        att = _attn_prompt(q, mem_k, mem_v, i, bp, lp)
        att = _attn_sample(q, cache_k, cache_v, i, bs, ls, tp, att)
        w_route = jnp.concatenate([r_w_grp[i], r_w_exp[i]], axis=1)
        w_route = jnp.pad(w_route, ((0, 0), (0, LANES - w_route.shape[1])))
        b_route = jnp.pad(jnp.concatenate([r_b_grp[i], r_b_exp[i]]), (0, LANES - MOE_GROUPS - MOE_EXPERTS))
        x2, logits = _mm_ln(att, m_w_o[i].astype(BF16), zero_bias, x1, row(ln_g[i, 1]), row(ln_b[i, 1]),
                            "attn_out", router=_hi_lo(w_route))

        route, counts = _route(logits, row(b_route), ltri)
        dest, blk_e, nused, tail_rows, n_blocks = _moe_plan(route, counts)
        xs = _dispatch(tail_rows, dest, x2, n_blocks * MOE_BLOCK_ROWS)
        yb = _experts(blk_e, nused, xs, e_w_gate_up, e_w_down, i)
        if i + 1 < DEPTH:
            x = _combine(dest, x2, route, row(ln_g[i, 2]), row(ln_b[i, 2]), yb)
        else:
            y_p, y_s = _combine(dest, x2, route, row(ln_g[i, 2]), row(ln_b[i, 2]), yb, split_rows=tp)

    y_prompt = y_p.reshape(bp, lp, d)
    y_sample = y_s.reshape(bs, ls, d)
    mem_k_p = _cache_unrows(mem_k_flat, MEM_HEADS)
    mem_v_p = _cache_unrows(mem_v_flat, MEM_HEADS)
    return (y_prompt, y_sample, mem_k_p, mem_v_p, jnp.stack(outs["gla_p"]), jnp.stack(outs["hgrn_p"]),
            jnp.stack(outs["conv_p"]), jnp.stack(outs["v"]), jnp.stack(outs["gla_s"]),
            jnp.stack(outs["hgrn_s"]), jnp.stack(outs["conv_s"]))
```

```python
import functools
import math

import jax
import jax.numpy as jnp
from jax import lax
from jax.experimental import pallas as pl
from jax.experimental.pallas import tpu as pltpu

F32 = jnp.float32
BF16 = jnp.bfloat16

D_MODEL = 1024
DEPTH = 4
N_MIXERS = 4
ALPHA = (2.0 * DEPTH) ** 0.25
LN_EPS = 1e-5
A_CHUNK = 128
A_GROUPS = 4
B_HEADS = 4
B_GATE_RANK = 16
B_GATE_TAU = 16.0
C_EXPAND = 128
D_CONV_W = 31
D_BUF = D_CONV_W - 1
LIN_CHUNK = 32
MEM_HEADS = 4
MOE_GROUPS = 4
MOE_PER_GROUP = 8
MOE_EXPERTS = MOE_GROUPS * MOE_PER_GROUP
MOE_HIDDEN = 512

LANES = 128
SUBLANES = 8
ROW_TILE = 512
MOE_BLOCK_ROWS = 256
ROUTE_LANE0 = MOE_GROUPS
VMEM_LIMIT = 56 * 1024 * 1024
INV_SQRT2 = 1.0 / math.sqrt(2.0)


def _params(*sem, vmem=VMEM_LIMIT):
    return pltpu.CompilerParams(dimension_semantics=sem, vmem_limit_bytes=vmem)


def _dot(a, b):
    return jnp.dot(a, b, preferred_element_type=F32)


def _dot_nt(a, b):
    return lax.dot_general(a, b, (((1,), (1,)), ((), ())), preferred_element_type=F32)


def _split3(a):
    hi = a.astype(BF16)
    r = a - hi.astype(F32)
    mid = r.astype(BF16)
    lo = (r - mid.astype(F32)).astype(BF16)
    return hi, mid, lo


def _dot_exact_lhs01(m01, a):
    hi, mid, lo = _split3(a)
    return _dot(m01, hi) + _dot(m01, mid) + _dot(m01, lo)


def _dot_hi(a, w_hi, w_lo):
    a_hi = a.astype(BF16)
    a_lo = (a - a_hi.astype(F32)).astype(BF16)
    return _dot(a_hi, w_hi) + _dot(a_lo, w_hi) + _dot(a_hi, w_lo)


def _layer_norm(x, g, b):
    mu = jnp.mean(x, axis=-1, keepdims=True)
    xc = x - mu
    var = jnp.mean(xc * xc, axis=-1, keepdims=True)
    return xc * lax.rsqrt(var + LN_EPS) * g + b


def _sigmoid(x):
    return 1.0 / (1.0 + jnp.exp(-x))


def _silu(x):
    return x * _sigmoid(x)


def _gelu(x):
    return 0.5 * x * (1.0 + lax.erf(x * INV_SQRT2))


def _log_sigmoid(x):
    return jnp.minimum(x, 0.0) - jnp.log(1.0 + jnp.exp(-jnp.abs(x)))


def _full(shape):
    return pl.BlockSpec(shape, lambda *_: (0,) * len(shape))


def _rows(tm, n, off=0):
    return pl.BlockSpec((tm, n), lambda i: (i + off, 0))


def _mm_kernel(a_ref, w_ref, o_ref):
    o_ref[...] = _dot(a_ref[...].astype(BF16), w_ref[...])


def _mem_proj_kernel(a_ref, w_ref, nat_ref, flat_ref, *, mem_len):
    res = _dot(a_ref[...].astype(BF16), w_ref[...])
    nat_ref[...] = res
    n_lt = res.shape[1] // (MEM_HEADS * LANES)
    for b in range(res.shape[0] // mem_len):
        for h in range(MEM_HEADS):
            for lt in range(n_lt):
                c0 = (h * n_lt + lt) * LANES
                flat_ref[b, pl.ds(lt * MEM_HEADS + h, mem_len, stride=MEM_HEADS * n_lt), :] = (
                    res[b * mem_len:(b + 1) * mem_len, c0:c0 + LANES])


def _mem_proj(a, w, mem_len):
    m, k = a.shape
    nl, _, n = w.shape
    tm = min(m, ROW_TILE)
    nbat = tm // mem_len
    return pl.pallas_call(
        functools.partial(_mem_proj_kernel, mem_len=mem_len),
        grid=(nl, m // tm),
        in_specs=[pl.BlockSpec((tm, k), lambda l, i: (i, 0)),
                  pl.BlockSpec((None, k, n), lambda l, i: (l, 0, 0))],
        out_specs=[pl.BlockSpec((None, tm, n), lambda l, i: (l, i, 0)),
                   pl.BlockSpec((None, nbat, mem_len * n // LANES, LANES), lambda l, i: (l, i, 0, 0))],
        out_shape=[jax.ShapeDtypeStruct((nl, m, n), F32),
                   jax.ShapeDtypeStruct((nl, m // mem_len, mem_len * n // LANES, LANES), F32)],
        compiler_params=_params("parallel", "parallel"),
        name="mem_kv_proj",
    )(a, w)


def _mm_rows(a, w, name):
    m, k = a.shape
    n = w.shape[1]
    tm = ROW_TILE
    return pl.pallas_call(
        _mm_kernel,
        grid=(m // tm,),
        in_specs=[_rows(tm, k), _full((k, n))],
        out_specs=_rows(tm, n),
        out_shape=jax.ShapeDtypeStruct((m, n), F32),
        compiler_params=_params("parallel"),
        name=name,
    )(a, w)


def _store_tile_rows(ref, val, lead=()):
    n = val.shape[0]
    for s in range(val.shape[1] // LANES):
        ref[lead + (pl.ds(s, n, stride=SUBLANES), slice(None))] = val[:, s * LANES:(s + 1) * LANES]


def _load_tile_rows(ref, n, lead=()):
    return jnp.concatenate([ref[lead + (pl.ds(s, n, stride=SUBLANES), slice(None))] for s in range(SUBLANES)],
                           axis=1)


def _mm_ln_kernel(a_ref, w_ref, bias_ref, x_ref, g_ref, b_ref, *rest, with_logits):
    h = _dot(a_ref[...].astype(BF16), w_ref[...]) + bias_ref[...]
    y = _layer_norm(ALPHA * x_ref[...] + h, g_ref[...], b_ref[...])
    if with_logits:
        wr_hi_ref, wr_lo_ref, o_ref, lg_ref = rest
        lg_ref[...] = _dot_hi(y, wr_hi_ref[...], wr_lo_ref[...])
        _store_tile_rows(o_ref, y)
    else:
        (o_ref,) = rest
        o_ref[...] = y


def _mm_ln(a, w, bias, x, g, b, name, router=None):
    m, k = a.shape
    d = x.shape[1]
    tm = ROW_TILE
    in_specs = [_rows(tm, k), _full((k, d)), _full((1, d)), _rows(tm, d), _full((1, d)), _full((1, d))]
    args = [a, w, bias, x, g, b]
    out_specs = _rows(tm, d)
    out_shape = jax.ShapeDtypeStruct((m, d), F32)
    if router is not None:
        in_specs += [_full((d, LANES)), _full((d, LANES))]
        args += list(router)
        assert d == SUBLANES * LANES
        out_specs = [_rows(tm * SUBLANES, LANES), _rows(tm, LANES)]
        out_shape = [jax.ShapeDtypeStruct((m * SUBLANES, LANES), F32), jax.ShapeDtypeStruct((m, LANES), F32)]
    return pl.pallas_call(
        functools.partial(_mm_ln_kernel, with_logits=router is not None),
        grid=(m // tm,),
        in_specs=in_specs,
        out_specs=out_specs,
        out_shape=out_shape,
        compiler_params=_params("parallel"),
        name=name,
    )(*args)


def _gmlp_kernel(x_ref, w_in_ref, b_in_ref, lng_ref, lnb_ref, wc_ref, bc_ref, w_out_ref, b_out_ref,
                 g_ref, b_ref, *rest, emit_v, n_chunks):
    if emit_v:
        _alias_ref, o_ref, v_ref, vn_ref = rest
    else:
        o_ref, vn_ref = rest
    half = w_out_ref.shape[0]
    gw = half // A_GROUPS
    x = x_ref[...]
    xb = x.astype(BF16)
    v = _gelu(_dot(xb, w_in_ref[:, half:]) + b_in_ref[:, half:])
    vn = _layer_norm(v, lng_ref[...], lnb_ref[...])
    vn_ref[...] = vn
    if emit_v:
        v_ref[...] = vn
    acc = jnp.zeros(x.shape, F32)
    for grp in range(A_GROUPS):
        cols = slice(grp * gw, (grp + 1) * gw)
        u = _gelu(_dot(xb, w_in_ref[:, cols]) + b_in_ref[:, cols])
        mixed = []
        for c in range(n_chunks):
            vc = vn_ref[c * A_CHUNK:(c + 1) * A_CHUNK, cols].astype(BF16)
            mixed.append(_dot(wc_ref[grp], vc) + bc_ref[grp])
        mixed = mixed[0] if n_chunks == 1 else jnp.concatenate(mixed, axis=0)
        acc = acc + _dot((u * mixed).astype(BF16), w_out_ref[cols, :])
    h = acc + b_out_ref[...]
    o_ref[...] = _layer_norm(ALPHA * x + h, g_ref[...], b_ref[...])


def _gmlp(x, row_off, t, w_in, b_in, lng, lnb, wc, bc, w_out, b_out, g, b, alias=None):
    n_rows, d = x.shape
    ffn = w_in.shape[1]
    half = ffn // 2
    tm = 2 * A_CHUNK
    emit_v = alias is not None
    off = row_off // tm
    in_specs = [_rows(tm, d), _full((d, ffn)), _full((1, ffn)), _full((1, half)), _full((1, half)),
                _full((A_GROUPS, A_CHUNK, A_CHUNK)), _full((A_GROUPS, A_CHUNK, 1)), _full((half, d)),
                _full((1, d)), _full((1, d)), _full((1, d))]
    args = [x, w_in, b_in, lng, lnb, wc, bc, w_out, b_out, g, b]
    out_specs = _rows(tm, d, off)
    out_shape = jax.ShapeDtypeStruct((t, d), F32)
    aliases = {}
    if emit_v:
        in_specs.append(pl.BlockSpec(memory_space=pl.ANY))
        args.append(alias)
        aliases = {len(args) - 1: 0}
        out_specs = [out_specs, _rows(tm, half)]
        out_shape = [out_shape, jax.ShapeDtypeStruct((n_rows, half), F32)]
    return pl.pallas_call(
        functools.partial(_gmlp_kernel, emit_v=emit_v, n_chunks=tm // A_CHUNK),
        grid=(n_rows // tm,),
        in_specs=in_specs,
        out_specs=out_specs,
        out_shape=out_shape,
        scratch_shapes=[pltpu.VMEM((tm, half), F32)],
        input_output_aliases=aliases,
        compiler_params=_params("parallel"),
        name="gmlp_sample" if emit_v else "gmlp_prompt",
    )(*args)


def _gla_proj_kernel(x_ref, w_ref, wlow_ref, g2_hi_ref, g2_lo_ref, bg_ref,
                     q_ref, k_ref, v_ref, r_ref, la_ref, *, dk, dv, q_scale):
    xb = x_ref[...].astype(BF16)
    q_ref[...] = _dot(xb, w_ref[:, 0:dk]) * q_scale
    k_ref[...] = _dot(xb, w_ref[:, dk:2 * dk])
    v_ref[...] = _dot(xb, w_ref[:, 2 * dk:2 * dk + dv])
    r_ref[...] = _dot(xb, w_ref[:, 2 * dk + dv:2 * dk + 2 * dv])
    g_low = _dot(xb, wlow_ref[...])
    pre = _dot_hi(g_low, g2_hi_ref[...], g2_lo_ref[...]) + bg_ref[...]
    la_ref[...] = _log_sigmoid(pre) * (1.0 / B_GATE_TAU)


def _gla_proj(x, w_main, w_low, g2_hi, g2_lo, bg, dk, dv):
    t, d = x.shape
    tm = ROW_TILE
    shapes = [dk, dk, dv, dv, dk]
    return pl.pallas_call(
        functools.partial(_gla_proj_kernel, dk=dk, dv=dv, q_scale=(dk // B_HEADS) ** -0.5),
        grid=(t // tm,),
        in_specs=[_rows(tm, d), _full(w_main.shape), _full(w_low.shape), _full(g2_hi.shape),
                  _full(g2_lo.shape), _full((1, dk))],
        out_specs=[_rows(tm, n) for n in shapes],
        out_shape=[jax.ShapeDtypeStruct((t, n), F32) for n in shapes],
        compiler_params=_params("parallel"),
        name="gla_proj",
    )(x, w_main, w_low, g2_hi, g2_lo, bg)


def _hgrn_proj_kernel(x_ref, w_ref, lb_ref, q_ref, lf_ref, k_ref, v_ref, gt_ref, *, d, q_scale):
    xb = x_ref[...].astype(BF16)
    lb = lb_ref[...]
    q_ref[...] = _silu(_dot(xb, w_ref[:, 0:d])) * q_scale
    f = _dot(xb, w_ref[:, d:2 * d])
    lf_ref[...] = jnp.log(lb + (1.0 - lb) * _sigmoid(f))
    k_ref[...] = (1.0 - lb) * _sigmoid(-f)
    v_ref[...] = _dot(xb, w_ref[:, 2 * d:3 * d])
    gt_ref[...] = _dot(xb, w_ref[:, 3 * d:4 * d])


def _hgrn_proj(x, w, lb):
    t, d = x.shape
    tm = ROW_TILE
    return pl.pallas_call(
        functools.partial(_hgrn_proj_kernel, d=d, q_scale=C_EXPAND ** -0.5),
        grid=(t // tm,),
        in_specs=[_rows(tm, d), _full(w.shape), _full((1, d))],
        out_specs=[_rows(tm, d)] * 5,
        out_shape=[jax.ShapeDtypeStruct((t, d), F32)] * 5,
        compiler_params=_params("parallel"),
        name="hgrn_proj",
    )(x, w, lb)


def _cumsum_rows(x, chunk):
    pos = lax.broadcasted_iota(jnp.int32, x.shape, 0) & (chunk - 1)
    step = 1
    while step < chunk:
        x = x + jnp.where(pos >= step, pltpu.roll(x, step, axis=0), 0.0)
        step *= 2
    return x


def _chunk_rows(x, row, chunk, n_chunks):
    parts = [jnp.broadcast_to(x[c * chunk + row:c * chunk + row + 1, :], (chunk, x.shape[1]))
             for c in range(n_chunks)]
    return parts[0] if n_chunks == 1 else jnp.concatenate(parts, axis=0)


def _linrec_kernel(q_ref, k_ref, g_ref, v_ref, gate_ref, ng_ref, s0_ref, *rest,
                   heads, dk, dv, chunk, n_chunks, nb, aliased):
    if aliased:
        _alias_ref, o_ref, sout_ref, st_ref = rest
    else:
        o_ref, sout_ref, st_ref = rest
    li = pl.program_id(1)
    tl = chunk * n_chunks
    r_id = lax.broadcasted_iota(jnp.int32, (tl, tl), 0)
    c_id = lax.broadcasted_iota(jnp.int32, (tl, tl), 1)
    mask = (r_id >= c_id) & ((r_id // chunk) == (c_id // chunk))

    for n in range(nb):
        @pl.when(li == 0)
        def _():
            st_ref[...] = s0_ref[n]

        rs = slice(n * tl, (n + 1) * tl)
        for h in range(heads):
            ks = slice(h * dk, (h + 1) * dk)
            vs = slice(h * dv, (h + 1) * dv)
            bcum = _cumsum_rows(g_ref[rs, ks], chunk)
            b_mid = _chunk_rows(bcum, chunk // 2 - 1, chunk, n_chunks)
            b_end = _chunk_rows(bcum, chunk - 1, chunk, n_chunks)
            q = q_ref[rs, ks]
            k = k_ref[rs, ks]
            vb = v_ref[rs, vs].astype(BF16)
            q_in = (q * jnp.exp(bcum)).astype(BF16)
            q_a = (q * jnp.exp(bcum - b_mid)).astype(BF16)
            k_a = (k * jnp.exp(b_mid - bcum)).astype(BF16)
            k_end = k * jnp.exp(b_end - bcum)
            a = jnp.where(mask, _dot_nt(q_a, k_a), 0.0).astype(BF16)
            o_intra = _dot(a, vb)
            st = st_ref[h]
            outs = []
            for c in range(n_chunks):
                cr = slice(c * chunk, (c + 1) * chunk)
                outs.append(o_intra[cr] + _dot(q_in[cr], st.astype(BF16)))
                decay = jnp.exp(bcum[cr].T[:, chunk - 1:chunk])
                st = st * decay + _dot(k_end[cr].T.astype(BF16), vb[cr])
            st_ref[h] = st
            o = outs[0] if n_chunks == 1 else jnp.concatenate(outs, axis=0)
            o = o * lax.rsqrt(jnp.mean(o * o, axis=-1, keepdims=True) + LN_EPS) * ng_ref[:, vs]
            o_ref[rs, vs] = o * _silu(gate_ref[rs, vs])
        sout_ref[n] = st_ref[...]


LINREC_CHUNK = 2 * LIN_CHUNK
LINREC_SAMPLE_NB = 4


def _linrec(q, k, g, v, gate, ng, s0, *, n_batch, seq, row_off, heads, alias=None):
    t, hk = q.shape
    hv = v.shape[1]
    dk, dv = hk // heads, hv // heads
    chunk = LINREC_CHUNK if seq % LINREC_CHUNK == 0 else seq
    tl = min(seq, 4 * chunk)
    n_l = seq // tl
    nb = LINREC_SAMPLE_NB if n_l == 1 else 1
    off = row_off // (nb * tl)

    def rows(n):
        return pl.BlockSpec((nb * tl, n), lambda b, l: (off + b * n_l + l, 0))

    state_spec = pl.BlockSpec((nb, heads, dk, dv), lambda b, l: (b, 0, 0, 0))
    in_specs = [rows(hk), rows(hk), rows(hk), rows(hv), rows(hv), pl.BlockSpec((1, hv), lambda b, l: (0, 0)),
                state_spec]
    args = [q, k, g, v, gate, ng, s0]
    aliases = {}
    if alias is not None:
        in_specs.append(pl.BlockSpec(memory_space=pl.ANY))
        args.append(alias)
        aliases = {len(args) - 1: 0}
    return pl.pallas_call(
        functools.partial(_linrec_kernel, heads=heads, dk=dk, dv=dv, chunk=chunk, n_chunks=tl // chunk, nb=nb,
                          aliased=alias is not None),
        grid=(n_batch // nb, n_l),
        in_specs=in_specs,
        out_specs=[rows(hv), state_spec],
        out_shape=[jax.ShapeDtypeStruct((t, hv), F32), jax.ShapeDtypeStruct((n_batch, heads, dk, dv), F32)],
        scratch_shapes=[pltpu.VMEM((heads, dk, dv), F32)],
        input_output_aliases=aliases,
        compiler_params=_params("parallel", "arbitrary"),
        name="linrec_h%d_%s" % (heads, "sample" if alias is not None else "prompt"),
    )(*args)


def _glu_kernel(x_ref, w_ref, b_ref, o_ref, *, d):
    xb = x_ref[...].astype(BF16)
    a = _dot(xb, w_ref[:, 0:d]) + b_ref[:, 0:d]
    gate = _dot(xb, w_ref[:, d:2 * d]) + b_ref[:, d:2 * d]
    o_ref[...] = a * _sigmoid(gate)


def _glu(x, w, b):
    t, d = x.shape
    tm = ROW_TILE
    return pl.pallas_call(
        functools.partial(_glu_kernel, d=d),
        grid=(t // tm,),
        in_specs=[_rows(tm, d), _full(w.shape), _full(b.shape)],
        out_specs=_rows(tm, d),
        out_shape=jax.ShapeDtypeStruct((t, d), F32),
        compiler_params=_params("parallel"),
        name="conf_glu",
    )(x, w, b)


CONV_PAD = 32


def _conv_kernel(h_ref, st_ref, wdw_ref, bdw_ref, lg_ref, lb_ref, *rest, nb, tl, aliased):
    if aliased:
        _alias_ref, o_ref, sout_ref, buf_ref, conv_ref = rest
    else:
        o_ref, sout_ref, buf_ref, conv_ref = rest
    li = pl.program_id(1)
    d = h_ref.shape[-1]
    lead = CONV_PAD - D_BUF
    rb = min(tl, 64)
    cw = LANES
    buf_ref[CONV_PAD + tl:CONV_PAD + tl + SUBLANES, :] = jnp.zeros((SUBLANES, d), F32)
    for n in range(nb):
        @pl.when(li == 0)
        def _():
            buf_ref[lead:CONV_PAD, :] = st_ref[n]

        buf_ref[CONV_PAD:CONV_PAD + tl, :] = h_ref[n * tl:(n + 1) * tl, :]
        for r0 in range(0, tl, rb):
            for c0 in range(0, d, cw):
                cols = slice(c0, c0 + cw)
                acc = jnp.zeros((rb, cw), F32)
                for s in range(SUBLANES):
                    part = None
                    for a in range((CONV_PAD + SUBLANES) // SUBLANES):
                        j = SUBLANES * a + s - lead
                        if 0 <= j < D_CONV_W:
                            rows = slice(r0 + SUBLANES * a, r0 + SUBLANES * a + rb + SUBLANES)
                            term = wdw_ref[j:j + 1, cols] * buf_ref[rows, cols]
                            part = term if part is None else part + term
                    acc = acc + part[s:s + rb, :]
                conv_ref[r0:r0 + rb, cols] = acc + bdw_ref[:, cols]
        y = _layer_norm(conv_ref[...], lg_ref[...], lb_ref[...])
        o_ref[n * tl:(n + 1) * tl, :] = _silu(y)
        sout_ref[n] = buf_ref[tl + lead:tl + CONV_PAD, :]
        buf_ref[0:CONV_PAD, :] = buf_ref[tl:tl + CONV_PAD, :]


def _conv(h, state, wdw, bdw, lg, lb, *, n_batch, seq, row_off, alias=None):
    t, d = h.shape
    tl = min(seq, 256)
    n_l = seq // tl
    nb = 8 if n_l == 1 else 1
    off = row_off // (nb * tl)
    rows = pl.BlockSpec((nb * tl, d), lambda b, l: (off + b * n_l + l, 0))
    state_spec = pl.BlockSpec((nb, D_BUF, d), lambda b, l: (b, 0, 0))
    in_specs = [rows, state_spec, _full(wdw.shape), _full((1, d)), _full((1, d)), _full((1, d))]
    args = [h, state, wdw, bdw, lg, lb]
    aliases = {}
    if alias is not None:
        in_specs.append(pl.BlockSpec(memory_space=pl.ANY))
        args.append(alias)
        aliases = {len(args) - 1: 0}
    return pl.pallas_call(
        functools.partial(_conv_kernel, nb=nb, tl=tl, aliased=alias is not None),
        grid=(n_batch // nb, n_l),
        in_specs=in_specs,
        out_specs=[rows, state_spec],
        out_shape=[jax.ShapeDtypeStruct((t, d), F32), jax.ShapeDtypeStruct((n_batch, D_BUF, d), F32)],
        scratch_shapes=[pltpu.VMEM((CONV_PAD + tl + SUBLANES, d), F32), pltpu.VMEM((tl, d), F32)],
        input_output_aliases=aliases,
        compiler_params=_params("parallel", "arbitrary"),
        name="conf_conv_sample" if alias is not None else "conf_conv_prompt",
    )(*args)


def _attn_heads(q, k_of, v_of, hd):
    outs = []
    for h in range(MEM_HEADS):
        hs = slice(h * hd, (h + 1) * hd)
        s = _dot_nt(q[:, hs].astype(BF16), k_of(hs).astype(BF16)) * (hd ** -0.5)
        p = jnp.exp(s - jnp.max(s, axis=-1, keepdims=True))
        denom = jnp.sum(p, axis=-1, keepdims=True)
        outs.append(_dot(p.astype(BF16), v_of(hs).astype(BF16)) / denom)
    return outs


def _attn_prompt_kernel(q_ref, k_ref, v_ref, o_ref):
    hd = q_ref.shape[-1] // MEM_HEADS
    outs = _attn_heads(q_ref[...], lambda hs: k_ref[:, hs], lambda hs: v_ref[:, hs], hd)
    for h, o in enumerate(outs):
        o_ref[:, h * hd:(h + 1) * hd] = o


def _attn_prompt(q, mem_k, mem_v, layer, n_batch, seq):
    t, d = q.shape
    m = mem_k.shape[1] // n_batch
    tl = ROW_TILE
    n_l = seq // tl
    rows = pl.BlockSpec((tl, d), lambda b, l: (b * n_l + l, 0))
    kv = pl.BlockSpec((None, m, d), lambda b, l: (layer, b, 0))
    return pl.pallas_call(
        _attn_prompt_kernel,
        grid=(n_batch, n_l),
        in_specs=[rows, kv, kv],
        out_specs=rows,
        out_shape=jax.ShapeDtypeStruct((t, d), F32),
        compiler_params=_params("parallel", "parallel"),
        name="attn_prompt",
    )(q, mem_k, mem_v)


ATTN_SAMPLE_NB = 4


def _attn_sample_kernel(q_ref, k_ref, v_ref, _alias_ref, o_ref, *, nb, seq):
    hd = q_ref.shape[-1] // MEM_HEADS
    n_lt = hd // LANES
    m = k_ref.shape[1] // (MEM_HEADS * n_lt)

    def head(ref, n, h):
        tiles = [ref[n, pl.ds(lt * MEM_HEADS + h, m, stride=MEM_HEADS * n_lt), :] for lt in range(n_lt)]
        return jnp.concatenate(tiles, axis=1).astype(BF16)

    pairs = [(n, h) for n in range(nb) for h in range(MEM_HEADS)]
    s = jnp.concatenate(
        [_dot_nt(q_ref[n * seq:(n + 1) * seq, h * hd:(h + 1) * hd].astype(BF16), head(k_ref, n, h))
         for n, h in pairs], axis=0) * (hd ** -0.5)
    p = jnp.exp(s - jnp.max(s, axis=-1, keepdims=True))
    inv = 1.0 / jnp.sum(p, axis=-1, keepdims=True)
    for idx, (n, h) in enumerate(pairs):
        rs = slice(idx * seq, (idx + 1) * seq)
        o = _dot(p[rs].astype(BF16), head(v_ref, n, h)) * inv[rs]
        o_ref[n * seq:(n + 1) * seq, h * hd:(h + 1) * hd] = o


def _cache_rows(cache):
    nl, nbat, m, heads, hd = cache.shape
    c = cache.reshape(nl, nbat, m, heads, hd // LANES, LANES)
    return c.transpose(0, 1, 2, 4, 3, 5).reshape(nl, nbat, m * (hd // LANES) * heads, LANES)


def _cache_unrows(flat, heads):
    nl, nbat, rows, _ = flat.shape
    n_lt = D_MODEL // (heads * LANES)
    m = rows // (heads * n_lt)
    c = flat.reshape(nl, nbat, m, n_lt, heads, LANES).transpose(0, 1, 2, 4, 3, 5)
    return c.reshape(nl, nbat, m, heads, n_lt * LANES)


def _attn_sample(q, cache_k, cache_v, layer, n_batch, seq, row_off, alias):
    t, d = q.shape
    nb = ATTN_SAMPLE_NB
    off = row_off // (nb * seq)
    rows = pl.BlockSpec((nb * seq, d), lambda i: (off + i, 0))
    kv = pl.BlockSpec((None, nb) + cache_k.shape[2:], lambda i: (layer, i, 0, 0))
    return pl.pallas_call(
        functools.partial(_attn_sample_kernel, nb=nb, seq=seq),
        grid=(n_batch // nb,),
        in_specs=[rows, kv, kv, pl.BlockSpec(memory_space=pl.ANY)],
        out_specs=rows,
        out_shape=jax.ShapeDtypeStruct((t, d), F32),
        input_output_aliases={3: 0},
        compiler_params=_params("parallel"),
        name="attn_sample",
    )(q, cache_k, cache_v, alias)


def _route_kernel(lg_ref, bias_ref, ltri_ref, route_ref, counts_ref, carry_ref):
    i = pl.program_id(0)

    @pl.when(i == 0)
    def _():
        carry_ref[...] = jnp.zeros(carry_ref.shape, F32)

    z = lg_ref[...] + bias_ref[...]
    lane = lax.broadcasted_iota(jnp.int32, z.shape, 1).astype(F32)
    neg = -jnp.inf
    far = float(LANES)

    def first_max(mask):
        vmax = jnp.max(jnp.where(mask, z, neg), axis=-1, keepdims=True)
        idx = jnp.min(jnp.where(mask & (z == vmax), lane, far), axis=-1, keepdims=True)
        return vmax, idx

    gmask = lane < float(MOE_GROUPS)
    gmax, gidx = first_max(gmask)
    gsum = jnp.sum(jnp.where(gmask, jnp.exp(z - gmax), 0.0), axis=-1, keepdims=True)
    g_w = 1.0 / gsum
    lo = float(ROUTE_LANE0) + float(MOE_PER_GROUP) * gidx
    emask = (lane >= lo) & (lane < lo + float(MOE_PER_GROUP))
    v1, i1 = first_max(emask)
    v2, i2 = first_max(emask & (lane != i1))
    tt = jnp.exp(v2 - v1)
    w0 = g_w / (1.0 + tt)
    w1 = g_w * tt / (1.0 + tt)
    sel1 = lane == i1
    sel2 = lane == i2
    onehot = jnp.where(sel1 | sel2, 1.0, 0.0)
    before = _dot(ltri_ref[...], onehot.astype(BF16)) + carry_ref[...]
    rank0 = jnp.sum(jnp.where(sel1, before, 0.0), axis=-1, keepdims=True)
    rank1 = jnp.sum(jnp.where(sel2, before, 0.0), axis=-1, keepdims=True)
    carry = carry_ref[...] + jnp.sum(onehot, axis=0, keepdims=True)
    carry_ref[...] = carry
    counts_ref[...] = carry
    e_off = float(ROUTE_LANE0)
    out = jnp.zeros(z.shape, F32)
    for ln, val in enumerate((i1 - e_off, i2 - e_off, w0, w1, rank0, rank1)):
        out = jnp.where(lane == float(ln), val, out)
    route_ref[...] = out


def _route(logits, bias, ltri):
    t = logits.shape[0]
    tm = ROW_TILE
    return pl.pallas_call(
        _route_kernel,
        grid=(t // tm,),
        in_specs=[_rows(tm, LANES), _full((1, LANES)), _full((tm, tm))],
        out_specs=[_rows(tm, LANES), _full((1, LANES))],
        out_shape=[jax.ShapeDtypeStruct((t, LANES), F32), jax.ShapeDtypeStruct((1, LANES), F32)],
        scratch_shapes=[pltpu.VMEM((1, LANES), F32)],
        compiler_params=_params("arbitrary"),
        name="moe_route",
    )(logits, bias, ltri)


def _row_copy_wait(src_rows, dst_rows, sem):
    pltpu.make_async_copy(src_rows, dst_rows, sem).wait()


MOE_IDX_BLOCKS = 4


def _expert_kernel(blk_e_ref, nused_ref, src_ref, src_next_ref, dst_ref, x_hbm, wgu_ref, wd_ref, slots_hbm,
                   xbuf, ybuf, wgu_bf, wd_bf, gsem, ssem):
    i = pl.program_id(0)
    nused = nused_ref[0]
    bm = xbuf.shape[1] // SUBLANES
    tile = lambda ref, row8: ref.at[pl.ds(pl.multiple_of(row8, SUBLANES), SUBLANES)]

    def window(b):
        return lax.rem(b, MOE_IDX_BLOCKS) * bm

    def gather(idx_ref, b):
        base = window(b)
        slot = lax.rem(b, 2)

        def issue(grp, carry):
            r0 = grp * SUBLANES
            for k in range(SUBLANES):
                pltpu.make_async_copy(tile(x_hbm, idx_ref[base + r0 + k]),
                                      tile(xbuf.at[slot], (r0 + k) * SUBLANES), gsem.at[slot]).start()
            return carry

        lax.fori_loop(0, bm // SUBLANES, issue, 0)

    def scatter_wait(slot):
        _row_copy_wait(ybuf.at[slot], slots_hbm.at[pl.ds(0, bm * SUBLANES)], ssem.at[slot])

    @pl.when(i == 0)
    def _():
        gather(src_ref, i)

    @pl.when(i + 1 < nused)
    def _():
        gather(src_next_ref, i + 1)

    prev = blk_e_ref[jnp.maximum(i - 1, 0)]

    @pl.when((i == 0) | (blk_e_ref[i] != prev))
    def _():
        wgu_bf[...] = wgu_ref[...].astype(BF16)
        wd_bf[...] = wd_ref[...].astype(BF16)

    @pl.when(i < nused)
    def _():
        slot = lax.rem(i, 2)
        _row_copy_wait(x_hbm.at[pl.ds(0, bm * SUBLANES)], xbuf.at[slot], gsem.at[slot])

        @pl.when(i >= 2)
        def _():
            scatter_wait(slot)

        hid = wd_bf.shape[0]
        xb = _load_tile_rows(xbuf, bm, (slot,)).astype(BF16)
        gate = _dot(xb, wgu_bf[:, 0:hid])
        up = _dot(xb, wgu_bf[:, hid:2 * hid])
        _store_tile_rows(ybuf, _dot((_silu(gate) * up).astype(BF16), wd_bf[...]), (slot,))
        base = window(i)

        def issue(grp, carry):
            r0 = grp * SUBLANES
            for k in range(SUBLANES):
                pltpu.make_async_copy(tile(ybuf.at[slot], (r0 + k) * SUBLANES),
                                      tile(slots_hbm, dst_ref[base + r0 + k]), ssem.at[slot]).start()
            return carry

        lax.fori_loop(0, bm // SUBLANES, issue, 0)

        @pl.when(i == nused - 1)
        def _():
            scatter_wait(slot)

            @pl.when(i >= 1)
            def _():
                scatter_wait(1 - slot)


def _experts(blk_e, nused, src_tok, dst_row, x, w_gate_up, w_down, layer, n_slot_rows):
    d = x.shape[1] * SUBLANES
    bm = MOE_BLOCK_ROWS
    n_blocks = src_tok.shape[0] // bm
    win = MOE_IDX_BLOCKS * bm
    n_win = src_tok.shape[0] // win
    hid2 = w_gate_up.shape[-1]
    hid = w_down.shape[-2]
    smem = lambda f: pl.BlockSpec((win,), f, memory_space=pltpu.SMEM)
    grid_spec = pltpu.PrefetchScalarGridSpec(
        num_scalar_prefetch=2,
        grid=(n_blocks,),
        in_specs=[smem(lambda i, be, nu: (i // MOE_IDX_BLOCKS,)),
                  smem(lambda i, be, nu: (jnp.minimum((i + 1) // MOE_IDX_BLOCKS, n_win - 1),)),
                  smem(lambda i, be, nu: (i // MOE_IDX_BLOCKS,)),
                  pl.BlockSpec(memory_space=pl.ANY),
                  pl.BlockSpec((None, None, d, hid2), lambda i, be, nu: (layer, be[i], 0, 0)),
                  pl.BlockSpec((None, None, hid, d), lambda i, be, nu: (layer, be[i], 0, 0))],
        out_specs=pl.BlockSpec(memory_space=pl.ANY),
        scratch_shapes=[pltpu.VMEM((2, bm * SUBLANES, LANES), F32), pltpu.VMEM((2, bm * SUBLANES, LANES), F32),
                        pltpu.VMEM((d, hid2), BF16), pltpu.VMEM((hid, d), BF16),
                        pltpu.SemaphoreType.DMA((2,)), pltpu.SemaphoreType.DMA((2,))],
    )
    return pl.pallas_call(
        _expert_kernel,
        grid_spec=grid_spec,
        out_shape=jax.ShapeDtypeStruct((n_slot_rows * SUBLANES, LANES), F32),
        compiler_params=_params("arbitrary"),
        name="moe_experts",
    )(blk_e, nused, src_tok, src_tok, dst_row, x, w_gate_up, w_down)


def _combine_kernel(x_ref, route_ref, g_ref, b_ref, y0_ref, y1_ref, *rest, n_first):
    if n_first is None:
        (o_ref,) = rest
    else:
        o_ref, o2_ref = rest
    i = pl.program_id(0)
    tm = route_ref.shape[0]
    moe = route_ref[:, 2:3] * _load_tile_rows(y0_ref, tm) + route_ref[:, 3:4] * _load_tile_rows(y1_ref, tm)
    y = _layer_norm(ALPHA * _load_tile_rows(x_ref, tm) + moe, g_ref[...], b_ref[...])
    if n_first is None:
        o_ref[...] = y
    else:
        @pl.when(i < n_first)
        def _():
            o_ref[...] = y

        @pl.when(i >= n_first)
        def _():
            o2_ref[...] = y


def _combine(x, route, g, b, slots, split_rows=None):
    t, d = x.shape[0] // SUBLANES, x.shape[1] * SUBLANES
    tm = ROW_TILE
    n_tiles = t // tm
    tiles = lambda off: _rows(tm * SUBLANES, LANES, off)
    out_specs = _rows(tm, d)
    out_shape = jax.ShapeDtypeStruct((t, d), F32)
    n_first = None
    if split_rows is not None:
        n_first = split_rows // tm
        out_specs = [pl.BlockSpec((tm, d), lambda i: (jnp.minimum(i, n_first - 1), 0)),
                     pl.BlockSpec((tm, d), lambda i: (jnp.maximum(i - n_first, 0), 0))]
        out_shape = [jax.ShapeDtypeStruct((split_rows, d), F32), jax.ShapeDtypeStruct((t - split_rows, d), F32)]
    return pl.pallas_call(
        functools.partial(_combine_kernel, n_first=n_first),
        grid=(n_tiles,),
        in_specs=[tiles(0), _rows(tm, LANES), _full((1, d)), _full((1, d)), tiles(0), tiles(n_tiles)],
        out_specs=out_specs,
        out_shape=out_shape,
        compiler_params=_params("arbitrary"),
        name="moe_combine",
    )(x, route, g, b, slots, slots)


def _moe_plan(route, counts):
    bm = MOE_BLOCK_ROWS
    t = route.shape[0]
    n_blocks = -(-2 * t // bm) + MOE_EXPERTS
    n_blocks = -(-n_blocks // MOE_IDX_BLOCKS) * MOE_IDX_BLOCKS
    e_idx = route[:, 0:2].astype(jnp.int32)
    rank = route[:, 4:6].astype(jnp.int32)
    cnt = counts[0, ROUTE_LANE0:ROUTE_LANE0 + MOE_EXPERTS].astype(jnp.int32)
    padded = ((cnt + bm - 1) // bm) * bm
    pad_end = jnp.cumsum(padded)
    pad_start = pad_end - padded
    experts = jnp.arange(MOE_EXPERTS, dtype=jnp.int32)
    dest = (rank + jnp.sum(jnp.where(e_idx[..., None] == experts, pad_start, 0), axis=-1)).reshape(-1)
    nused = pad_end[-1] // bm
    blk = jnp.arange(n_blocks, dtype=jnp.int32)
    blk_e = jnp.sum((pad_end[None, :] <= (blk * bm)[:, None]).astype(jnp.int32), axis=1)
    blk_e = jnp.minimum(blk_e, MOE_EXPERTS - 1)
    last_e = jnp.max(jnp.where(cnt > 0, experts, 0))
    blk_e = jnp.where(blk < nused, blk_e, last_e)
    n_rows = n_blocks * bm
    pair = jnp.full((n_rows,), -1, jnp.int32).at[dest].set(jnp.arange(2 * t, dtype=jnp.int32))
    r = jnp.arange(n_rows, dtype=jnp.int32)
    src_tok = jnp.where(pair >= 0, pair // 2, 0) * SUBLANES
    dst_row = jnp.where(pair >= 0, (pair % 2) * t + pair // 2, 2 * t + r % (2 * bm)) * SUBLANES
    return (src_tok.astype(jnp.int32), dst_row.astype(jnp.int32), blk_e.astype(jnp.int32),
            nused.reshape(1).astype(jnp.int32), 2 * t + 2 * bm)


def _hi_lo(w):
    hi = w.astype(BF16)
    return hi, (w - hi.astype(F32)).astype(BF16)


def kernel(x_prompt, x_sample, mem_prompt, cache_mem_k, cache_mem_v, state_gla, state_hgrn, state_conv,
           ln_g, ln_b, a_w_in, a_b_in, a_ln_g, a_ln_b, a_w_s, a_b_s, a_w_out, a_b_out,
           b_w_in, b_w_g2, b_b_g, b_norm_g, b_w_out, c_lb, c_w_in, c_norm_g, c_w_out,
           d_w_in, d_b_in, d_w_dw, d_b_dw, d_ln_g, d_ln_b, d_w_out, d_b_out,
           m_w_q, m_w_k, m_w_v, m_w_o, r_w_grp, r_b_grp, r_w_exp, r_b_exp, e_w_gate_up, e_w_down):
    bp, lp, d = x_prompt.shape
    bs, ls, _ = x_sample.shape
    tp, ts = bp * lp, bs * ls
    t = tp + ts
    mem_len = mem_prompt.shape[1]
    hd = d // MEM_HEADS

    row = lambda a: a.reshape(1, -1)
    x = None

    mem2d = mem_prompt.reshape(bp * mem_len, d)
    mem_k, mem_k_flat = _mem_proj(mem2d, m_w_k.astype(BF16), mem_len)
    mem_v, mem_v_flat = _mem_proj(mem2d, m_w_v.astype(BF16), mem_len)

    cache_k = _cache_rows(cache_mem_k)
    cache_v = _cache_rows(cache_mem_v)

    lb_all = jnp.cumsum(jax.nn.softmax(c_lb.astype(F32), axis=0), axis=0)
    lb_all = lb_all - lb_all[:1]
    ltri = jnp.tril(jnp.ones((ROW_TILE, ROW_TILE), F32), -1).astype(BF16)
    zero_bias = jnp.zeros((1, d), F32)

    outs = {"v": [], "gla_p": [], "gla_s": [], "hgrn_p": [], "hgrn_s": [], "conv_p": [], "conv_s": []}
    for i in range(DEPTH):
        j = i // N_MIXERS
        kind = i % N_MIXERS
        g1, b1 = row(ln_g[i, 0]), row(ln_b[i, 0])
        if kind == 0:
            tril = jnp.tril(jnp.ones((A_CHUNK, A_CHUNK), bool))
            wc_p = jnp.where(tril, a_w_s[j], 0.0).astype(BF16)
            bc_p = a_b_s[j][:, :, None]
            reps = A_CHUNK // ls
            small = jnp.where(jnp.tril(jnp.ones((ls, ls), bool)), a_w_s[j][:, :ls, :ls], 0.0)
            wc_s = jax.vmap(lambda m: jnp.kron(jnp.eye(reps, dtype=F32), m))(small).astype(BF16)
            bc_s = jnp.tile(a_b_s[j][:, :ls], (1, reps))[:, :, None]
            common = (a_w_in[j].astype(BF16), row(a_b_in[j]), row(a_ln_g[j]), row(a_ln_b[j]))
            tail = (a_w_out[j].astype(BF16), row(a_b_out[j]), g1, b1)
            xp_in = x_prompt.reshape(tp, d) if x is None else x[:tp]
            xs_in = x_sample.reshape(ts, d) if x is None else x[tp:]
            x1 = _gmlp(xp_in, 0, t, *common, wc_p, bc_p, *tail)
            x1, v_s = _gmlp(xs_in, tp, t, *common, wc_s, bc_s, *tail, alias=x1)
            outs["v"].append(v_s.reshape(bs, ls, -1))
        elif kind == 1:
            dk, dv = b_w_g2.shape[-1], b_w_out.shape[1]
            w_in = b_w_in[j]
            w_main = w_in[:, :2 * dk + 2 * dv].astype(BF16)
            w_low = jnp.pad(w_in[:, 2 * dk + 2 * dv:], ((0, 0), (0, LANES - B_GATE_RANK))).astype(BF16)
            g2_hi, g2_lo = _hi_lo(jnp.pad(b_w_g2[j], ((0, LANES - B_GATE_RANK), (0, 0))))
            q, k, v, r, la = _gla_proj(x, w_main, w_low, g2_hi, g2_lo, row(b_b_g[j]), dk, dv)
            ng = row(b_norm_g[j])
            s0_p = jnp.zeros((bp,) + state_gla.shape[2:], F32)
            o, s_p = _linrec(q, k, la, v, r, ng, s0_p, n_batch=bp, seq=lp, row_off=0, heads=B_HEADS)
            o, s_s = _linrec(q, k, la, v, r, ng, state_gla[j], n_batch=bs, seq=ls, row_off=tp,
                             heads=B_HEADS, alias=o)
            outs["gla_p"].append(s_p)
            outs["gla_s"].append(s_s)
            x1 = _mm_ln(o, b_w_out[j].astype(BF16), zero_bias, x, g1, b1, "gla_out")
        elif kind == 2:
            heads = state_hgrn.shape[2]
            q, lf, k, v, gt = _hgrn_proj(x, c_w_in[j].astype(BF16), row(lb_all[i]))
            ng = row(c_norm_g[j])
            s0_p = jnp.zeros((bp,) + state_hgrn.shape[2:], F32)
            o, s_p = _linrec(q, k, lf, v, gt, ng, s0_p, n_batch=bp, seq=lp, row_off=0, heads=heads)
            o, s_s = _linrec(q, k, lf, v, gt, ng, state_hgrn[j], n_batch=bs, seq=ls, row_off=tp,
                             heads=heads, alias=o)
            outs["hgrn_p"].append(s_p)
            outs["hgrn_s"].append(s_s)
            x1 = _mm_ln(o, c_w_out[j].astype(BF16), zero_bias, x, g1, b1, "hgrn_out")
        else:
            h = _glu(x, d_w_in[j].astype(BF16), row(d_b_in[j]))
            cargs = (d_w_dw[j], row(d_b_dw[j]), row(d_ln_g[j]), row(d_ln_b[j]))
            conv0 = jnp.zeros((bp, D_BUF, d), F32)
            c, s_p = _conv(h, conv0, *cargs, n_batch=bp, seq=lp, row_off=0)
            c, s_s = _conv(h, state_conv[j], *cargs, n_batch=bs, seq=ls, row_off=tp, alias=c)
            outs["conv_p"].append(s_p)
            outs["conv_s"].append(s_s)
            x1 = _mm_ln(c, d_w_out[j].astype(BF16), row(d_b_out[j]), x, g1, b1, "conf_out")

        q = _mm_rows(x1, m_w_q[i].astype(BF16), "attn_q")
        att = _attn_prompt(q, mem_k, mem_v, i, bp, lp)
        att = _attn_sample(q, cache_k, cache_v, i, bs, ls, tp, att)
        w_route = jnp.concatenate([r_w_grp[i], r_w_exp[i]], axis=1)
        w_route = jnp.pad(w_route, ((0, 0), (0, LANES - w_route.shape[1])))
        b_route = jnp.pad(jnp.concatenate([r_b_grp[i], r_b_exp[i]]), (0, LANES - MOE_GROUPS - MOE_EXPERTS))
        x2, logits = _mm_ln(att, m_w_o[i].astype(BF16), zero_bias, x1, row(ln_g[i, 1]), row(ln_b[i, 1]),
                            "attn_out", router=_hi_lo(w_route))

        route, counts = _route(logits, row(b_route), ltri)
        src_tok, dst_row, blk_e, nused, n_slot_rows = _moe_plan(route, counts)
        slots = _experts(blk_e, nused, src_tok, dst_row, x2, e_w_gate_up, e_w_down, i, n_slot_rows)
        if i + 1 < DEPTH:
            x = _combine(x2, route, row(ln_g[i, 2]), row(ln_b[i, 2]), slots)
        else:
            y_p, y_s = _combine(x2, route, row(ln_g[i, 2]), row(ln_b[i, 2]), slots, split_rows=tp)

    y_prompt = y_p.reshape(bp, lp, d)
    y_sample = y_s.reshape(bs, ls, d)
    mem_k_p = _cache_unrows(mem_k_flat, MEM_HEADS)
    mem_v_p = _cache_unrows(mem_v_flat, MEM_HEADS)
    return (y_prompt, y_sample, mem_k_p, mem_v_p, jnp.stack(outs["gla_p"]), jnp.stack(outs["hgrn_p"]),
            jnp.stack(outs["conv_p"]), jnp.stack(outs["v"]), jnp.stack(outs["gla_s"]),
            jnp.stack(outs["hgrn_s"]), jnp.stack(outs["conv_s"]))
```

```python
import functools
import math

import jax
import jax.numpy as jnp
from jax import lax
from jax.experimental import pallas as pl
from jax.experimental.pallas import tpu as pltpu

F32 = jnp.float32
BF16 = jnp.bfloat16

D_MODEL = 1024
DEPTH = 4
N_MIXERS = 4
ALPHA = (2.0 * DEPTH) ** 0.25
LN_EPS = 1e-5
A_CHUNK = 128
A_GROUPS = 4
B_HEADS = 4
B_GATE_RANK = 16
B_GATE_TAU = 16.0
C_EXPAND = 128
D_CONV_W = 31
D_BUF = D_CONV_W - 1
LIN_CHUNK = 32
MEM_HEADS = 4
MOE_GROUPS = 4
MOE_PER_GROUP = 8
MOE_EXPERTS = MOE_GROUPS * MOE_PER_GROUP
MOE_HIDDEN = 512

LANES = 128
SUBLANES = 8
ROW_TILE = 512
MOE_BLOCK_ROWS = 256
ROUTE_LANE0 = MOE_GROUPS
VMEM_LIMIT = 56 * 1024 * 1024
INV_SQRT2 = 1.0 / math.sqrt(2.0)


def _params(*sem, vmem=VMEM_LIMIT):
    return pltpu.CompilerParams(dimension_semantics=sem, vmem_limit_bytes=vmem)


def _dot(a, b):
    return jnp.dot(a, b, preferred_element_type=F32)


def _dot_nt(a, b):
    return lax.dot_general(a, b, (((1,), (1,)), ((), ())), preferred_element_type=F32)


def _dot_hi(a, w_hi, w_lo):
    a_hi = a.astype(BF16)
    a_lo = (a - a_hi.astype(F32)).astype(BF16)
    return _dot(a_hi, w_hi) + _dot(a_lo, w_hi) + _dot(a_hi, w_lo)


def _layer_norm(x, g, b):
    mu = jnp.mean(x, axis=-1, keepdims=True)
    xc = x - mu
    var = jnp.mean(xc * xc, axis=-1, keepdims=True)
    return xc * lax.rsqrt(var + LN_EPS) * g + b


def _sigmoid(x):
    return 1.0 / (1.0 + jnp.exp(-x))


def _silu(x):
    return x * _sigmoid(x)


def _gelu(x):
    return 0.5 * x * (1.0 + lax.erf(x * INV_SQRT2))


def _log_sigmoid(x):
    return jnp.minimum(x, 0.0) - jnp.log(1.0 + jnp.exp(-jnp.abs(x)))


def _full(shape):
    return pl.BlockSpec(shape, lambda *_: (0,) * len(shape))


def _rows(tm, n, off=0):
    return pl.BlockSpec((tm, n), lambda i: (i + off, 0))


def _mm_kernel(a_ref, w_ref, o_ref):
    o_ref[...] = _dot(a_ref[...].astype(BF16), w_ref[...])


def _mem_proj_kernel(a_ref, w_ref, nat_ref, flat_ref, *, mem_len):
    res = _dot(a_ref[...].astype(BF16), w_ref[...])
    nat_ref[...] = res
    n_lt = res.shape[1] // (MEM_HEADS * LANES)
    for b in range(res.shape[0] // mem_len):
        for h in range(MEM_HEADS):
            for lt in range(n_lt):
                c0 = (h * n_lt + lt) * LANES
                flat_ref[b, pl.ds(lt * MEM_HEADS + h, mem_len, stride=MEM_HEADS * n_lt), :] = (
                    res[b * mem_len:(b + 1) * mem_len, c0:c0 + LANES])


def _mem_proj(a, w, mem_len):
    m, k = a.shape
    nl, _, n = w.shape
    tm = min(m, ROW_TILE)
    nbat = tm // mem_len
    return pl.pallas_call(
        functools.partial(_mem_proj_kernel, mem_len=mem_len),
        grid=(nl, m // tm),
        in_specs=[pl.BlockSpec((tm, k), lambda l, i: (i, 0)),
                  pl.BlockSpec((None, k, n), lambda l, i: (l, 0, 0))],
        out_specs=[pl.BlockSpec((None, tm, n), lambda l, i: (l, i, 0)),
                   pl.BlockSpec((None, nbat, mem_len * n // LANES, LANES), lambda l, i: (l, i, 0, 0))],
        out_shape=[jax.ShapeDtypeStruct((nl, m, n), F32),
                   jax.ShapeDtypeStruct((nl, m // mem_len, mem_len * n // LANES, LANES), F32)],
        compiler_params=_params("parallel", "parallel"),
        name="mem_kv_proj",
    )(a, w)


def _mm_rows(a, w, name, row_off, n_rows):
    k = a.shape[1]
    n = w.shape[1]
    tm = ROW_TILE
    return pl.pallas_call(
        _mm_kernel,
        grid=(n_rows // tm,),
        in_specs=[_rows(tm, k, row_off // tm), _full((k, n))],
        out_specs=_rows(tm, n),
        out_shape=jax.ShapeDtypeStruct((n_rows, n), F32),
        compiler_params=_params("parallel"),
        name=name,
    )(a, w)


def _mm_ln_kernel(a_ref, w_ref, bias_ref, x_ref, g_ref, b_ref, *rest, with_logits, n_alias):
    h = _dot(a_ref[...].astype(BF16), w_ref[...]) + bias_ref[...]
    y = _layer_norm(ALPHA * x_ref[...] + h, g_ref[...], b_ref[...])
    if with_logits:
        wr_hi_ref, wr_lo_ref = rest[:2]
        o_ref, lg_ref = rest[2 + n_alias:]
        lg_ref[...] = _dot_hi(y, wr_hi_ref[...], wr_lo_ref[...])
    else:
        (o_ref,) = rest[n_alias:]
    o_ref[...] = y


def _mm_ln(a, w, bias, x, g, b, name, router=None, row_off=0, fill=None):
    m, k = a.shape
    t, d = x.shape
    tm = ROW_TILE
    off = row_off // tm
    in_specs = [_rows(tm, k), _full((k, d)), _full((1, d)), _rows(tm, d, off), _full((1, d)), _full((1, d))]
    args = [a, w, bias, x, g, b]
    out_specs = [_rows(tm, d, off)]
    out_shape = [jax.ShapeDtypeStruct((t, d), F32)]
    if router is not None:
        in_specs += [_full((d, LANES)), _full((d, LANES))]
        args += list(router)
        out_specs.append(_rows(tm, LANES, off))
        out_shape.append(jax.ShapeDtypeStruct((t, LANES), F32))
    aliases = {}
    for n, arr in enumerate(fill or ()):
        in_specs.append(pl.BlockSpec(memory_space=pl.ANY))
        args.append(arr)
        aliases[len(args) - 1] = n
    res = pl.pallas_call(
        functools.partial(_mm_ln_kernel, with_logits=router is not None, n_alias=len(aliases)),
        grid=(m // tm,),
        in_specs=in_specs,
        out_specs=out_specs,
        out_shape=out_shape,
        input_output_aliases=aliases,
        compiler_params=_params("parallel"),
        name=name,
    )(*args)
    return res if router is not None else res[0]


def _gmlp_kernel(x_ref, w_in_ref, b_in_ref, lng_ref, lnb_ref, wc_ref, bc_ref, w_out_ref, b_out_ref,
                 g_ref, b_ref, *rest, emit_v, n_chunks):
    if emit_v:
        _alias_ref, o_ref, v_ref, vn_ref = rest
    else:
        o_ref, vn_ref = rest
    half = w_out_ref.shape[0]
    gw = half // A_GROUPS
    x = x_ref[...]
    xb = x.astype(BF16)
    v = _gelu(_dot(xb, w_in_ref[:, half:]) + b_in_ref[:, half:])
    vn = _layer_norm(v, lng_ref[...], lnb_ref[...])
    vn_ref[...] = vn
    if emit_v:
        v_ref[...] = vn
    acc = jnp.zeros(x.shape, F32)
    for grp in range(A_GROUPS):
        cols = slice(grp * gw, (grp + 1) * gw)
        u = _gelu(_dot(xb, w_in_ref[:, cols]) + b_in_ref[:, cols])
        mixed = []
        for c in range(n_chunks):
            vc = vn_ref[c * A_CHUNK:(c + 1) * A_CHUNK, cols].astype(BF16)
            mixed.append(_dot(wc_ref[grp], vc) + bc_ref[grp])
        mixed = mixed[0] if n_chunks == 1 else jnp.concatenate(mixed, axis=0)
        acc = acc + _dot((u * mixed).astype(BF16), w_out_ref[cols, :])
    h = acc + b_out_ref[...]
    o_ref[...] = _layer_norm(ALPHA * x + h, g_ref[...], b_ref[...])


def _gmlp(x, row_off, t, w_in, b_in, lng, lnb, wc, bc, w_out, b_out, g, b, alias=None):
    n_rows, d = x.shape
    ffn = w_in.shape[1]
    half = ffn // 2
    tm = 2 * A_CHUNK
    emit_v = alias is not None
    off = row_off // tm
    in_specs = [_rows(tm, d), _full((d, ffn)), _full((1, ffn)), _full((1, half)), _full((1, half)),
                _full((A_GROUPS, A_CHUNK, A_CHUNK)), _full((A_GROUPS, A_CHUNK, 1)), _full((half, d)),
                _full((1, d)), _full((1, d)), _full((1, d))]
    args = [x, w_in, b_in, lng, lnb, wc, bc, w_out, b_out, g, b]
    out_specs = _rows(tm, d, off)
    out_shape = jax.ShapeDtypeStruct((t, d), F32)
    aliases = {}
    if emit_v:
        in_specs.append(pl.BlockSpec(memory_space=pl.ANY))
        args.append(alias)
        aliases = {len(args) - 1: 0}
        out_specs = [out_specs, _rows(tm, half)]
        out_shape = [out_shape, jax.ShapeDtypeStruct((n_rows, half), F32)]
    return pl.pallas_call(
        functools.partial(_gmlp_kernel, emit_v=emit_v, n_chunks=tm // A_CHUNK),
        grid=(n_rows // tm,),
        in_specs=in_specs,
        out_specs=out_specs,
        out_shape=out_shape,
        scratch_shapes=[pltpu.VMEM((tm, half), F32)],
        input_output_aliases=aliases,
        compiler_params=_params("parallel"),
        name="gmlp_sample" if emit_v else "gmlp_prompt",
    )(*args)


def _gla_proj_kernel(x_ref, w_ref, wlow_ref, g2_hi_ref, g2_lo_ref, bg_ref,
                     q_ref, la_ref, k_ref, v_ref, r_ref, *, dk, dv, q_scale):
    xb = x_ref[...].astype(BF16)
    q_ref[...] = _dot(xb, w_ref[:, 0:dk]) * q_scale
    k_ref[...] = _dot(xb, w_ref[:, dk:2 * dk])
    v_ref[...] = _dot(xb, w_ref[:, 2 * dk:2 * dk + dv])
    r_ref[...] = _dot(xb, w_ref[:, 2 * dk + dv:2 * dk + 2 * dv])
    g_low = _dot(xb, wlow_ref[...])
    pre = _dot_hi(g_low, g2_hi_ref[...], g2_lo_ref[...]) + bg_ref[...]
    la_ref[...] = _log_sigmoid(pre) * (1.0 / B_GATE_TAU)


def _gla_proj(x, weights, dk, dv, row_off, n_rows):
    d = x.shape[1]
    tm = ROW_TILE
    shapes = [dk, dk, dk, dv, dv]
    return pl.pallas_call(
        functools.partial(_gla_proj_kernel, dk=dk, dv=dv, q_scale=(dk // B_HEADS) ** -0.5),
        grid=(n_rows // tm,),
        in_specs=[_rows(tm, d, row_off // tm)] + [_full(w.shape) for w in weights],
        out_specs=[_rows(tm, n) for n in shapes],
        out_shape=[jax.ShapeDtypeStruct((n_rows, n), F32) for n in shapes],
        compiler_params=_params("parallel"),
        name="gla_proj",
    )(x, *weights)


def _hgrn_proj_kernel(x_ref, w_ref, lb_ref, q_ref, lf_ref, k_ref, v_ref, gt_ref, *, d, q_scale):
    xb = x_ref[...].astype(BF16)
    lb = lb_ref[...]
    q_ref[...] = _silu(_dot(xb, w_ref[:, 0:d])) * q_scale
    f = _dot(xb, w_ref[:, d:2 * d])
    lf_ref[...] = jnp.log(lb + (1.0 - lb) * _sigmoid(f))
    k_ref[...] = (1.0 - lb) * _sigmoid(-f)
    v_ref[...] = _dot(xb, w_ref[:, 2 * d:3 * d])
    gt_ref[...] = _dot(xb, w_ref[:, 3 * d:4 * d])


def _hgrn_proj(x, weights, row_off, n_rows):
    d = x.shape[1]
    tm = ROW_TILE
    return pl.pallas_call(
        functools.partial(_hgrn_proj_kernel, d=d, q_scale=C_EXPAND ** -0.5),
        grid=(n_rows // tm,),
        in_specs=[_rows(tm, d, row_off // tm)] + [_full(w.shape) for w in weights],
        out_specs=[_rows(tm, d)] * 5,
        out_shape=[jax.ShapeDtypeStruct((n_rows, d), F32)] * 5,
        compiler_params=_params("parallel"),
        name="hgrn_proj",
    )(x, *weights)


def _cumsum_rows(x, chunk):
    pos = lax.broadcasted_iota(jnp.int32, x.shape, 0) & (chunk - 1)
    step = 1
    while step < chunk:
        x = x + jnp.where(pos >= step, pltpu.roll(x, step, axis=0), 0.0)
        step *= 2
    return x


def _chunk_rows(x, row, chunk, n_chunks):
    parts = [jnp.broadcast_to(x[c * chunk + row:c * chunk + row + 1, :], (chunk, x.shape[1]))
             for c in range(n_chunks)]
    return parts[0] if n_chunks == 1 else jnp.concatenate(parts, axis=0)


def _linrec_mask(tl, chunk):
    r_id = lax.broadcasted_iota(jnp.int32, (tl, tl), 0)
    c_id = lax.broadcasted_iota(jnp.int32, (tl, tl), 1)
    return (r_id >= c_id) & ((r_id // chunk) == (c_id // chunk))


def _linrec_heads(q_ref, k_ref, g_ref, v_ref, gate_ref, ng_ref, o_ref, st_ref, rs, *, heads, dk, dv, chunk, n_chunks):
    mask = _linrec_mask(chunk * n_chunks, chunk)
    for h in range(heads):
        ks = slice(h * dk, (h + 1) * dk)
        vs = slice(h * dv, (h + 1) * dv)
        bcum = _cumsum_rows(g_ref[rs, ks], chunk)
        b_mid = _chunk_rows(bcum, chunk // 2 - 1, chunk, n_chunks)
        b_end = _chunk_rows(bcum, chunk - 1, chunk, n_chunks)
        q = q_ref[rs, ks]
        k = k_ref[rs, ks]
        vb = v_ref[rs, vs].astype(BF16)
        q_in = (q * jnp.exp(bcum)).astype(BF16)
        q_a = (q * jnp.exp(bcum - b_mid)).astype(BF16)
        k_a = (k * jnp.exp(b_mid - bcum)).astype(BF16)
        k_end = k * jnp.exp(b_end - bcum)
        a = jnp.where(mask, _dot_nt(q_a, k_a), 0.0).astype(BF16)
        o_intra = _dot(a, vb)
        st = st_ref[h]
        outs = []
        for c in range(n_chunks):
            cr = slice(c * chunk, (c + 1) * chunk)
            outs.append(o_intra[cr] + _dot(q_in[cr], st.astype(BF16)))
            decay = jnp.exp(bcum[cr].T[:, chunk - 1:chunk])
            st = st * decay + _dot(k_end[cr].T.astype(BF16), vb[cr])
        st_ref[h] = st
        o = outs[0] if n_chunks == 1 else jnp.concatenate(outs, axis=0)
        o = o * lax.rsqrt(jnp.mean(o * o, axis=-1, keepdims=True) + LN_EPS) * ng_ref[:, vs]
        o_ref[rs, vs] = o * _silu(gate_ref[rs, vs])


def _linrec_kernel(q_ref, k_ref, g_ref, v_ref, gate_ref, ng_ref, s0_ref, o_ref, sout_ref, st_ref, *, nb, tl, **dims):
    li = pl.program_id(1)
    for n in range(nb):
        @pl.when(li == 0)
        def _():
            st_ref[...] = s0_ref[n]

        _linrec_heads(q_ref, k_ref, g_ref, v_ref, gate_ref, ng_ref, o_ref, st_ref, slice(n * tl, (n + 1) * tl),
                      **dims)
        sout_ref[n] = st_ref[...]


LINREC_CHUNK = 2 * LIN_CHUNK
LINREC_SAMPLE_NB = 4


def _linrec_tiling(seq):
    chunk = LINREC_CHUNK if seq % LINREC_CHUNK == 0 else seq
    tl = min(seq, 4 * chunk)
    return chunk, tl, seq // tl


def _linrec(q, k, g, v, gate, ng, s0, *, n_batch, seq, heads):
    t, hk = q.shape
    hv = v.shape[1]
    dk, dv = hk // heads, hv // heads
    chunk, tl, n_l = _linrec_tiling(seq)
    nb = LINREC_SAMPLE_NB if n_l == 1 else 1

    def rows(n):
        return pl.BlockSpec((nb * tl, n), lambda b, l: (b * n_l + l, 0))

    state_spec = pl.BlockSpec((nb, heads, dk, dv), lambda b, l: (b, 0, 0, 0))
    return pl.pallas_call(
        functools.partial(_linrec_kernel, nb=nb, tl=tl, heads=heads, dk=dk, dv=dv, chunk=chunk,
                          n_chunks=tl // chunk),
        grid=(n_batch // nb, n_l),
        in_specs=[rows(hk), rows(hk), rows(hk), rows(hv), rows(hv), pl.BlockSpec((1, hv), lambda b, l: (0, 0)),
                  state_spec],
        out_specs=[rows(hv), state_spec],
        out_shape=[jax.ShapeDtypeStruct((t, hv), F32), jax.ShapeDtypeStruct((n_batch, heads, dk, dv), F32)],
        scratch_shapes=[pltpu.VMEM((heads, dk, dv), F32)],
        compiler_params=_params("parallel", "arbitrary"),
        name="linrec_h%d" % heads,
    )(q, k, g, v, gate, ng, s0)


def _linrec_block_kernel(*refs, proj, n_proj_w, **dims):
    x_ref = refs[0]
    proj_w = refs[1:1 + n_proj_w]
    ng_ref, s0_ref, wo_ref, g_ref, b_ref, o_ref, sout_ref = refs[1 + n_proj_w:8 + n_proj_w]
    q_s, g_s, k_s, v_s, gate_s, o_s, st_ref = refs[8 + n_proj_w:]

    @pl.when(pl.program_id(1) == 0)
    def _():
        st_ref[...] = s0_ref[0]

    proj(x_ref, *proj_w, q_s, g_s, k_s, v_s, gate_s)
    _linrec_heads(q_s, k_s, g_s, v_s, gate_s, ng_ref, o_s, st_ref, slice(0, x_ref.shape[0]), **dims)
    h = _dot(o_s[...].astype(BF16), wo_ref[...])
    o_ref[...] = _layer_norm(ALPHA * x_ref[...] + h, g_ref[...], b_ref[...])
    sout_ref[0] = st_ref[...]


def _linrec_block(x, proj, proj_w, ng, s0, w_out, g, b, *, n_batch, seq, heads, hk, hv, name):
    t, d = x.shape
    dk, dv = hk // heads, hv // heads
    chunk, tl, n_l = _linrec_tiling(seq)
    rows = pl.BlockSpec((tl, d), lambda bi, l: (bi * n_l + l, 0))
    full = lambda shape: pl.BlockSpec(shape, lambda bi, l: (0,) * len(shape))
    state_spec = pl.BlockSpec((1, heads, dk, dv), lambda bi, l: (bi, 0, 0, 0))
    return pl.pallas_call(
        functools.partial(_linrec_block_kernel, proj=proj, n_proj_w=len(proj_w), heads=heads, dk=dk, dv=dv,
                          chunk=chunk, n_chunks=tl // chunk),
        grid=(n_batch, n_l),
        in_specs=[rows] + [full(w.shape) for w in proj_w] + [full((1, hv)), state_spec, full(w_out.shape),
                                                            full((1, d)), full((1, d))],
        out_specs=[rows, state_spec],
        out_shape=[jax.ShapeDtypeStruct((t, d), F32), jax.ShapeDtypeStruct((n_batch, heads, dk, dv), F32)],
        scratch_shapes=[pltpu.VMEM((tl, hk), F32), pltpu.VMEM((tl, hk), F32), pltpu.VMEM((tl, hk), F32),
                        pltpu.VMEM((tl, hv), F32), pltpu.VMEM((tl, hv), F32), pltpu.VMEM((tl, hv), F32),
                        pltpu.VMEM((heads, dk, dv), F32)],
        compiler_params=_params("parallel", "arbitrary"),
        name=name,
    )(x, *proj_w, ng, s0, w_out, g, b)


def _glu_kernel(x_ref, w_ref, b_ref, o_ref, *, d):
    xb = x_ref[...].astype(BF16)
    a = _dot(xb, w_ref[:, 0:d]) + b_ref[:, 0:d]
    gate = _dot(xb, w_ref[:, d:2 * d]) + b_ref[:, d:2 * d]
    o_ref[...] = a * _sigmoid(gate)


def _glu(x, w, b):
    t, d = x.shape
    tm = ROW_TILE
    return pl.pallas_call(
        functools.partial(_glu_kernel, d=d),
        grid=(t // tm,),
        in_specs=[_rows(tm, d), _full(w.shape), _full(b.shape)],
        out_specs=_rows(tm, d),
        out_shape=jax.ShapeDtypeStruct((t, d), F32),
        compiler_params=_params("parallel"),
        name="conf_glu",
    )(x, w, b)


CONV_PAD = 32


def _conv_kernel(h_ref, st_ref, wdw_ref, bdw_ref, lg_ref, lb_ref, *rest, nb, tl, aliased):
    if aliased:
        _alias_ref, o_ref, sout_ref, buf_ref, conv_ref = rest
    else:
        o_ref, sout_ref, buf_ref, conv_ref = rest
    li = pl.program_id(1)
    d = h_ref.shape[-1]
    lead = CONV_PAD - D_BUF
    rb = min(tl, 64)
    cw = LANES
    buf_ref[CONV_PAD + tl:CONV_PAD + tl + SUBLANES, :] = jnp.zeros((SUBLANES, d), F32)
    for n in range(nb):
        @pl.when(li == 0)
        def _():
            buf_ref[lead:CONV_PAD, :] = st_ref[n]

        buf_ref[CONV_PAD:CONV_PAD + tl, :] = h_ref[n * tl:(n + 1) * tl, :]
        for r0 in range(0, tl, rb):
            for c0 in range(0, d, cw):
                cols = slice(c0, c0 + cw)
                acc = jnp.zeros((rb, cw), F32)
                for s in range(SUBLANES):
                    part = None
                    for a in range((CONV_PAD + SUBLANES) // SUBLANES):
                        j = SUBLANES * a + s - lead
                        if 0 <= j < D_CONV_W:
                            rows = slice(r0 + SUBLANES * a, r0 + SUBLANES * a + rb + SUBLANES)
                            term = wdw_ref[j:j + 1, cols] * buf_ref[rows, cols]
                            part = term if part is None else part + term
                    acc = acc + part[s:s + rb, :]
                conv_ref[r0:r0 + rb, cols] = acc + bdw_ref[:, cols]
        y = _layer_norm(conv_ref[...], lg_ref[...], lb_ref[...])
        o_ref[n * tl:(n + 1) * tl, :] = _silu(y)
        sout_ref[n] = buf_ref[tl + lead:tl + CONV_PAD, :]
        buf_ref[0:CONV_PAD, :] = buf_ref[tl:tl + CONV_PAD, :]


def _conv(h, state, wdw, bdw, lg, lb, *, n_batch, seq, row_off, alias=None):
    t, d = h.shape
    tl = min(seq, 256)
    n_l = seq // tl
    nb = 8 if n_l == 1 else 1
    off = row_off // (nb * tl)
    rows = pl.BlockSpec((nb * tl, d), lambda b, l: (off + b * n_l + l, 0))
    state_spec = pl.BlockSpec((nb, D_BUF, d), lambda b, l: (b, 0, 0))
    in_specs = [rows, state_spec, _full(wdw.shape), _full((1, d)), _full((1, d)), _full((1, d))]
    args = [h, state, wdw, bdw, lg, lb]
    aliases = {}
    if alias is not None:
        in_specs.append(pl.BlockSpec(memory_space=pl.ANY))
        args.append(alias)
        aliases = {len(args) - 1: 0}
    return pl.pallas_call(
        functools.partial(_conv_kernel, nb=nb, tl=tl, aliased=alias is not None),
        grid=(n_batch // nb, n_l),
        in_specs=in_specs,
        out_specs=[rows, state_spec],
        out_shape=[jax.ShapeDtypeStruct((t, d), F32), jax.ShapeDtypeStruct((n_batch, D_BUF, d), F32)],
        scratch_shapes=[pltpu.VMEM((CONV_PAD + tl + SUBLANES, d), F32), pltpu.VMEM((tl, d), F32)],
        input_output_aliases=aliases,
        compiler_params=_params("parallel", "arbitrary"),
        name="conf_conv_sample" if alias is not None else "conf_conv_prompt",
    )(*args)


def _attn_heads(q, k_of, v_of, hd):
    outs = []
    for h in range(MEM_HEADS):
        hs = slice(h * hd, (h + 1) * hd)
        s = _dot_nt(q[:, hs].astype(BF16), k_of(hs).astype(BF16)) * (hd ** -0.5)
        p = jnp.exp(s - jnp.max(s, axis=-1, keepdims=True))
        denom = jnp.sum(p, axis=-1, keepdims=True)
        outs.append(_dot(p.astype(BF16), v_of(hs).astype(BF16)) / denom)
    return outs


def _attn_block_kernel(x_ref, wq_ref, k_ref, v_ref, wo_ref, g_ref, b_ref, wr_hi_ref, wr_lo_ref, o_ref, lg_ref):
    x = x_ref[...]
    hd = x.shape[1] // MEM_HEADS
    q = _dot(x.astype(BF16), wq_ref[...])
    att = jnp.concatenate(_attn_heads(q, lambda hs: k_ref[:, hs], lambda hs: v_ref[:, hs], hd), axis=1)
    y = _layer_norm(ALPHA * x + _dot(att.astype(BF16), wo_ref[...]), g_ref[...], b_ref[...])
    lg_ref[...] = _dot_hi(y, wr_hi_ref[...], wr_lo_ref[...])
    o_ref[...] = y


def _attn_block_prompt(x, w_q, mem_k, mem_v, w_o, g, b, router, layer, n_batch, seq):
    t, d = x.shape
    m = mem_k.shape[1] // n_batch
    tl = ROW_TILE
    n_l = seq // tl
    rows = lambda n: pl.BlockSpec((tl, n), lambda bi, l: (bi * n_l + l, 0))
    kv = pl.BlockSpec((None, m, d), lambda bi, l: (layer, bi, 0))
    full = lambda shape: pl.BlockSpec(shape, lambda bi, l: (0,) * len(shape))
    return pl.pallas_call(
        _attn_block_kernel,
        grid=(n_batch, n_l),
        in_specs=[rows(d), full((d, d)), kv, kv, full((d, d)), full((1, d)), full((1, d)),
                  full((d, LANES)), full((d, LANES))],
        out_specs=[rows(d), rows(LANES)],
        out_shape=[jax.ShapeDtypeStruct((t, d), F32), jax.ShapeDtypeStruct((t, LANES), F32)],
        compiler_params=_params("parallel", "parallel"),
        name="attn_block_prompt",
    )(x, w_q, mem_k, mem_v, w_o, g, b, *router)


ATTN_SAMPLE_NB = 4


def _attn_sample_kernel(q_ref, k_ref, v_ref, o_ref, *, nb, seq):
    hd = q_ref.shape[-1] // MEM_HEADS
    n_lt = hd // LANES
    m = k_ref.shape[1] // (MEM_HEADS * n_lt)

    def head(ref, n, h):
        tiles = [ref[n, pl.ds(lt * MEM_HEADS + h, m, stride=MEM_HEADS * n_lt), :] for lt in range(n_lt)]
        return jnp.concatenate(tiles, axis=1).astype(BF16)

    pairs = [(n, h) for n in range(nb) for h in range(MEM_HEADS)]
    s = jnp.concatenate(
        [_dot_nt(q_ref[n * seq:(n + 1) * seq, h * hd:(h + 1) * hd].astype(BF16), head(k_ref, n, h))
         for n, h in pairs], axis=0) * (hd ** -0.5)
    p = jnp.exp(s - jnp.max(s, axis=-1, keepdims=True))
    inv = 1.0 / jnp.sum(p, axis=-1, keepdims=True)
    for idx, (n, h) in enumerate(pairs):
        rs = slice(idx * seq, (idx + 1) * seq)
        o = _dot(p[rs].astype(BF16), head(v_ref, n, h)) * inv[rs]
        o_ref[n * seq:(n + 1) * seq, h * hd:(h + 1) * hd] = o


def _cache_rows(cache):
    nl, nbat, m, heads, hd = cache.shape
    c = cache.reshape(nl, nbat, m, heads, hd // LANES, LANES)
    return c.transpose(0, 1, 2, 4, 3, 5).reshape(nl, nbat, m * (hd // LANES) * heads, LANES)


def _cache_unrows(flat, heads):
    nl, nbat, rows, _ = flat.shape
    n_lt = D_MODEL // (heads * LANES)
    m = rows // (heads * n_lt)
    c = flat.reshape(nl, nbat, m, n_lt, heads, LANES).transpose(0, 1, 2, 4, 3, 5)
    return c.reshape(nl, nbat, m, heads, n_lt * LANES)


def _attn_sample(q, cache_k, cache_v, layer, n_batch, seq):
    t, d = q.shape
    nb = ATTN_SAMPLE_NB
    rows = pl.BlockSpec((nb * seq, d), lambda i: (i, 0))
    kv = pl.BlockSpec((None, nb) + cache_k.shape[2:], lambda i: (layer, i, 0, 0))
    return pl.pallas_call(
        functools.partial(_attn_sample_kernel, nb=nb, seq=seq),
        grid=(n_batch // nb,),
        in_specs=[rows, kv, kv],
        out_specs=rows,
        out_shape=jax.ShapeDtypeStruct((t, d), F32),
        compiler_params=_params("parallel"),
        name="attn_sample",
    )(q, cache_k, cache_v)


def _route_kernel(lg_ref, bias_ref, ltri_ref, route_ref, counts_ref, carry_ref):
    i = pl.program_id(0)

    @pl.when(i == 0)
    def _():
        carry_ref[...] = jnp.zeros(carry_ref.shape, F32)

    z = lg_ref[...] + bias_ref[...]
    lane = lax.broadcasted_iota(jnp.int32, z.shape, 1).astype(F32)
    neg = -jnp.inf
    far = float(LANES)

    def first_max(mask):
        vmax = jnp.max(jnp.where(mask, z, neg), axis=-1, keepdims=True)
        idx = jnp.min(jnp.where(mask & (z == vmax), lane, far), axis=-1, keepdims=True)
        return vmax, idx

    gmask = lane < float(MOE_GROUPS)
    gmax, gidx = first_max(gmask)
    gsum = jnp.sum(jnp.where(gmask, jnp.exp(z - gmax), 0.0), axis=-1, keepdims=True)
    g_w = 1.0 / gsum
    lo = float(ROUTE_LANE0) + float(MOE_PER_GROUP) * gidx
    emask = (lane >= lo) & (lane < lo + float(MOE_PER_GROUP))
    v1, i1 = first_max(emask)
    v2, i2 = first_max(emask & (lane != i1))
    tt = jnp.exp(v2 - v1)
    w0 = g_w / (1.0 + tt)
    w1 = g_w * tt / (1.0 + tt)
    sel1 = lane == i1
    sel2 = lane == i2
    onehot = jnp.where(sel1 | sel2, 1.0, 0.0)
    before = _dot(ltri_ref[...], onehot.astype(BF16)) + carry_ref[...]
    rank0 = jnp.sum(jnp.where(sel1, before, 0.0), axis=-1, keepdims=True)
    rank1 = jnp.sum(jnp.where(sel2, before, 0.0), axis=-1, keepdims=True)
    carry = carry_ref[...] + jnp.sum(onehot, axis=0, keepdims=True)
    carry_ref[...] = carry
    counts_ref[...] = carry
    e_off = float(ROUTE_LANE0)
    out = jnp.zeros(z.shape, F32)
    for ln, val in enumerate((i1 - e_off, i2 - e_off, w0, w1, rank0, rank1)):
        out = jnp.where(lane == float(ln), val, out)
    route_ref[...] = out


def _route(logits, bias, ltri):
    t = logits.shape[0]
    tm = ROW_TILE
    return pl.pallas_call(
        _route_kernel,
        grid=(t // tm,),
        in_specs=[_rows(tm, LANES), _full((1, LANES)), _full((tm, tm))],
        out_specs=[_rows(tm, LANES), _full((1, LANES))],
        out_shape=[jax.ShapeDtypeStruct((t, LANES), F32), jax.ShapeDtypeStruct((1, LANES), F32)],
        scratch_shapes=[pltpu.VMEM((1, LANES), F32)],
        compiler_params=_params("arbitrary"),
        name="moe_route",
    )(logits, bias, ltri)


def _row_copy_wait(src_rows, dst_rows, sem):
    pltpu.make_async_copy(src_rows, dst_rows, sem).wait()


def _dispatch_kernel(tail_ref, dest_ref, x_ref, xs_ref, zeros_ref, sem, *, tm):
    @pl.when(pl.program_id(0) == 0)
    def _():
        zeros_ref[...] = jnp.zeros(zeros_ref.shape, F32)
        bm = zeros_ref.shape[0]
        tails = [xs_ref.at[pl.ds(pl.multiple_of(tail_ref[e], bm), bm)] for e in range(MOE_EXPERTS)]
        for dst in tails:
            pltpu.make_async_copy(zeros_ref, dst, sem).start()
        for dst in tails:
            pltpu.make_async_copy(zeros_ref, dst, sem).wait()

    def issue(t, carry):
        for j in range(2):
            d = dest_ref[2 * t + j]
            pltpu.make_async_copy(x_ref.at[pl.ds(t, 1)], xs_ref.at[pl.ds(d, 1)], sem).start(priority=j)
        return carry

    lax.fori_loop(0, tm, issue, 0, unroll=8)
    for _ in range(2):
        _row_copy_wait(x_ref, xs_ref.at[pl.ds(0, tm)], sem)


def _dispatch(tail_rows, dest_flat, x, n_rows):
    t, d = x.shape
    tm = ROW_TILE
    grid_spec = pltpu.PrefetchScalarGridSpec(
        num_scalar_prefetch=1,
        grid=(t // tm,),
        in_specs=[pl.BlockSpec((2 * tm,), lambda i, tail: (i,), memory_space=pltpu.SMEM),
                  pl.BlockSpec((tm, d), lambda i, tail: (i, 0))],
        out_specs=pl.BlockSpec(memory_space=pl.ANY),
        scratch_shapes=[pltpu.VMEM((MOE_BLOCK_ROWS, d), F32), pltpu.SemaphoreType.DMA(())],
    )
    return pl.pallas_call(
        functools.partial(_dispatch_kernel, tm=tm),
        grid_spec=grid_spec,
        out_shape=jax.ShapeDtypeStruct((n_rows, d), F32),
        compiler_params=_params("arbitrary"),
        name="moe_dispatch",
    )(tail_rows, dest_flat, x)


def _expert_kernel(blk_e_ref, nused_ref, xs_ref, wgu_ref, wd_ref, y_ref, wgu_bf, wd_bf):
    i = pl.program_id(0)
    prev = blk_e_ref[jnp.maximum(i - 1, 0)]
    new_expert = (i == 0) | (blk_e_ref[i] != prev)

    @pl.when(new_expert)
    def _():
        wgu_bf[...] = wgu_ref[...].astype(BF16)
        wd_bf[...] = wd_ref[...].astype(BF16)

    @pl.when(i < nused_ref[0])
    def _():
        hid = wd_bf.shape[0]
        xb = xs_ref[...].astype(BF16)
        gate = _dot(xb, wgu_bf[:, 0:hid])
        up = _dot(xb, wgu_bf[:, hid:2 * hid])
        y_ref[...] = _dot((_silu(gate) * up).astype(BF16), wd_bf[...])

    @pl.when(i >= nused_ref[0])
    def _():
        y_ref[...] = jnp.zeros(y_ref.shape, F32)


def _experts(blk_e, nused, xs, w_gate_up, w_down, layer):
    nr, d = xs.shape
    bm = MOE_BLOCK_ROWS
    hid2 = w_gate_up.shape[-1]
    hid = w_down.shape[-2]
    grid_spec = pltpu.PrefetchScalarGridSpec(
        num_scalar_prefetch=2,
        grid=(nr // bm,),
        in_specs=[pl.BlockSpec((bm, d), lambda i, be, nu: (jnp.minimum(i, nu[0] - 1), 0)),
                  pl.BlockSpec((None, None, d, hid2), lambda i, be, nu: (layer, be[i], 0, 0)),
                  pl.BlockSpec((None, None, hid, d), lambda i, be, nu: (layer, be[i], 0, 0))],
        out_specs=pl.BlockSpec((bm, d), lambda i, be, nu: (i, 0)),
        scratch_shapes=[pltpu.VMEM((d, hid2), BF16), pltpu.VMEM((hid, d), BF16)],
    )
    return pl.pallas_call(
        _expert_kernel,
        grid_spec=grid_spec,
        out_shape=jax.ShapeDtypeStruct((nr, d), F32),
        compiler_params=_params("arbitrary"),
        name="moe_experts",
    )(blk_e, nused, xs, w_gate_up, w_down)


def _combine_kernel(dest_ref, dest_next_ref, x_ref, route_ref, g_ref, b_ref, y_hbm, *rest, tm, n_first):
    if n_first is None:
        o_ref, ybuf, sems = rest
    else:
        o_ref, o2_ref, ybuf, sems = rest
    i = pl.program_id(0)
    n_tiles = pl.num_programs(0)

    def gather(d_ref, slot):
        def issue(t, carry):
            for j in range(2):
                d = d_ref[2 * t + j]
                pltpu.make_async_copy(y_hbm.at[pl.ds(d, 1)], ybuf.at[slot, j, pl.ds(t, 1)],
                                      sems.at[slot]).start(priority=j)
            return carry

        lax.fori_loop(0, tm, issue, 0, unroll=8)

    @pl.when(i == 0)
    def _():
        gather(dest_ref, 0)

    @pl.when(i + 1 < n_tiles)
    def _():
        gather(dest_next_ref, lax.rem(i + 1, 2))

    slot = lax.rem(i, 2)
    for j in range(2):
        _row_copy_wait(y_hbm.at[pl.ds(0, tm)], ybuf.at[slot, j], sems.at[slot])
    moe = route_ref[:, 2:3] * ybuf[slot, 0] + route_ref[:, 3:4] * ybuf[slot, 1]
    y = _layer_norm(ALPHA * x_ref[...] + moe, g_ref[...], b_ref[...])
    if n_first is None:
        o_ref[...] = y
    else:
        @pl.when(i < n_first)
        def _():
            o_ref[...] = y

        @pl.when(i >= n_first)
        def _():
            o2_ref[...] = y


def _combine(dest_flat, x, route, g, b, yb, split_rows=None):
    t, d = x.shape
    tm = ROW_TILE
    n_tiles = t // tm
    out_specs = _rows(tm, d)
    out_shape = jax.ShapeDtypeStruct((t, d), F32)
    n_first = None
    if split_rows is not None:
        n_first = split_rows // tm
        out_specs = [pl.BlockSpec((tm, d), lambda i: (jnp.minimum(i, n_first - 1), 0)),
                     pl.BlockSpec((tm, d), lambda i: (jnp.maximum(i - n_first, 0), 0))]
        out_shape = [jax.ShapeDtypeStruct((split_rows, d), F32), jax.ShapeDtypeStruct((t - split_rows, d), F32)]
    return pl.pallas_call(
        functools.partial(_combine_kernel, tm=tm, n_first=n_first),
        grid=(n_tiles,),
        in_specs=[pl.BlockSpec((2 * tm,), lambda i: (i,), memory_space=pltpu.SMEM),
                  pl.BlockSpec((2 * tm,), lambda i: (jnp.minimum(i + 1, n_tiles - 1),), memory_space=pltpu.SMEM),
                  _rows(tm, d), _rows(tm, LANES), _full((1, d)), _full((1, d)),
                  pl.BlockSpec(memory_space=pl.ANY)],
        out_specs=out_specs,
        out_shape=out_shape,
        scratch_shapes=[pltpu.VMEM((2, 2, tm, d), F32), pltpu.SemaphoreType.DMA((2,))],
        compiler_params=_params("arbitrary"),
        name="moe_combine",
    )(dest_flat, dest_flat, x, route, g, b, yb)


def _moe_plan(route, counts):
    bm = MOE_BLOCK_ROWS
    t = route.shape[0]
    n_blocks = -(-2 * t // bm) + MOE_EXPERTS
    e_idx = route[:, 0:2].astype(jnp.int32)
    rank = route[:, 4:6].astype(jnp.int32)
    cnt = counts[0, ROUTE_LANE0:ROUTE_LANE0 + MOE_EXPERTS].astype(jnp.int32)
    padded = ((cnt + bm - 1) // bm) * bm
    pad_end = jnp.cumsum(padded)
    pad_start = pad_end - padded
    experts = jnp.arange(MOE_EXPERTS, dtype=jnp.int32)
    dest = (rank + jnp.sum(jnp.where(e_idx[..., None] == experts, pad_start, 0), axis=-1)).reshape(-1)
    nused = pad_end[-1] // bm
    blk = jnp.arange(n_blocks, dtype=jnp.int32)
    blk_e = jnp.sum((pad_end[None, :] <= (blk * bm)[:, None]).astype(jnp.int32), axis=1)
    blk_e = jnp.minimum(blk_e, MOE_EXPERTS - 1)
    last_e = jnp.max(jnp.where(cnt > 0, experts, 0))
    blk_e = jnp.where(blk < nused, blk_e, last_e)
    empty = cnt == 0
    tail_rows = jnp.where(empty, (nused + jnp.cumsum(empty.astype(jnp.int32)) - 1) * bm, pad_end - bm)
    return (dest.astype(jnp.int32), blk_e.astype(jnp.int32), nused.reshape(1).astype(jnp.int32),
            tail_rows.astype(jnp.int32), n_blocks)


def _hi_lo(w):
    hi = w.astype(BF16)
    return hi, (w - hi.astype(F32)).astype(BF16)


def kernel(x_prompt, x_sample, mem_prompt, cache_mem_k, cache_mem_v, state_gla, state_hgrn, state_conv,
           ln_g, ln_b, a_w_in, a_b_in, a_ln_g, a_ln_b, a_w_s, a_b_s, a_w_out, a_b_out,
           b_w_in, b_w_g2, b_b_g, b_norm_g, b_w_out, c_lb, c_w_in, c_norm_g, c_w_out,
           d_w_in, d_b_in, d_w_dw, d_b_dw, d_ln_g, d_ln_b, d_w_out, d_b_out,
           m_w_q, m_w_k, m_w_v, m_w_o, r_w_grp, r_b_grp, r_w_exp, r_b_exp, e_w_gate_up, e_w_down):
    bp, lp, d = x_prompt.shape
    bs, ls, _ = x_sample.shape
    tp, ts = bp * lp, bs * ls
    t = tp + ts
    mem_len = mem_prompt.shape[1]

    row = lambda a: a.reshape(1, -1)
    x = None

    mem2d = mem_prompt.reshape(bp * mem_len, d)
    mem_k, mem_k_flat = _mem_proj(mem2d, m_w_k.astype(BF16), mem_len)
    mem_v, mem_v_flat = _mem_proj(mem2d, m_w_v.astype(BF16), mem_len)

    cache_k = _cache_rows(cache_mem_k)
    cache_v = _cache_rows(cache_mem_v)

    lb_all = jnp.cumsum(jax.nn.softmax(c_lb.astype(F32), axis=0), axis=0)
    lb_all = lb_all - lb_all[:1]
    ltri = jnp.tril(jnp.ones((ROW_TILE, ROW_TILE), F32), -1).astype(BF16)
    zero_bias = jnp.zeros((1, d), F32)

    outs = {"v": [], "gla_p": [], "gla_s": [], "hgrn_p": [], "hgrn_s": [], "conv_p": [], "conv_s": []}
    for i in range(DEPTH):
        j = i // N_MIXERS
        kind = i % N_MIXERS
        g1, b1 = row(ln_g[i, 0]), row(ln_b[i, 0])
        if kind == 0:
            tril = jnp.tril(jnp.ones((A_CHUNK, A_CHUNK), bool))
            wc_p = jnp.where(tril, a_w_s[j], 0.0).astype(BF16)
            bc_p = a_b_s[j][:, :, None]
            reps = A_CHUNK // ls
            small = jnp.where(jnp.tril(jnp.ones((ls, ls), bool)), a_w_s[j][:, :ls, :ls], 0.0)
            wc_s = jax.vmap(lambda m: jnp.kron(jnp.eye(reps, dtype=F32), m))(small).astype(BF16)
            bc_s = jnp.tile(a_b_s[j][:, :ls], (1, reps))[:, :, None]
            common = (a_w_in[j].astype(BF16), row(a_b_in[j]), row(a_ln_g[j]), row(a_ln_b[j]))
            tail = (a_w_out[j].astype(BF16), row(a_b_out[j]), g1, b1)
            xp_in = x_prompt.reshape(tp, d) if x is None else x[:tp]
            xs_in = x_sample.reshape(ts, d) if x is None else x[tp:]
            x1 = _gmlp(xp_in, 0, t, *common, wc_p, bc_p, *tail)
            x1, v_s = _gmlp(xs_in, tp, t, *common, wc_s, bc_s, *tail, alias=x1)
            outs["v"].append(v_s.reshape(bs, ls, -1))
        elif kind == 1:
            dk, dv = b_w_g2.shape[-1], b_w_out.shape[1]
            w_in = b_w_in[j]
            w_main = w_in[:, :2 * dk + 2 * dv].astype(BF16)
            w_low = jnp.pad(w_in[:, 2 * dk + 2 * dv:], ((0, 0), (0, LANES - B_GATE_RANK))).astype(BF16)
            g2_hi, g2_lo = _hi_lo(jnp.pad(b_w_g2[j], ((0, LANES - B_GATE_RANK), (0, 0))))
            proj_w = (w_main, w_low, g2_hi, g2_lo, row(b_b_g[j]))
            proj = functools.partial(_gla_proj_kernel, dk=dk, dv=dv, q_scale=(dk // B_HEADS) ** -0.5)
            ng = row(b_norm_g[j])
            w_out = b_w_out[j].astype(BF16)
            s0_p = jnp.zeros((bp,) + state_gla.shape[2:], F32)
            x1, s_p = _linrec_block(x, proj, proj_w, ng, s0_p, w_out, g1, b1, n_batch=bp, seq=lp, heads=B_HEADS,
                                    hk=dk, hv=dv, name="gla_block_prompt")
            q, la, k, v, r = _gla_proj(x, proj_w, dk, dv, tp, ts)
            o_s, s_s = _linrec(q, k, la, v, r, ng, state_gla[j], n_batch=bs, seq=ls, heads=B_HEADS)
            x1 = _mm_ln(o_s, w_out, zero_bias, x, g1, b1, "gla_out_sample", row_off=tp, fill=(x1,))
            outs["gla_p"].append(s_p)
            outs["gla_s"].append(s_s)
        elif kind == 2:
            heads = state_hgrn.shape[2]
            proj_w = (c_w_in[j].astype(BF16), row(lb_all[i]))
            proj = functools.partial(_hgrn_proj_kernel, d=d, q_scale=C_EXPAND ** -0.5)
            ng = row(c_norm_g[j])
            w_out = c_w_out[j].astype(BF16)
            s0_p = jnp.zeros((bp,) + state_hgrn.shape[2:], F32)
            x1, s_p = _linrec_block(x, proj, proj_w, ng, s0_p, w_out, g1, b1, n_batch=bp, seq=lp, heads=heads,
                                    hk=d, hv=d, name="hgrn_block_prompt")
            q, lf, k, v, gt = _hgrn_proj(x, proj_w, tp, ts)
            o_s, s_s = _linrec(q, k, lf, v, gt, ng, state_hgrn[j], n_batch=bs, seq=ls, heads=heads)
            x1 = _mm_ln(o_s, w_out, zero_bias, x, g1, b1, "hgrn_out_sample", row_off=tp, fill=(x1,))
            outs["hgrn_p"].append(s_p)
            outs["hgrn_s"].append(s_s)
        else:
            h = _glu(x, d_w_in[j].astype(BF16), row(d_b_in[j]))
            cargs = (d_w_dw[j], row(d_b_dw[j]), row(d_ln_g[j]), row(d_ln_b[j]))
            conv0 = jnp.zeros((bp, D_BUF, d), F32)
            c, s_p = _conv(h, conv0, *cargs, n_batch=bp, seq=lp, row_off=0)
            c, s_s = _conv(h, state_conv[j], *cargs, n_batch=bs, seq=ls, row_off=tp, alias=c)
            outs["conv_p"].append(s_p)
            outs["conv_s"].append(s_s)
            x1 = _mm_ln(c, d_w_out[j].astype(BF16), row(d_b_out[j]), x, g1, b1, "conf_out")

        w_route = jnp.concatenate([r_w_grp[i], r_w_exp[i]], axis=1)
        w_route = jnp.pad(w_route, ((0, 0), (0, LANES - w_route.shape[1])))
        b_route = jnp.pad(jnp.concatenate([r_b_grp[i], r_b_exp[i]]), (0, LANES - MOE_GROUPS - MOE_EXPERTS))
        router = _hi_lo(w_route)
        w_q, w_o = m_w_q[i].astype(BF16), m_w_o[i].astype(BF16)
        g2, b2 = row(ln_g[i, 1]), row(ln_b[i, 1])
        x2, logits = _attn_block_prompt(x1, w_q, mem_k, mem_v, w_o, g2, b2, router, i, bp, lp)
        q_s = _mm_rows(x1, w_q, "attn_q_sample", tp, ts)
        att_s = _attn_sample(q_s, cache_k, cache_v, i, bs, ls)
        x2, logits = _mm_ln(att_s, w_o, zero_bias, x1, g2, b2, "attn_out_sample", router=router, row_off=tp,
                            fill=(x2, logits))

        route, counts = _route(logits, row(b_route), ltri)
        dest, blk_e, nused, tail_rows, n_blocks = _moe_plan(route, counts)
        xs = _dispatch(tail_rows, dest, x2, n_blocks * MOE_BLOCK_ROWS)
        yb = _experts(blk_e, nused, xs, e_w_gate_up, e_w_down, i)
        if i + 1 < DEPTH:
            x = _combine(dest, x2, route, row(ln_g[i, 2]), row(ln_b[i, 2]), yb)
        else:
            y_p, y_s = _combine(dest, x2, route, row(ln_g[i, 2]), row(ln_b[i, 2]), yb, split_rows=tp)

    y_prompt = y_p.reshape(bp, lp, d)
    y_sample = y_s.reshape(bs, ls, d)
    mem_k_p = _cache_unrows(mem_k_flat, MEM_HEADS)
    mem_v_p = _cache_unrows(mem_v_flat, MEM_HEADS)
    return (y_prompt, y_sample, mem_k_p, mem_v_p, jnp.stack(outs["gla_p"]), jnp.stack(outs["hgrn_p"]),
            jnp.stack(outs["conv_p"]), jnp.stack(outs["v"]), jnp.stack(outs["gla_s"]),
            jnp.stack(outs["hgrn_s"]), jnp.stack(outs["conv_s"]))
```

```python
import functools
import math

import jax
import jax.numpy as jnp
from jax import lax
from jax.experimental import pallas as pl
from jax.experimental.pallas import tpu as pltpu

F32 = jnp.float32
BF16 = jnp.bfloat16

D_MODEL = 1024
DEPTH = 4
N_MIXERS = 4
ALPHA = (2.0 * DEPTH) ** 0.25
LN_EPS = 1e-5
A_CHUNK = 128
A_GROUPS = 4
B_HEADS = 4
B_GATE_RANK = 16
B_GATE_TAU = 16.0
C_EXPAND = 128
D_CONV_W = 31
D_BUF = D_CONV_W - 1
LIN_CHUNK = 32
MEM_HEADS = 4
MOE_GROUPS = 4
MOE_PER_GROUP = 8
MOE_EXPERTS = MOE_GROUPS * MOE_PER_GROUP
MOE_HIDDEN = 512

LANES = 128
SUBLANES = 8
ROW_TILE = 512
MOE_BLOCK_ROWS = 256
ROUTE_LANE0 = MOE_GROUPS
VMEM_LIMIT = 56 * 1024 * 1024
INV_SQRT2 = 1.0 / math.sqrt(2.0)


def _params(*sem, vmem=VMEM_LIMIT):
    return pltpu.CompilerParams(dimension_semantics=sem, vmem_limit_bytes=vmem)


def _dot(a, b):
    return jnp.dot(a, b, preferred_element_type=F32)


def _dot_nt(a, b):
    return lax.dot_general(a, b, (((1,), (1,)), ((), ())), preferred_element_type=F32)


def _dot_hi(a, w_hi, w_lo):
    a_hi = a.astype(BF16)
    a_lo = (a - a_hi.astype(F32)).astype(BF16)
    return _dot(a_hi, w_hi) + _dot(a_lo, w_hi) + _dot(a_hi, w_lo)


def _layer_norm(x, g, b):
    mu = jnp.mean(x, axis=-1, keepdims=True)
    xc = x - mu
    var = jnp.mean(xc * xc, axis=-1, keepdims=True)
    return xc * lax.rsqrt(var + LN_EPS) * g + b


def _sigmoid(x):
    return 1.0 / (1.0 + jnp.exp(-x))


def _silu(x):
    return x * _sigmoid(x)


def _gelu(x):
    return 0.5 * x * (1.0 + lax.erf(x * INV_SQRT2))


def _log_sigmoid(x):
    return jnp.minimum(x, 0.0) - jnp.log(1.0 + jnp.exp(-jnp.abs(x)))


def _full(shape):
    return pl.BlockSpec(shape, lambda *_: (0,) * len(shape))


def _rows(tm, n, off=0):
    return pl.BlockSpec((tm, n), lambda i: (i + off, 0))


def _mm_kernel(a_ref, w_ref, o_ref):
    o_ref[...] = _dot(a_ref[...].astype(BF16), w_ref[...])


def _mem_proj_kernel(a_ref, w_ref, nat_ref, flat_ref, *, mem_len):
    res = _dot(a_ref[...].astype(BF16), w_ref[...])
    nat_ref[...] = res
    n_lt = res.shape[1] // (MEM_HEADS * LANES)
    for b in range(res.shape[0] // mem_len):
        for h in range(MEM_HEADS):
            for lt in range(n_lt):
                c0 = (h * n_lt + lt) * LANES
                flat_ref[b, pl.ds(lt * MEM_HEADS + h, mem_len, stride=MEM_HEADS * n_lt), :] = (
                    res[b * mem_len:(b + 1) * mem_len, c0:c0 + LANES])


def _mem_proj(a, w, mem_len):
    m, k = a.shape
    nl, _, n = w.shape
    tm = min(m, ROW_TILE)
    nbat = tm // mem_len
    return pl.pallas_call(
        functools.partial(_mem_proj_kernel, mem_len=mem_len),
        grid=(nl, m // tm),
        in_specs=[pl.BlockSpec((tm, k), lambda l, i: (i, 0)),
                  pl.BlockSpec((None, k, n), lambda l, i: (l, 0, 0))],
        out_specs=[pl.BlockSpec((None, tm, n), lambda l, i: (l, i, 0)),
                   pl.BlockSpec((None, nbat, mem_len * n // LANES, LANES), lambda l, i: (l, i, 0, 0))],
        out_shape=[jax.ShapeDtypeStruct((nl, m, n), F32),
                   jax.ShapeDtypeStruct((nl, m // mem_len, mem_len * n // LANES, LANES), F32)],
        compiler_params=_params("parallel", "parallel"),
        name="mem_kv_proj",
    )(a, w)


def _mm_rows(a, w, name, row_off, n_rows):
    k = a.shape[1]
    n = w.shape[1]
    tm = ROW_TILE
    return pl.pallas_call(
        _mm_kernel,
        grid=(n_rows // tm,),
        in_specs=[_rows(tm, k, row_off // tm), _full((k, n))],
        out_specs=_rows(tm, n),
        out_shape=jax.ShapeDtypeStruct((n_rows, n), F32),
        compiler_params=_params("parallel"),
        name=name,
    )(a, w)


def _mm_ln_kernel(a_ref, w_ref, bias_ref, x_ref, g_ref, b_ref, *rest, with_logits, n_alias):
    h = _dot(a_ref[...].astype(BF16), w_ref[...]) + bias_ref[...]
    y = _layer_norm(ALPHA * x_ref[...] + h, g_ref[...], b_ref[...])
    if with_logits:
        wr_hi_ref, wr_lo_ref = rest[:2]
        o_ref, lg_ref = rest[2 + n_alias:]
        lg_ref[...] = _dot_hi(y, wr_hi_ref[...], wr_lo_ref[...])
    else:
        (o_ref,) = rest[n_alias:]
    o_ref[...] = y


def _mm_ln(a, w, bias, x, g, b, name, router=None, row_off=0, fill=None):
    m, k = a.shape
    t, d = x.shape
    tm = ROW_TILE
    off = row_off // tm
    in_specs = [_rows(tm, k), _full((k, d)), _full((1, d)), _rows(tm, d, off), _full((1, d)), _full((1, d))]
    args = [a, w, bias, x, g, b]
    out_specs = [_rows(tm, d, off)]
    out_shape = [jax.ShapeDtypeStruct((t, d), F32)]
    if router is not None:
        in_specs += [_full((d, LANES)), _full((d, LANES))]
        args += list(router)
        out_specs.append(_rows(tm, LANES, off))
        out_shape.append(jax.ShapeDtypeStruct((t, LANES), F32))
    aliases = {}
    for n, arr in enumerate(fill or ()):
        in_specs.append(pl.BlockSpec(memory_space=pl.ANY))
        args.append(arr)
        aliases[len(args) - 1] = n
    res = pl.pallas_call(
        functools.partial(_mm_ln_kernel, with_logits=router is not None, n_alias=len(aliases)),
        grid=(m // tm,),
        in_specs=in_specs,
        out_specs=out_specs,
        out_shape=out_shape,
        input_output_aliases=aliases,
        compiler_params=_params("parallel"),
        name=name,
    )(*args)
    return res if router is not None else res[0]


def _gmlp_kernel(x_ref, w_in_ref, b_in_ref, lng_ref, lnb_ref, wc_ref, bc_ref, w_out_ref, b_out_ref,
                 g_ref, b_ref, *rest, emit_v, n_chunks):
    if emit_v:
        _alias_ref, o_ref, v_ref, vn_ref = rest
    else:
        o_ref, vn_ref = rest
    half = w_out_ref.shape[0]
    gw = half // A_GROUPS
    x = x_ref[...]
    xb = x.astype(BF16)
    v = _gelu(_dot(xb, w_in_ref[:, half:]) + b_in_ref[:, half:])
    vn = _layer_norm(v, lng_ref[...], lnb_ref[...])
    vn_ref[...] = vn
    if emit_v:
        v_ref[...] = vn
    acc = jnp.zeros(x.shape, F32)
    for grp in range(A_GROUPS):
        cols = slice(grp * gw, (grp + 1) * gw)
        u = _gelu(_dot(xb, w_in_ref[:, cols]) + b_in_ref[:, cols])
        mixed = []
        for c in range(n_chunks):
            vc = vn_ref[c * A_CHUNK:(c + 1) * A_CHUNK, cols].astype(BF16)
            mixed.append(_dot(wc_ref[grp], vc) + bc_ref[grp])
        mixed = mixed[0] if n_chunks == 1 else jnp.concatenate(mixed, axis=0)
        acc = acc + _dot((u * mixed).astype(BF16), w_out_ref[cols, :])
    h = acc + b_out_ref[...]
    o_ref[...] = _layer_norm(ALPHA * x + h, g_ref[...], b_ref[...])


GMLP_CHUNKS = 4


def _gmlp(x, row_off, t, w_in, b_in, lng, lnb, wc, bc, w_out, b_out, g, b, alias=None):
    n_rows, d = x.shape
    ffn = w_in.shape[1]
    half = ffn // 2
    tm = GMLP_CHUNKS * A_CHUNK
    emit_v = alias is not None
    off = row_off // tm
    once = lambda shape: pl.BlockSpec(shape, lambda *_: (0,) * len(shape), pipeline_mode=pl.Buffered(1))
    in_specs = [_rows(tm, d), once((d, ffn)), _full((1, ffn)), _full((1, half)), _full((1, half)),
                _full((A_GROUPS, A_CHUNK, A_CHUNK)), _full((A_GROUPS, A_CHUNK, 1)), once((half, d)),
                _full((1, d)), _full((1, d)), _full((1, d))]
    args = [x, w_in, b_in, lng, lnb, wc, bc, w_out, b_out, g, b]
    out_specs = _rows(tm, d, off)
    out_shape = jax.ShapeDtypeStruct((t, d), F32)
    aliases = {}
    if emit_v:
        in_specs.append(pl.BlockSpec(memory_space=pl.ANY))
        args.append(alias)
        aliases = {len(args) - 1: 0}
        out_specs = [out_specs, _rows(tm, half)]
        out_shape = [out_shape, jax.ShapeDtypeStruct((n_rows, half), F32)]
    return pl.pallas_call(
        functools.partial(_gmlp_kernel, emit_v=emit_v, n_chunks=tm // A_CHUNK),
        grid=(n_rows // tm,),
        in_specs=in_specs,
        out_specs=out_specs,
        out_shape=out_shape,
        scratch_shapes=[pltpu.VMEM((tm, half), F32)],
        input_output_aliases=aliases,
        compiler_params=_params("parallel"),
        name="gmlp_sample" if emit_v else "gmlp_prompt",
    )(*args)


def _gla_proj_kernel(x_ref, w_ref, wlow_ref, g2_hi_ref, g2_lo_ref, bg_ref,
                     q_ref, la_ref, k_ref, v_ref, r_ref, *, dk, dv, q_scale):
    xb = x_ref[...].astype(BF16)
    q_ref[...] = _dot(xb, w_ref[:, 0:dk]) * q_scale
    k_ref[...] = _dot(xb, w_ref[:, dk:2 * dk])
    v_ref[...] = _dot(xb, w_ref[:, 2 * dk:2 * dk + dv])
    r_ref[...] = _dot(xb, w_ref[:, 2 * dk + dv:2 * dk + 2 * dv])
    g_low = _dot(xb, wlow_ref[...])
    pre = _dot_hi(g_low, g2_hi_ref[...], g2_lo_ref[...]) + bg_ref[...]
    la_ref[...] = _log_sigmoid(pre) * (1.0 / B_GATE_TAU)


def _gla_proj(x, weights, dk, dv, row_off, n_rows):
    d = x.shape[1]
    tm = ROW_TILE
    shapes = [dk, dk, dk, dv, dv]
    return pl.pallas_call(
        functools.partial(_gla_proj_kernel, dk=dk, dv=dv, q_scale=(dk // B_HEADS) ** -0.5),
        grid=(n_rows // tm,),
        in_specs=[_rows(tm, d, row_off // tm)] + [_full(w.shape) for w in weights],
        out_specs=[_rows(tm, n) for n in shapes],
        out_shape=[jax.ShapeDtypeStruct((n_rows, n), F32) for n in shapes],
        compiler_params=_params("parallel"),
        name="gla_proj",
    )(x, *weights)


def _hgrn_proj_kernel(x_ref, w_ref, lb_ref, q_ref, lf_ref, k_ref, v_ref, gt_ref, *, d, q_scale):
    xb = x_ref[...].astype(BF16)
    lb = lb_ref[...]
    q_ref[...] = _silu(_dot(xb, w_ref[:, 0:d])) * q_scale
    f = _dot(xb, w_ref[:, d:2 * d])
    lf_ref[...] = jnp.log(lb + (1.0 - lb) * _sigmoid(f))
    k_ref[...] = (1.0 - lb) * _sigmoid(-f)
    v_ref[...] = _dot(xb, w_ref[:, 2 * d:3 * d])
    gt_ref[...] = _dot(xb, w_ref[:, 3 * d:4 * d])


def _hgrn_proj(x, weights, row_off, n_rows):
    d = x.shape[1]
    tm = ROW_TILE
    return pl.pallas_call(
        functools.partial(_hgrn_proj_kernel, d=d, q_scale=C_EXPAND ** -0.5),
        grid=(n_rows // tm,),
        in_specs=[_rows(tm, d, row_off // tm)] + [_full(w.shape) for w in weights],
        out_specs=[_rows(tm, d)] * 5,
        out_shape=[jax.ShapeDtypeStruct((n_rows, d), F32)] * 5,
        compiler_params=_params("parallel"),
        name="hgrn_proj",
    )(x, *weights)


def _cumsum_rows(x, chunk):
    pos = lax.broadcasted_iota(jnp.int32, x.shape, 0) & (chunk - 1)
    step = 1
    while step < chunk:
        x = x + jnp.where(pos >= step, pltpu.roll(x, step, axis=0), 0.0)
        step *= 2
    return x


def _chunk_rows(x, row, chunk, n_chunks):
    parts = [jnp.broadcast_to(x[c * chunk + row:c * chunk + row + 1, :], (chunk, x.shape[1]))
             for c in range(n_chunks)]
    return parts[0] if n_chunks == 1 else jnp.concatenate(parts, axis=0)


def _linrec_mask(tl, chunk):
    r_id = lax.broadcasted_iota(jnp.int32, (tl, tl), 0)
    c_id = lax.broadcasted_iota(jnp.int32, (tl, tl), 1)
    return (r_id >= c_id) & ((r_id // chunk) == (c_id // chunk))


def _linrec_heads(q_ref, k_ref, g_ref, v_ref, gate_ref, ng_ref, o_ref, st_ref, rs, *, heads, dk, dv, chunk, n_chunks):
    mask = _linrec_mask(chunk * n_chunks, chunk)
    for h in range(heads):
        ks = slice(h * dk, (h + 1) * dk)
        vs = slice(h * dv, (h + 1) * dv)
        bcum = _cumsum_rows(g_ref[rs, ks], chunk)
        b_mid = _chunk_rows(bcum, chunk // 2 - 1, chunk, n_chunks)
        b_end = _chunk_rows(bcum, chunk - 1, chunk, n_chunks)
        q = q_ref[rs, ks]
        k = k_ref[rs, ks]
        vb = v_ref[rs, vs].astype(BF16)
        q_in = (q * jnp.exp(bcum)).astype(BF16)
        q_a = (q * jnp.exp(bcum - b_mid)).astype(BF16)
        k_a = (k * jnp.exp(b_mid - bcum)).astype(BF16)
        k_end = k * jnp.exp(b_end - bcum)
        a = jnp.where(mask, _dot_nt(q_a, k_a), 0.0).astype(BF16)
        o_intra = _dot(a, vb)
        st = st_ref[h]
        outs = []
        for c in range(n_chunks):
            cr = slice(c * chunk, (c + 1) * chunk)
            outs.append(o_intra[cr] + _dot(q_in[cr], st.astype(BF16)))
            decay = jnp.exp(bcum[cr].T[:, chunk - 1:chunk])
            st = st * decay + _dot(k_end[cr].T.astype(BF16), vb[cr])
        st_ref[h] = st
        o = outs[0] if n_chunks == 1 else jnp.concatenate(outs, axis=0)
        o = o * lax.rsqrt(jnp.mean(o * o, axis=-1, keepdims=True) + LN_EPS) * ng_ref[:, vs]
        o_ref[rs, vs] = o * _silu(gate_ref[rs, vs])


def _linrec_kernel(q_ref, k_ref, g_ref, v_ref, gate_ref, ng_ref, s0_ref, o_ref, sout_ref, st_ref, *, nb, tl, **dims):
    li = pl.program_id(1)
    for n in range(nb):
        @pl.when(li == 0)
        def _():
            st_ref[...] = s0_ref[n]

        _linrec_heads(q_ref, k_ref, g_ref, v_ref, gate_ref, ng_ref, o_ref, st_ref, slice(n * tl, (n + 1) * tl),
                      **dims)
        sout_ref[n] = st_ref[...]


LINREC_CHUNK = 2 * LIN_CHUNK
LINREC_SAMPLE_NB = 4


def _linrec_tiling(seq):
    chunk = LINREC_CHUNK if seq % LINREC_CHUNK == 0 else seq
    tl = min(seq, 4 * chunk)
    return chunk, tl, seq // tl


def _linrec(q, k, g, v, gate, ng, s0, *, n_batch, seq, heads):
    t, hk = q.shape
    hv = v.shape[1]
    dk, dv = hk // heads, hv // heads
    chunk, tl, n_l = _linrec_tiling(seq)
    nb = LINREC_SAMPLE_NB if n_l == 1 else 1

    def rows(n):
        return pl.BlockSpec((nb * tl, n), lambda b, l: (b * n_l + l, 0))

    state_spec = pl.BlockSpec((nb, heads, dk, dv), lambda b, l: (b, 0, 0, 0))
    return pl.pallas_call(
        functools.partial(_linrec_kernel, nb=nb, tl=tl, heads=heads, dk=dk, dv=dv, chunk=chunk,
                          n_chunks=tl // chunk),
        grid=(n_batch // nb, n_l),
        in_specs=[rows(hk), rows(hk), rows(hk), rows(hv), rows(hv), pl.BlockSpec((1, hv), lambda b, l: (0, 0)),
                  state_spec],
        out_specs=[rows(hv), state_spec],
        out_shape=[jax.ShapeDtypeStruct((t, hv), F32), jax.ShapeDtypeStruct((n_batch, heads, dk, dv), F32)],
        scratch_shapes=[pltpu.VMEM((heads, dk, dv), F32)],
        compiler_params=_params("parallel", "arbitrary"),
        name="linrec_h%d" % heads,
    )(q, k, g, v, gate, ng, s0)


def _linrec_block_kernel(*refs, proj, n_proj_w, **dims):
    x_ref = refs[0]
    proj_w = refs[1:1 + n_proj_w]
    ng_ref, s0_ref, wo_ref, g_ref, b_ref, o_ref, sout_ref = refs[1 + n_proj_w:8 + n_proj_w]
    q_s, g_s, k_s, v_s, gate_s, o_s, st_ref = refs[8 + n_proj_w:]

    @pl.when(pl.program_id(1) == 0)
    def _():
        st_ref[...] = s0_ref[0]

    proj(x_ref, *proj_w, q_s, g_s, k_s, v_s, gate_s)
    _linrec_heads(q_s, k_s, g_s, v_s, gate_s, ng_ref, o_s, st_ref, slice(0, x_ref.shape[0]), **dims)
    h = _dot(o_s[...].astype(BF16), wo_ref[...])
    o_ref[...] = _layer_norm(ALPHA * x_ref[...] + h, g_ref[...], b_ref[...])
    sout_ref[0] = st_ref[...]


def _linrec_block(x, proj, proj_w, ng, s0, w_out, g, b, *, n_batch, seq, heads, hk, hv, name):
    t, d = x.shape
    dk, dv = hk // heads, hv // heads
    chunk, tl, n_l = _linrec_tiling(seq)
    rows = pl.BlockSpec((tl, d), lambda bi, l: (bi * n_l + l, 0))
    full = lambda shape: pl.BlockSpec(shape, lambda bi, l: (0,) * len(shape))
    state_spec = pl.BlockSpec((1, heads, dk, dv), lambda bi, l: (bi, 0, 0, 0))
    return pl.pallas_call(
        functools.partial(_linrec_block_kernel, proj=proj, n_proj_w=len(proj_w), heads=heads, dk=dk, dv=dv,
                          chunk=chunk, n_chunks=tl // chunk),
        grid=(n_batch, n_l),
        in_specs=[rows] + [full(w.shape) for w in proj_w] + [full((1, hv)), state_spec, full(w_out.shape),
                                                            full((1, d)), full((1, d))],
        out_specs=[rows, state_spec],
        out_shape=[jax.ShapeDtypeStruct((t, d), F32), jax.ShapeDtypeStruct((n_batch, heads, dk, dv), F32)],
        scratch_shapes=[pltpu.VMEM((tl, hk), F32), pltpu.VMEM((tl, hk), F32), pltpu.VMEM((tl, hk), F32),
                        pltpu.VMEM((tl, hv), F32), pltpu.VMEM((tl, hv), F32), pltpu.VMEM((tl, hv), F32),
                        pltpu.VMEM((heads, dk, dv), F32)],
        compiler_params=_params("parallel", "arbitrary"),
        name=name,
    )(x, *proj_w, ng, s0, w_out, g, b)


def _glu_kernel(x_ref, w_ref, b_ref, o_ref, *, d):
    xb = x_ref[...].astype(BF16)
    a = _dot(xb, w_ref[:, 0:d]) + b_ref[:, 0:d]
    gate = _dot(xb, w_ref[:, d:2 * d]) + b_ref[:, d:2 * d]
    o_ref[...] = a * _sigmoid(gate)


def _glu(x, w, b):
    t, d = x.shape
    tm = ROW_TILE
    return pl.pallas_call(
        functools.partial(_glu_kernel, d=d),
        grid=(t // tm,),
        in_specs=[_rows(tm, d), _full(w.shape), _full(b.shape)],
        out_specs=_rows(tm, d),
        out_shape=jax.ShapeDtypeStruct((t, d), F32),
        compiler_params=_params("parallel"),
        name="conf_glu",
    )(x, w, b)


CONV_PAD = 32


def _conv_kernel(h_ref, st_ref, wdw_ref, bdw_ref, lg_ref, lb_ref, *rest, nb, tl, aliased):
    if aliased:
        _alias_ref, o_ref, sout_ref, buf_ref, conv_ref = rest
    else:
        o_ref, sout_ref, buf_ref, conv_ref = rest
    li = pl.program_id(1)
    d = h_ref.shape[-1]
    lead = CONV_PAD - D_BUF
    rb = min(tl, 64)
    cw = LANES
    buf_ref[CONV_PAD + tl:CONV_PAD + tl + SUBLANES, :] = jnp.zeros((SUBLANES, d), F32)
    for n in range(nb):
        @pl.when(li == 0)
        def _():
            buf_ref[lead:CONV_PAD, :] = st_ref[n]

        buf_ref[CONV_PAD:CONV_PAD + tl, :] = h_ref[n * tl:(n + 1) * tl, :]
        for r0 in range(0, tl, rb):
            for c0 in range(0, d, cw):
                cols = slice(c0, c0 + cw)
                acc = jnp.zeros((rb, cw), F32)
                for s in range(SUBLANES):
                    part = None
                    for a in range((CONV_PAD + SUBLANES) // SUBLANES):
                        j = SUBLANES * a + s - lead
                        if 0 <= j < D_CONV_W:
                            rows = slice(r0 + SUBLANES * a, r0 + SUBLANES * a + rb + SUBLANES)
                            term = wdw_ref[j:j + 1, cols] * buf_ref[rows, cols]
                            part = term if part is None else part + term
                    acc = acc + part[s:s + rb, :]
                conv_ref[r0:r0 + rb, cols] = acc + bdw_ref[:, cols]
        y = _layer_norm(conv_ref[...], lg_ref[...], lb_ref[...])
        o_ref[n * tl:(n + 1) * tl, :] = _silu(y)
        sout_ref[n] = buf_ref[tl + lead:tl + CONV_PAD, :]
        buf_ref[0:CONV_PAD, :] = buf_ref[tl:tl + CONV_PAD, :]


def _conv(h, state, wdw, bdw, lg, lb, *, n_batch, seq, row_off, alias=None):
    t, d = h.shape
    tl = min(seq, 256)
    n_l = seq // tl
    nb = 8 if n_l == 1 else 1
    off = row_off // (nb * tl)
    rows = pl.BlockSpec((nb * tl, d), lambda b, l: (off + b * n_l + l, 0))
    state_spec = pl.BlockSpec((nb, D_BUF, d), lambda b, l: (b, 0, 0))
    in_specs = [rows, state_spec, _full(wdw.shape), _full((1, d)), _full((1, d)), _full((1, d))]
    args = [h, state, wdw, bdw, lg, lb]
    aliases = {}
    if alias is not None:
        in_specs.append(pl.BlockSpec(memory_space=pl.ANY))
        args.append(alias)
        aliases = {len(args) - 1: 0}
    return pl.pallas_call(
        functools.partial(_conv_kernel, nb=nb, tl=tl, aliased=alias is not None),
        grid=(n_batch // nb, n_l),
        in_specs=in_specs,
        out_specs=[rows, state_spec],
        out_shape=[jax.ShapeDtypeStruct((t, d), F32), jax.ShapeDtypeStruct((n_batch, D_BUF, d), F32)],
        scratch_shapes=[pltpu.VMEM((CONV_PAD + tl + SUBLANES, d), F32), pltpu.VMEM((tl, d), F32)],
        input_output_aliases=aliases,
        compiler_params=_params("parallel", "arbitrary"),
        name="conf_conv_sample" if alias is not None else "conf_conv_prompt",
    )(*args)


def _attn_heads(q, k_of, v_of, hd):
    outs = []
    for h in range(MEM_HEADS):
        hs = slice(h * hd, (h + 1) * hd)
        s = _dot_nt(q[:, hs].astype(BF16), k_of(hs).astype(BF16)) * (hd ** -0.5)
        p = jnp.exp(s - jnp.max(s, axis=-1, keepdims=True))
        denom = jnp.sum(p, axis=-1, keepdims=True)
        outs.append(_dot(p.astype(BF16), v_of(hs).astype(BF16)) / denom)
    return outs


def _attn_block_kernel(x_ref, wq_ref, k_ref, v_ref, wo_ref, g_ref, b_ref, wr_hi_ref, wr_lo_ref, o_ref, lg_ref):
    x = x_ref[...]
    hd = x.shape[1] // MEM_HEADS
    q = _dot(x.astype(BF16), wq_ref[...])
    att = jnp.concatenate(_attn_heads(q, lambda hs: k_ref[:, hs], lambda hs: v_ref[:, hs], hd), axis=1)
    y = _layer_norm(ALPHA * x + _dot(att.astype(BF16), wo_ref[...]), g_ref[...], b_ref[...])
    lg_ref[...] = _dot_hi(y, wr_hi_ref[...], wr_lo_ref[...])
    o_ref[...] = y


def _attn_block_prompt(x, w_q, mem_k, mem_v, w_o, g, b, router, layer, n_batch, seq):
    t, d = x.shape
    m = mem_k.shape[1] // n_batch
    tl = ROW_TILE
    n_l = seq // tl
    rows = lambda n: pl.BlockSpec((tl, n), lambda bi, l: (bi * n_l + l, 0))
    kv = pl.BlockSpec((None, m, d), lambda bi, l: (layer, bi, 0))
    full = lambda shape: pl.BlockSpec(shape, lambda bi, l: (0,) * len(shape))
    return pl.pallas_call(
        _attn_block_kernel,
        grid=(n_batch, n_l),
        in_specs=[rows(d), full((d, d)), kv, kv, full((d, d)), full((1, d)), full((1, d)),
                  full((d, LANES)), full((d, LANES))],
        out_specs=[rows(d), rows(LANES)],
        out_shape=[jax.ShapeDtypeStruct((t, d), F32), jax.ShapeDtypeStruct((t, LANES), F32)],
        compiler_params=_params("parallel", "parallel"),
        name="attn_block_prompt",
    )(x, w_q, mem_k, mem_v, w_o, g, b, *router)


ATTN_SAMPLE_NB = 4


def _attn_sample_kernel(q_ref, k_ref, v_ref, o_ref, *, nb, seq):
    hd = q_ref.shape[-1] // MEM_HEADS
    n_lt = hd // LANES
    m = k_ref.shape[1] // (MEM_HEADS * n_lt)

    def head(ref, n, h):
        tiles = [ref[n, pl.ds(lt * MEM_HEADS + h, m, stride=MEM_HEADS * n_lt), :] for lt in range(n_lt)]
        return jnp.concatenate(tiles, axis=1).astype(BF16)

    pairs = [(n, h) for n in range(nb) for h in range(MEM_HEADS)]
    s = jnp.concatenate(
        [_dot_nt(q_ref[n * seq:(n + 1) * seq, h * hd:(h + 1) * hd].astype(BF16), head(k_ref, n, h))
         for n, h in pairs], axis=0) * (hd ** -0.5)
    p = jnp.exp(s - jnp.max(s, axis=-1, keepdims=True))
    inv = 1.0 / jnp.sum(p, axis=-1, keepdims=True)
    for idx, (n, h) in enumerate(pairs):
        rs = slice(idx * seq, (idx + 1) * seq)
        o = _dot(p[rs].astype(BF16), head(v_ref, n, h)) * inv[rs]
        o_ref[n * seq:(n + 1) * seq, h * hd:(h + 1) * hd] = o


def _cache_rows(cache):
    nl, nbat, m, heads, hd = cache.shape
    c = cache.reshape(nl, nbat, m, heads, hd // LANES, LANES)
    return c.transpose(0, 1, 2, 4, 3, 5).reshape(nl, nbat, m * (hd // LANES) * heads, LANES)


def _cache_unrows(flat, heads):
    nl, nbat, rows, _ = flat.shape
    n_lt = D_MODEL // (heads * LANES)
    m = rows // (heads * n_lt)
    c = flat.reshape(nl, nbat, m, n_lt, heads, LANES).transpose(0, 1, 2, 4, 3, 5)
    return c.reshape(nl, nbat, m, heads, n_lt * LANES)


def _attn_sample(q, cache_k, cache_v, layer, n_batch, seq):
    t, d = q.shape
    nb = ATTN_SAMPLE_NB
    rows = pl.BlockSpec((nb * seq, d), lambda i: (i, 0))
    kv = pl.BlockSpec((None, nb) + cache_k.shape[2:], lambda i: (layer, i, 0, 0))
    return pl.pallas_call(
        functools.partial(_attn_sample_kernel, nb=nb, seq=seq),
        grid=(n_batch // nb,),
        in_specs=[rows, kv, kv],
        out_specs=rows,
        out_shape=jax.ShapeDtypeStruct((t, d), F32),
        compiler_params=_params("parallel"),
        name="attn_sample",
    )(q, cache_k, cache_v)


def _route_kernel(lg_ref, bias_ref, ltri_ref, route_ref, counts_ref, carry_ref):
    i = pl.program_id(0)

    @pl.when(i == 0)
    def _():
        carry_ref[...] = jnp.zeros(carry_ref.shape, F32)

    z = lg_ref[...] + bias_ref[...]
    lane = lax.broadcasted_iota(jnp.int32, z.shape, 1).astype(F32)
    neg = -jnp.inf
    far = float(LANES)

    def first_max(mask):
        vmax = jnp.max(jnp.where(mask, z, neg), axis=-1, keepdims=True)
        idx = jnp.min(jnp.where(mask & (z == vmax), lane, far), axis=-1, keepdims=True)
        return vmax, idx

    gmask = lane < float(MOE_GROUPS)
    gmax, gidx = first_max(gmask)
    gsum = jnp.sum(jnp.where(gmask, jnp.exp(z - gmax), 0.0), axis=-1, keepdims=True)
    g_w = 1.0 / gsum
    lo = float(ROUTE_LANE0) + float(MOE_PER_GROUP) * gidx
    emask = (lane >= lo) & (lane < lo + float(MOE_PER_GROUP))
    v1, i1 = first_max(emask)
    v2, i2 = first_max(emask & (lane != i1))
    tt = jnp.exp(v2 - v1)
    w0 = g_w / (1.0 + tt)
    w1 = g_w * tt / (1.0 + tt)
    sel1 = lane == i1
    sel2 = lane == i2
    onehot = jnp.where(sel1 | sel2, 1.0, 0.0)
    before = _dot(ltri_ref[...], onehot.astype(BF16)) + carry_ref[...]
    rank0 = jnp.sum(jnp.where(sel1, before, 0.0), axis=-1, keepdims=True)
    rank1 = jnp.sum(jnp.where(sel2, before, 0.0), axis=-1, keepdims=True)
    carry = carry_ref[...] + jnp.sum(onehot, axis=0, keepdims=True)
    carry_ref[...] = carry
    counts_ref[...] = carry
    e_off = float(ROUTE_LANE0)
    out = jnp.zeros(z.shape, F32)
    for ln, val in enumerate((i1 - e_off, i2 - e_off, w0, w1, rank0, rank1)):
        out = jnp.where(lane == float(ln), val, out)
    route_ref[...] = out


def _route(logits, bias, ltri):
    t = logits.shape[0]
    tm = ROW_TILE
    return pl.pallas_call(
        _route_kernel,
        grid=(t // tm,),
        in_specs=[_rows(tm, LANES), _full((1, LANES)), _full((tm, tm))],
        out_specs=[_rows(tm, LANES), _full((1, LANES))],
        out_shape=[jax.ShapeDtypeStruct((t, LANES), F32), jax.ShapeDtypeStruct((1, LANES), F32)],
        scratch_shapes=[pltpu.VMEM((1, LANES), F32)],
        compiler_params=_params("arbitrary"),
        name="moe_route",
    )(logits, bias, ltri)


def _row_copy_wait(src_rows, dst_rows, sem):
    pltpu.make_async_copy(src_rows, dst_rows, sem).wait()


def _dispatch_kernel(tail_ref, dest_ref, x_ref, xs_ref, zeros_ref, sem, *, tm):
    @pl.when(pl.program_id(0) == 0)
    def _():
        zeros_ref[...] = jnp.zeros(zeros_ref.shape, F32)
        bm = zeros_ref.shape[0]
        tails = [xs_ref.at[pl.ds(pl.multiple_of(tail_ref[e], bm), bm)] for e in range(MOE_EXPERTS)]
        for dst in tails:
            pltpu.make_async_copy(zeros_ref, dst, sem).start()
        for dst in tails:
            pltpu.make_async_copy(zeros_ref, dst, sem).wait()

    def issue(t, carry):
        for j in range(2):
            d = dest_ref[2 * t + j]
            pltpu.make_async_copy(x_ref.at[pl.ds(t, 1)], xs_ref.at[pl.ds(d, 1)], sem).start(priority=j)
        return carry

    lax.fori_loop(0, tm, issue, 0, unroll=8)
    for _ in range(2):
        _row_copy_wait(x_ref, xs_ref.at[pl.ds(0, tm)], sem)


def _dispatch(tail_rows, dest_flat, x, n_rows):
    t, d = x.shape
    tm = ROW_TILE
    grid_spec = pltpu.PrefetchScalarGridSpec(
        num_scalar_prefetch=1,
        grid=(t // tm,),
        in_specs=[pl.BlockSpec((2 * tm,), lambda i, tail: (i,), memory_space=pltpu.SMEM),
                  pl.BlockSpec((tm, d), lambda i, tail: (i, 0))],
        out_specs=pl.BlockSpec(memory_space=pl.ANY),
        scratch_shapes=[pltpu.VMEM((MOE_BLOCK_ROWS, d), F32), pltpu.SemaphoreType.DMA(())],
    )
    return pl.pallas_call(
        functools.partial(_dispatch_kernel, tm=tm),
        grid_spec=grid_spec,
        out_shape=jax.ShapeDtypeStruct((n_rows, d), F32),
        compiler_params=_params("arbitrary"),
        name="moe_dispatch",
    )(tail_rows, dest_flat, x)


def _expert_kernel(first_ref, count_ref, xs_hbm, wgu_ref, wd_ref, y_hbm, xbuf, ybuf, wgu_bf, wd_bf, isem, osem):
    e = pl.program_id(0)
    n = count_ref[e]
    b0 = first_ref[e]
    bm = xbuf.shape[1]

    def rows_of(k):
        return pl.ds(pl.multiple_of((b0 + k) * bm, bm), bm)

    def in_copy(k, slot):
        return pltpu.make_async_copy(xs_hbm.at[rows_of(k)], xbuf.at[slot], isem.at[slot])

    def out_copy(k, slot):
        return pltpu.make_async_copy(ybuf.at[slot], y_hbm.at[rows_of(k)], osem.at[slot])

    @pl.when(n > 0)
    def _():
        in_copy(0, 0).start()
        wgu_bf[...] = wgu_ref[...].astype(BF16)
        wd_bf[...] = wd_ref[...].astype(BF16)
        hid = wd_bf.shape[0]

        def block(k, carry):
            slot = lax.rem(k, 2)
            in_copy(k, slot).wait()

            @pl.when(k + 1 < n)
            def _():
                in_copy(k + 1, 1 - slot).start()

            @pl.when(k >= 2)
            def _():
                out_copy(k - 2, slot).wait()

            xb = xbuf[slot].astype(BF16)
            gate = _dot(xb, wgu_bf[:, 0:hid])
            up = _dot(xb, wgu_bf[:, hid:2 * hid])
            ybuf[slot] = _dot((_silu(gate) * up).astype(BF16), wd_bf[...])
            out_copy(k, slot).start()
            return carry

        lax.fori_loop(0, n, block, 0)

        @pl.when(n >= 2)
        def _():
            out_copy(n - 2, lax.rem(n, 2)).wait()

        out_copy(n - 1, lax.rem(n - 1, 2)).wait()


def _experts(first_blk, n_blk, xs, w_gate_up, w_down, layer):
    nr, d = xs.shape
    bm = MOE_BLOCK_ROWS
    hid2 = w_gate_up.shape[-1]
    hid = w_down.shape[-2]
    grid_spec = pltpu.PrefetchScalarGridSpec(
        num_scalar_prefetch=2,
        grid=(MOE_EXPERTS,),
        in_specs=[pl.BlockSpec(memory_space=pl.ANY),
                  pl.BlockSpec((None, None, d, hid2), lambda e, fb, nb: (layer, e, 0, 0)),
                  pl.BlockSpec((None, None, hid, d), lambda e, fb, nb: (layer, e, 0, 0))],
        out_specs=pl.BlockSpec(memory_space=pl.ANY),
        scratch_shapes=[pltpu.VMEM((2, bm, d), F32), pltpu.VMEM((2, bm, d), F32),
                        pltpu.VMEM((d, hid2), BF16), pltpu.VMEM((hid, d), BF16),
                        pltpu.SemaphoreType.DMA((2,)), pltpu.SemaphoreType.DMA((2,))],
    )
    return pl.pallas_call(
        _expert_kernel,
        grid_spec=grid_spec,
        out_shape=jax.ShapeDtypeStruct((nr, d), F32),
        compiler_params=_params("arbitrary"),
        name="moe_experts",
    )(first_blk, n_blk, xs, w_gate_up, w_down)


def _combine_kernel(dest_ref, dest_next_ref, x_ref, route_ref, g_ref, b_ref, y_hbm, *rest, tm, n_first):
    if n_first is None:
        o_ref, ybuf, sems = rest
    else:
        o_ref, o2_ref, ybuf, sems = rest
    i = pl.program_id(0)
    n_tiles = pl.num_programs(0)

    def gather(d_ref, slot):
        def issue(t, carry):
            for j in range(2):
                d = d_ref[2 * t + j]
                pltpu.make_async_copy(y_hbm.at[pl.ds(d, 1)], ybuf.at[slot, j, pl.ds(t, 1)],
                                      sems.at[slot]).start(priority=j)
            return carry

        lax.fori_loop(0, tm, issue, 0, unroll=8)

    @pl.when(i == 0)
    def _():
        gather(dest_ref, 0)

    @pl.when(i + 1 < n_tiles)
    def _():
        gather(dest_next_ref, lax.rem(i + 1, 2))

    slot = lax.rem(i, 2)
    for j in range(2):
        _row_copy_wait(y_hbm.at[pl.ds(0, tm)], ybuf.at[slot, j], sems.at[slot])
    moe = route_ref[:, 2:3] * ybuf[slot, 0] + route_ref[:, 3:4] * ybuf[slot, 1]
    y = _layer_norm(ALPHA * x_ref[...] + moe, g_ref[...], b_ref[...])
    if n_first is None:
        o_ref[...] = y
    else:
        @pl.when(i < n_first)
        def _():
            o_ref[...] = y

        @pl.when(i >= n_first)
        def _():
            o2_ref[...] = y


def _combine(dest_flat, x, route, g, b, yb, split_rows=None):
    t, d = x.shape
    tm = ROW_TILE
    n_tiles = t // tm
    out_specs = _rows(tm, d)
    out_shape = jax.ShapeDtypeStruct((t, d), F32)
    n_first = None
    if split_rows is not None:
        n_first = split_rows // tm
        out_specs = [pl.BlockSpec((tm, d), lambda i: (jnp.minimum(i, n_first - 1), 0)),
                     pl.BlockSpec((tm, d), lambda i: (jnp.maximum(i - n_first, 0), 0))]
        out_shape = [jax.ShapeDtypeStruct((split_rows, d), F32), jax.ShapeDtypeStruct((t - split_rows, d), F32)]
    return pl.pallas_call(
        functools.partial(_combine_kernel, tm=tm, n_first=n_first),
        grid=(n_tiles,),
        in_specs=[pl.BlockSpec((2 * tm,), lambda i: (i,), memory_space=pltpu.SMEM),
                  pl.BlockSpec((2 * tm,), lambda i: (jnp.minimum(i + 1, n_tiles - 1),), memory_space=pltpu.SMEM),
                  _rows(tm, d), _rows(tm, LANES), _full((1, d)), _full((1, d)),
                  pl.BlockSpec(memory_space=pl.ANY)],
        out_specs=out_specs,
        out_shape=out_shape,
        scratch_shapes=[pltpu.VMEM((2, 2, tm, d), F32), pltpu.SemaphoreType.DMA((2,))],
        compiler_params=_params("arbitrary"),
        name="moe_combine",
    )(dest_flat, dest_flat, x, route, g, b, yb)


def _moe_plan(route, counts):
    bm = MOE_BLOCK_ROWS
    t = route.shape[0]
    n_blocks = -(-2 * t // bm) + MOE_EXPERTS
    e_idx = route[:, 0:2].astype(jnp.int32)
    rank = route[:, 4:6].astype(jnp.int32)
    cnt = counts[0, ROUTE_LANE0:ROUTE_LANE0 + MOE_EXPERTS].astype(jnp.int32)
    padded = ((cnt + bm - 1) // bm) * bm
    pad_end = jnp.cumsum(padded)
    pad_start = pad_end - padded
    experts = jnp.arange(MOE_EXPERTS, dtype=jnp.int32)
    dest = (rank + jnp.sum(jnp.where(e_idx[..., None] == experts, pad_start, 0), axis=-1)).reshape(-1)
    nused = pad_end[-1] // bm
    empty = cnt == 0
    tail_rows = jnp.where(empty, (nused + jnp.cumsum(empty.astype(jnp.int32)) - 1) * bm, pad_end - bm)
    return (dest.astype(jnp.int32), (pad_start // bm).astype(jnp.int32), (padded // bm).astype(jnp.int32),
            tail_rows.astype(jnp.int32), n_blocks)


def _hi_lo(w):
    hi = w.astype(BF16)
    return hi, (w - hi.astype(F32)).astype(BF16)


def kernel(x_prompt, x_sample, mem_prompt, cache_mem_k, cache_mem_v, state_gla, state_hgrn, state_conv,
           ln_g, ln_b, a_w_in, a_b_in, a_ln_g, a_ln_b, a_w_s, a_b_s, a_w_out, a_b_out,
           b_w_in, b_w_g2, b_b_g, b_norm_g, b_w_out, c_lb, c_w_in, c_norm_g, c_w_out,
           d_w_in, d_b_in, d_w_dw, d_b_dw, d_ln_g, d_ln_b, d_w_out, d_b_out,
           m_w_q, m_w_k, m_w_v, m_w_o, r_w_grp, r_b_grp, r_w_exp, r_b_exp, e_w_gate_up, e_w_down):
    bp, lp, d = x_prompt.shape
    bs, ls, _ = x_sample.shape
    tp, ts = bp * lp, bs * ls
    t = tp + ts
    mem_len = mem_prompt.shape[1]

    row = lambda a: a.reshape(1, -1)
    x = None

    mem2d = mem_prompt.reshape(bp * mem_len, d)
    mem_k, mem_k_flat = _mem_proj(mem2d, m_w_k.astype(BF16), mem_len)
    mem_v, mem_v_flat = _mem_proj(mem2d, m_w_v.astype(BF16), mem_len)

    cache_k = _cache_rows(cache_mem_k)
    cache_v = _cache_rows(cache_mem_v)

    lb_all = jnp.cumsum(jax.nn.softmax(c_lb.astype(F32), axis=0), axis=0)
    lb_all = lb_all - lb_all[:1]
    ltri = jnp.tril(jnp.ones((ROW_TILE, ROW_TILE), F32), -1).astype(BF16)
    zero_bias = jnp.zeros((1, d), F32)

    outs = {"v": [], "gla_p": [], "gla_s": [], "hgrn_p": [], "hgrn_s": [], "conv_p": [], "conv_s": []}
    for i in range(DEPTH):
        j = i // N_MIXERS
        kind = i % N_MIXERS
        g1, b1 = row(ln_g[i, 0]), row(ln_b[i, 0])
        if kind == 0:
            tril = jnp.tril(jnp.ones((A_CHUNK, A_CHUNK), bool))
            wc_p = jnp.where(tril, a_w_s[j], 0.0).astype(BF16)
            bc_p = a_b_s[j][:, :, None]
            reps = A_CHUNK // ls
            small = jnp.where(jnp.tril(jnp.ones((ls, ls), bool)), a_w_s[j][:, :ls, :ls], 0.0)
            wc_s = jax.vmap(lambda m: jnp.kron(jnp.eye(reps, dtype=F32), m))(small).astype(BF16)
            bc_s = jnp.tile(a_b_s[j][:, :ls], (1, reps))[:, :, None]
            common = (a_w_in[j].astype(BF16), row(a_b_in[j]), row(a_ln_g[j]), row(a_ln_b[j]))
            tail = (a_w_out[j].astype(BF16), row(a_b_out[j]), g1, b1)
            xp_in = x_prompt.reshape(tp, d) if x is None else x[:tp]
            xs_in = x_sample.reshape(ts, d) if x is None else x[tp:]
            x1 = _gmlp(xp_in, 0, t, *common, wc_p, bc_p, *tail)
            x1, v_s = _gmlp(xs_in, tp, t, *common, wc_s, bc_s, *tail, alias=x1)
            outs["v"].append(v_s.reshape(bs, ls, -1))
        elif kind == 1:
            dk, dv = b_w_g2.shape[-1], b_w_out.shape[1]
            w_in = b_w_in[j]
            w_main = w_in[:, :2 * dk + 2 * dv].astype(BF16)
            w_low = jnp.pad(w_in[:, 2 * dk + 2 * dv:], ((0, 0), (0, LANES - B_GATE_RANK))).astype(BF16)
            g2_hi, g2_lo = _hi_lo(jnp.pad(b_w_g2[j], ((0, LANES - B_GATE_RANK), (0, 0))))
            proj_w = (w_main, w_low, g2_hi, g2_lo, row(b_b_g[j]))
            proj = functools.partial(_gla_proj_kernel, dk=dk, dv=dv, q_scale=(dk // B_HEADS) ** -0.5)
            ng = row(b_norm_g[j])
            w_out = b_w_out[j].astype(BF16)
            s0_p = jnp.zeros((bp,) + state_gla.shape[2:], F32)
            x1, s_p = _linrec_block(x, proj, proj_w, ng, s0_p, w_out, g1, b1, n_batch=bp, seq=lp, heads=B_HEADS,
                                    hk=dk, hv=dv, name="gla_block_prompt")
            q, la, k, v, r = _gla_proj(x, proj_w, dk, dv, tp, ts)
            o_s, s_s = _linrec(q, k, la, v, r, ng, state_gla[j], n_batch=bs, seq=ls, heads=B_HEADS)
            x1 = _mm_ln(o_s, w_out, zero_bias, x, g1, b1, "gla_out_sample", row_off=tp, fill=(x1,))
            outs["gla_p"].append(s_p)
            outs["gla_s"].append(s_s)
        elif kind == 2:
            heads = state_hgrn.shape[2]
            proj_w = (c_w_in[j].astype(BF16), row(lb_all[i]))
            proj = functools.partial(_hgrn_proj_kernel, d=d, q_scale=C_EXPAND ** -0.5)
            ng = row(c_norm_g[j])
            w_out = c_w_out[j].astype(BF16)
            s0_p = jnp.zeros((bp,) + state_hgrn.shape[2:], F32)
            x1, s_p = _linrec_block(x, proj, proj_w, ng, s0_p, w_out, g1, b1, n_batch=bp, seq=lp, heads=heads,
                                    hk=d, hv=d, name="hgrn_block_prompt")
            q, lf, k, v, gt = _hgrn_proj(x, proj_w, tp, ts)
            o_s, s_s = _linrec(q, k, lf, v, gt, ng, state_hgrn[j], n_batch=bs, seq=ls, heads=heads)
            x1 = _mm_ln(o_s, w_out, zero_bias, x, g1, b1, "hgrn_out_sample", row_off=tp, fill=(x1,))
            outs["hgrn_p"].append(s_p)
            outs["hgrn_s"].append(s_s)
        else:
            h = _glu(x, d_w_in[j].astype(BF16), row(d_b_in[j]))
            cargs = (d_w_dw[j], row(d_b_dw[j]), row(d_ln_g[j]), row(d_ln_b[j]))
            conv0 = jnp.zeros((bp, D_BUF, d), F32)
            c, s_p = _conv(h, conv0, *cargs, n_batch=bp, seq=lp, row_off=0)
            c, s_s = _conv(h, state_conv[j], *cargs, n_batch=bs, seq=ls, row_off=tp, alias=c)
            outs["conv_p"].append(s_p)
            outs["conv_s"].append(s_s)
            x1 = _mm_ln(c, d_w_out[j].astype(BF16), row(d_b_out[j]), x, g1, b1, "conf_out")

        w_route = jnp.concatenate([r_w_grp[i], r_w_exp[i]], axis=1)
        w_route = jnp.pad(w_route, ((0, 0), (0, LANES - w_route.shape[1])))
        b_route = jnp.pad(jnp.concatenate([r_b_grp[i], r_b_exp[i]]), (0, LANES - MOE_GROUPS - MOE_EXPERTS))
        router = _hi_lo(w_route)
        w_q, w_o = m_w_q[i].astype(BF16), m_w_o[i].astype(BF16)
        g2, b2 = row(ln_g[i, 1]), row(ln_b[i, 1])
        x2, logits = _attn_block_prompt(x1, w_q, mem_k, mem_v, w_o, g2, b2, router, i, bp, lp)
        q_s = _mm_rows(x1, w_q, "attn_q_sample", tp, ts)
        att_s = _attn_sample(q_s, cache_k, cache_v, i, bs, ls)
        x2, logits = _mm_ln(att_s, w_o, zero_bias, x1, g2, b2, "attn_out_sample", router=router, row_off=tp,
                            fill=(x2, logits))

        route, counts = _route(logits, row(b_route), ltri)
        dest, first_blk, n_blk, tail_rows, n_blocks = _moe_plan(route, counts)
        xs = _dispatch(tail_rows, dest, x2, n_blocks * MOE_BLOCK_ROWS)
        yb = _experts(first_blk, n_blk, xs, e_w_gate_up, e_w_down, i)
        if i + 1 < DEPTH:
            x = _combine(dest, x2, route, row(ln_g[i, 2]), row(ln_b[i, 2]), yb)
        else:
            y_p, y_s = _combine(dest, x2, route, row(ln_g[i, 2]), row(ln_b[i, 2]), yb, split_rows=tp)

    y_prompt = y_p.reshape(bp, lp, d)
    y_sample = y_s.reshape(bs, ls, d)
    mem_k_p = _cache_unrows(mem_k_flat, MEM_HEADS)
    mem_v_p = _cache_unrows(mem_v_flat, MEM_HEADS)
    return (y_prompt, y_sample, mem_k_p, mem_v_p, jnp.stack(outs["gla_p"]), jnp.stack(outs["hgrn_p"]),
            jnp.stack(outs["conv_p"]), jnp.stack(outs["v"]), jnp.stack(outs["gla_s"]),
            jnp.stack(outs["hgrn_s"]), jnp.stack(outs["conv_s"]))
```

```python
import functools
import math

import jax
import jax.numpy as jnp
from jax import lax
from jax.experimental import pallas as pl
from jax.experimental.pallas import tpu as pltpu

F32 = jnp.float32
BF16 = jnp.bfloat16

D_MODEL = 1024
DEPTH = 4
N_MIXERS = 4
ALPHA = (2.0 * DEPTH) ** 0.25
LN_EPS = 1e-5
A_CHUNK = 128
A_GROUPS = 4
B_HEADS = 4
B_GATE_RANK = 16
B_GATE_TAU = 16.0
C_EXPAND = 128
D_CONV_W = 31
D_BUF = D_CONV_W - 1
LIN_CHUNK = 32
MEM_HEADS = 4
MOE_GROUPS = 4
MOE_PER_GROUP = 8
MOE_EXPERTS = MOE_GROUPS * MOE_PER_GROUP
MOE_HIDDEN = 512

LANES = 128
SUBLANES = 8
ROW_TILE = 512
MOE_BLOCK_ROWS = 256
ROUTE_LANE0 = MOE_GROUPS
VMEM_LIMIT = 56 * 1024 * 1024
INV_SQRT2 = 1.0 / math.sqrt(2.0)


def _params(*sem, vmem=VMEM_LIMIT):
    return pltpu.CompilerParams(dimension_semantics=sem, vmem_limit_bytes=vmem)


def _dot(a, b):
    return jnp.dot(a, b, preferred_element_type=F32)


def _dot_nt(a, b):
    return lax.dot_general(a, b, (((1,), (1,)), ((), ())), preferred_element_type=F32)


def _dot_hi(a, w_hi, w_lo):
    a_hi = a.astype(BF16)
    a_lo = (a - a_hi.astype(F32)).astype(BF16)
    return _dot(a_hi, w_hi) + _dot(a_lo, w_hi) + _dot(a_hi, w_lo)


def _layer_norm(x, g, b):
    mu = jnp.mean(x, axis=-1, keepdims=True)
    xc = x - mu
    var = jnp.mean(xc * xc, axis=-1, keepdims=True)
    return xc * lax.rsqrt(var + LN_EPS) * g + b


def _sigmoid(x):
    return 1.0 / (1.0 + jnp.exp(-x))


def _silu(x):
    return x * _sigmoid(x)


def _gelu(x):
    return 0.5 * x * (1.0 + lax.erf(x * INV_SQRT2))


def _log_sigmoid(x):
    return jnp.minimum(x, 0.0) - jnp.log(1.0 + jnp.exp(-jnp.abs(x)))


def _full(shape):
    return pl.BlockSpec(shape, lambda *_: (0,) * len(shape))


def _rows(tm, n, off=0):
    return pl.BlockSpec((tm, n), lambda i: (i + off, 0))


def _mm_kernel(a_ref, w_ref, o_ref):
    o_ref[...] = _dot(a_ref[...].astype(BF16), w_ref[...])


def _mem_proj_kernel(a_ref, w_ref, nat_ref, flat_ref, *, mem_len):
    res = _dot(a_ref[...].astype(BF16), w_ref[...])
    nat_ref[...] = res
    n_lt = res.shape[1] // (MEM_HEADS * LANES)
    for b in range(res.shape[0] // mem_len):
        for h in range(MEM_HEADS):
            for lt in range(n_lt):
                c0 = (h * n_lt + lt) * LANES
                flat_ref[b, pl.ds(lt * MEM_HEADS + h, mem_len, stride=MEM_HEADS * n_lt), :] = (
                    res[b * mem_len:(b + 1) * mem_len, c0:c0 + LANES])


def _mem_proj(a, w, mem_len):
    m, k = a.shape
    nl, _, n = w.shape
    tm = min(m, ROW_TILE)
    nbat = tm // mem_len
    return pl.pallas_call(
        functools.partial(_mem_proj_kernel, mem_len=mem_len),
        grid=(nl, m // tm),
        in_specs=[pl.BlockSpec((tm, k), lambda l, i: (i, 0)),
                  pl.BlockSpec((None, k, n), lambda l, i: (l, 0, 0))],
        out_specs=[pl.BlockSpec((None, tm, n), lambda l, i: (l, i, 0)),
                   pl.BlockSpec((None, nbat, mem_len * n // LANES, LANES), lambda l, i: (l, i, 0, 0))],
        out_shape=[jax.ShapeDtypeStruct((nl, m, n), F32),
                   jax.ShapeDtypeStruct((nl, m // mem_len, mem_len * n // LANES, LANES), F32)],
        compiler_params=_params("parallel", "parallel"),
        name="mem_kv_proj",
    )(a, w)


def _mm_rows(a, w, name, row_off, n_rows):
    k = a.shape[1]
    n = w.shape[1]
    tm = ROW_TILE
    return pl.pallas_call(
        _mm_kernel,
        grid=(n_rows // tm,),
        in_specs=[_rows(tm, k, row_off // tm), _full((k, n))],
        out_specs=_rows(tm, n),
        out_shape=jax.ShapeDtypeStruct((n_rows, n), F32),
        compiler_params=_params("parallel"),
        name=name,
    )(a, w)


def _mm_ln_kernel(a_ref, w_ref, bias_ref, x_ref, g_ref, b_ref, *rest, with_logits, n_alias):
    h = _dot(a_ref[...].astype(BF16), w_ref[...]) + bias_ref[...]
    y = _layer_norm(ALPHA * x_ref[...] + h, g_ref[...], b_ref[...])
    if with_logits:
        wr_hi_ref, wr_lo_ref = rest[:2]
        o_ref, lg_ref = rest[2 + n_alias:]
        lg_ref[...] = _dot_hi(y, wr_hi_ref[...], wr_lo_ref[...])
    else:
        (o_ref,) = rest[n_alias:]
    o_ref[...] = y


def _mm_ln(a, w, bias, x, g, b, name, router=None, row_off=0, fill=None):
    m, k = a.shape
    t, d = x.shape
    tm = ROW_TILE
    off = row_off // tm
    in_specs = [_rows(tm, k), _full((k, d)), _full((1, d)), _rows(tm, d, off), _full((1, d)), _full((1, d))]
    args = [a, w, bias, x, g, b]
    out_specs = [_rows(tm, d, off)]
    out_shape = [jax.ShapeDtypeStruct((t, d), F32)]
    if router is not None:
        in_specs += [_full((d, LANES)), _full((d, LANES))]
        args += list(router)
        out_specs.append(_rows(tm, LANES, off))
        out_shape.append(jax.ShapeDtypeStruct((t, LANES), F32))
    aliases = {}
    for n, arr in enumerate(fill or ()):
        in_specs.append(pl.BlockSpec(memory_space=pl.ANY))
        args.append(arr)
        aliases[len(args) - 1] = n
    res = pl.pallas_call(
        functools.partial(_mm_ln_kernel, with_logits=router is not None, n_alias=len(aliases)),
        grid=(m // tm,),
        in_specs=in_specs,
        out_specs=out_specs,
        out_shape=out_shape,
        input_output_aliases=aliases,
        compiler_params=_params("parallel"),
        name=name,
    )(*args)
    return res if router is not None else res[0]


def _gmlp_kernel(x_ref, w_in_ref, b_in_ref, lng_ref, lnb_ref, wc_ref, bc_ref, w_out_ref, b_out_ref,
                 g_ref, b_ref, *rest, emit_v, n_chunks):
    if emit_v:
        _alias_ref, o_ref, v_ref, vn_ref = rest
    else:
        o_ref, vn_ref = rest
    half = w_out_ref.shape[0]
    gw = half // A_GROUPS
    x = x_ref[...]
    xb = x.astype(BF16)
    v = _gelu(_dot(xb, w_in_ref[:, half:]) + b_in_ref[:, half:])
    vn = _layer_norm(v, lng_ref[...], lnb_ref[...])
    vn_ref[...] = vn
    if emit_v:
        v_ref[...] = vn
    acc = jnp.zeros(x.shape, F32)
    for grp in range(A_GROUPS):
        cols = slice(grp * gw, (grp + 1) * gw)
        u = _gelu(_dot(xb, w_in_ref[:, cols]) + b_in_ref[:, cols])
        mixed = []
        for c in range(n_chunks):
            vc = vn_ref[c * A_CHUNK:(c + 1) * A_CHUNK, cols].astype(BF16)
            mixed.append(_dot(wc_ref[grp], vc) + bc_ref[grp])
        mixed = mixed[0] if n_chunks == 1 else jnp.concatenate(mixed, axis=0)
        acc = acc + _dot((u * mixed).astype(BF16), w_out_ref[cols, :])
    h = acc + b_out_ref[...]
    o_ref[...] = _layer_norm(ALPHA * x + h, g_ref[...], b_ref[...])


GMLP_CHUNKS = 4


def _gmlp(x, row_off, t, w_in, b_in, lng, lnb, wc, bc, w_out, b_out, g, b, alias=None):
    n_rows, d = x.shape
    ffn = w_in.shape[1]
    half = ffn // 2
    tm = GMLP_CHUNKS * A_CHUNK
    emit_v = alias is not None
    off = row_off // tm
    once = lambda shape: pl.BlockSpec(shape, lambda *_: (0,) * len(shape), pipeline_mode=pl.Buffered(1))
    in_specs = [_rows(tm, d), once((d, ffn)), _full((1, ffn)), _full((1, half)), _full((1, half)),
                _full((A_GROUPS, A_CHUNK, A_CHUNK)), _full((A_GROUPS, A_CHUNK, 1)), once((half, d)),
                _full((1, d)), _full((1, d)), _full((1, d))]
    args = [x, w_in, b_in, lng, lnb, wc, bc, w_out, b_out, g, b]
    out_specs = _rows(tm, d, off)
    out_shape = jax.ShapeDtypeStruct((t, d), F32)
    aliases = {}
    if emit_v:
        in_specs.append(pl.BlockSpec(memory_space=pl.ANY))
        args.append(alias)
        aliases = {len(args) - 1: 0}
        out_specs = [out_specs, _rows(tm, half)]
        out_shape = [out_shape, jax.ShapeDtypeStruct((n_rows, half), F32)]
    return pl.pallas_call(
        functools.partial(_gmlp_kernel, emit_v=emit_v, n_chunks=tm // A_CHUNK),
        grid=(n_rows // tm,),
        in_specs=in_specs,
        out_specs=out_specs,
        out_shape=out_shape,
        scratch_shapes=[pltpu.VMEM((tm, half), F32)],
        input_output_aliases=aliases,
        compiler_params=_params("parallel"),
        name="gmlp_sample" if emit_v else "gmlp_prompt",
    )(*args)


def _gla_proj_kernel(x_ref, w_ref, wlow_ref, g2_hi_ref, g2_lo_ref, bg_ref,
                     q_ref, la_ref, k_ref, v_ref, r_ref, *, dk, dv, q_scale):
    xb = x_ref[...].astype(BF16)
    q_ref[...] = _dot(xb, w_ref[:, 0:dk]) * q_scale
    k_ref[...] = _dot(xb, w_ref[:, dk:2 * dk])
    v_ref[...] = _dot(xb, w_ref[:, 2 * dk:2 * dk + dv])
    r_ref[...] = _dot(xb, w_ref[:, 2 * dk + dv:2 * dk + 2 * dv])
    g_low = _dot(xb, wlow_ref[...])
    pre = _dot_hi(g_low, g2_hi_ref[...], g2_lo_ref[...]) + bg_ref[...]
    la_ref[...] = _log_sigmoid(pre) * (1.0 / B_GATE_TAU)


def _gla_proj(x, weights, dk, dv, row_off, n_rows):
    d = x.shape[1]
    tm = ROW_TILE
    shapes = [dk, dk, dk, dv, dv]
    return pl.pallas_call(
        functools.partial(_gla_proj_kernel, dk=dk, dv=dv, q_scale=(dk // B_HEADS) ** -0.5),
        grid=(n_rows // tm,),
        in_specs=[_rows(tm, d, row_off // tm)] + [_full(w.shape) for w in weights],
        out_specs=[_rows(tm, n) for n in shapes],
        out_shape=[jax.ShapeDtypeStruct((n_rows, n), F32) for n in shapes],
        compiler_params=_params("parallel"),
        name="gla_proj",
    )(x, *weights)


def _hgrn_proj_kernel(x_ref, w_ref, lb_ref, q_ref, lf_ref, k_ref, v_ref, gt_ref, *, d, q_scale):
    xb = x_ref[...].astype(BF16)
    lb = lb_ref[...]
    q_ref[...] = _silu(_dot(xb, w_ref[:, 0:d])) * q_scale
    f = _dot(xb, w_ref[:, d:2 * d])
    lf_ref[...] = jnp.log(lb + (1.0 - lb) * _sigmoid(f))
    k_ref[...] = (1.0 - lb) * _sigmoid(-f)
    v_ref[...] = _dot(xb, w_ref[:, 2 * d:3 * d])
    gt_ref[...] = _dot(xb, w_ref[:, 3 * d:4 * d])


def _hgrn_proj(x, weights, row_off, n_rows):
    d = x.shape[1]
    tm = ROW_TILE
    return pl.pallas_call(
        functools.partial(_hgrn_proj_kernel, d=d, q_scale=C_EXPAND ** -0.5),
        grid=(n_rows // tm,),
        in_specs=[_rows(tm, d, row_off // tm)] + [_full(w.shape) for w in weights],
        out_specs=[_rows(tm, d)] * 5,
        out_shape=[jax.ShapeDtypeStruct((n_rows, d), F32)] * 5,
        compiler_params=_params("parallel"),
        name="hgrn_proj",
    )(x, *weights)


def _cumsum_rows(x, chunk):
    pos = lax.broadcasted_iota(jnp.int32, x.shape, 0) & (chunk - 1)
    step = 1
    while step < chunk:
        x = x + jnp.where(pos >= step, pltpu.roll(x, step, axis=0), 0.0)
        step *= 2
    return x


def _chunk_rows(x, row, chunk, n_chunks):
    parts = [jnp.broadcast_to(x[c * chunk + row:c * chunk + row + 1, :], (chunk, x.shape[1]))
             for c in range(n_chunks)]
    return parts[0] if n_chunks == 1 else jnp.concatenate(parts, axis=0)


def _linrec_mask(tl, chunk):
    r_id = lax.broadcasted_iota(jnp.int32, (tl, tl), 0)
    c_id = lax.broadcasted_iota(jnp.int32, (tl, tl), 1)
    return (r_id >= c_id) & ((r_id // chunk) == (c_id // chunk))


def _linrec_heads(q_ref, k_ref, g_ref, v_ref, gate_ref, ng_ref, o_ref, st_ref, rs, *, heads, dk, dv, chunk, n_chunks):
    mask = _linrec_mask(chunk * n_chunks, chunk)
    for h in range(heads):
        ks = slice(h * dk, (h + 1) * dk)
        vs = slice(h * dv, (h + 1) * dv)
        bcum = _cumsum_rows(g_ref[rs, ks], chunk)
        b_mid = _chunk_rows(bcum, chunk // 2 - 1, chunk, n_chunks)
        b_end = _chunk_rows(bcum, chunk - 1, chunk, n_chunks)
        q = q_ref[rs, ks]
        k = k_ref[rs, ks]
        vb = v_ref[rs, vs].astype(BF16)
        q_in = (q * jnp.exp(bcum)).astype(BF16)
        q_a = (q * jnp.exp(bcum - b_mid)).astype(BF16)
        k_a = (k * jnp.exp(b_mid - bcum)).astype(BF16)
        k_end = k * jnp.exp(b_end - bcum)
        a = jnp.where(mask, _dot_nt(q_a, k_a), 0.0).astype(BF16)
        o_intra = _dot(a, vb)
        st = st_ref[h]
        outs = []
        for c in range(n_chunks):
            cr = slice(c * chunk, (c + 1) * chunk)
            outs.append(o_intra[cr] + _dot(q_in[cr], st.astype(BF16)))
            decay = jnp.exp(bcum[cr].T[:, chunk - 1:chunk])
            st = st * decay + _dot(k_end[cr].T.astype(BF16), vb[cr])
        st_ref[h] = st
        o = outs[0] if n_chunks == 1 else jnp.concatenate(outs, axis=0)
        o = o * lax.rsqrt(jnp.mean(o * o, axis=-1, keepdims=True) + LN_EPS) * ng_ref[:, vs]
        o_ref[rs, vs] = o * _silu(gate_ref[rs, vs])


def _linrec_kernel(q_ref, k_ref, g_ref, v_ref, gate_ref, ng_ref, s0_ref, o_ref, sout_ref, st_ref, *, nb, tl, **dims):
    li = pl.program_id(1)
    for n in range(nb):
        @pl.when(li == 0)
        def _():
            st_ref[...] = s0_ref[n]

        _linrec_heads(q_ref, k_ref, g_ref, v_ref, gate_ref, ng_ref, o_ref, st_ref, slice(n * tl, (n + 1) * tl),
                      **dims)
        sout_ref[n] = st_ref[...]


LINREC_CHUNK = 2 * LIN_CHUNK
LINREC_SAMPLE_NB = 4


def _linrec_tiling(seq):
    chunk = LINREC_CHUNK if seq % LINREC_CHUNK == 0 else seq
    tl = min(seq, 4 * chunk)
    return chunk, tl, seq // tl


def _linrec(q, k, g, v, gate, ng, s0, *, n_batch, seq, heads):
    t, hk = q.shape
    hv = v.shape[1]
    dk, dv = hk // heads, hv // heads
    chunk, tl, n_l = _linrec_tiling(seq)
    nb = LINREC_SAMPLE_NB if n_l == 1 else 1

    def rows(n):
        return pl.BlockSpec((nb * tl, n), lambda b, l: (b * n_l + l, 0))

    state_spec = pl.BlockSpec((nb, heads, dk, dv), lambda b, l: (b, 0, 0, 0))
    return pl.pallas_call(
        functools.partial(_linrec_kernel, nb=nb, tl=tl, heads=heads, dk=dk, dv=dv, chunk=chunk,
                          n_chunks=tl // chunk),
        grid=(n_batch // nb, n_l),
        in_specs=[rows(hk), rows(hk), rows(hk), rows(hv), rows(hv), pl.BlockSpec((1, hv), lambda b, l: (0, 0)),
                  state_spec],
        out_specs=[rows(hv), state_spec],
        out_shape=[jax.ShapeDtypeStruct((t, hv), F32), jax.ShapeDtypeStruct((n_batch, heads, dk, dv), F32)],
        scratch_shapes=[pltpu.VMEM((heads, dk, dv), F32)],
        compiler_params=_params("parallel", "arbitrary"),
        name="linrec_h%d" % heads,
    )(q, k, g, v, gate, ng, s0)


def _linrec_block_kernel(*refs, proj, n_proj_w, **dims):
    x_ref = refs[0]
    proj_w = refs[1:1 + n_proj_w]
    ng_ref, s0_ref, wo_ref, g_ref, b_ref, o_ref, sout_ref = refs[1 + n_proj_w:8 + n_proj_w]
    q_s, g_s, k_s, v_s, gate_s, o_s, st_ref = refs[8 + n_proj_w:]

    @pl.when(pl.program_id(1) == 0)
    def _():
        st_ref[...] = s0_ref[0]

    proj(x_ref, *proj_w, q_s, g_s, k_s, v_s, gate_s)
    _linrec_heads(q_s, k_s, g_s, v_s, gate_s, ng_ref, o_s, st_ref, slice(0, x_ref.shape[0]), **dims)
    h = _dot(o_s[...].astype(BF16), wo_ref[...])
    o_ref[...] = _layer_norm(ALPHA * x_ref[...] + h, g_ref[...], b_ref[...])
    sout_ref[0] = st_ref[...]


def _linrec_block(x, proj, proj_w, ng, s0, w_out, g, b, *, n_batch, seq, heads, hk, hv, name):
    t, d = x.shape
    dk, dv = hk // heads, hv // heads
    chunk, tl, n_l = _linrec_tiling(seq)
    rows = pl.BlockSpec((tl, d), lambda bi, l: (bi * n_l + l, 0))
    full = lambda shape: pl.BlockSpec(shape, lambda bi, l: (0,) * len(shape))
    state_spec = pl.BlockSpec((1, heads, dk, dv), lambda bi, l: (bi, 0, 0, 0))
    return pl.pallas_call(
        functools.partial(_linrec_block_kernel, proj=proj, n_proj_w=len(proj_w), heads=heads, dk=dk, dv=dv,
                          chunk=chunk, n_chunks=tl // chunk),
        grid=(n_batch, n_l),
        in_specs=[rows] + [full(w.shape) for w in proj_w] + [full((1, hv)), state_spec, full(w_out.shape),
                                                            full((1, d)), full((1, d))],
        out_specs=[rows, state_spec],
        out_shape=[jax.ShapeDtypeStruct((t, d), F32), jax.ShapeDtypeStruct((n_batch, heads, dk, dv), F32)],
        scratch_shapes=[pltpu.VMEM((tl, hk), F32), pltpu.VMEM((tl, hk), F32), pltpu.VMEM((tl, hk), F32),
                        pltpu.VMEM((tl, hv), F32), pltpu.VMEM((tl, hv), F32), pltpu.VMEM((tl, hv), F32),
                        pltpu.VMEM((heads, dk, dv), F32)],
        compiler_params=_params("parallel", "arbitrary"),
        name=name,
    )(x, *proj_w, ng, s0, w_out, g, b)


def _glu_kernel(x_ref, w_ref, b_ref, o_ref, *, d):
    xb = x_ref[...].astype(BF16)
    a = _dot(xb, w_ref[:, 0:d]) + b_ref[:, 0:d]
    gate = _dot(xb, w_ref[:, d:2 * d]) + b_ref[:, d:2 * d]
    o_ref[...] = a * _sigmoid(gate)


def _glu(x, w, b, row_off, n_rows):
    d = x.shape[1]
    tm = ROW_TILE
    return pl.pallas_call(
        functools.partial(_glu_kernel, d=d),
        grid=(n_rows // tm,),
        in_specs=[_rows(tm, d, row_off // tm), _full(w.shape), _full(b.shape)],
        out_specs=_rows(tm, d),
        out_shape=jax.ShapeDtypeStruct((n_rows, d), F32),
        compiler_params=_params("parallel"),
        name="conf_glu",
    )(x, w, b)


CONV_PAD = 32


CONV_LEAD = CONV_PAD - D_BUF


def _conv_window_step(buf_ref, h, state_of, wdw_ref, bdw_ref, lg_ref, lb_ref, conv_ref, first, tl):
    d = h.shape[1]
    rb = min(tl, 64)
    cw = LANES
    buf_ref[CONV_PAD + tl:CONV_PAD + tl + SUBLANES, :] = jnp.zeros((SUBLANES, d), F32)

    @pl.when(first)
    def _():
        buf_ref[CONV_LEAD:CONV_PAD, :] = state_of()

    buf_ref[CONV_PAD:CONV_PAD + tl, :] = h
    for r0 in range(0, tl, rb):
        for c0 in range(0, d, cw):
            cols = slice(c0, c0 + cw)
            acc = jnp.zeros((rb, cw), F32)
            for s in range(SUBLANES):
                part = None
                for a in range((CONV_PAD + SUBLANES) // SUBLANES):
                    j = SUBLANES * a + s - CONV_LEAD
                    if 0 <= j < D_CONV_W:
                        rows = slice(r0 + SUBLANES * a, r0 + SUBLANES * a + rb + SUBLANES)
                        term = wdw_ref[j:j + 1, cols] * buf_ref[rows, cols]
                        part = term if part is None else part + term
                acc = acc + part[s:s + rb, :]
            conv_ref[r0:r0 + rb, cols] = acc + bdw_ref[:, cols]
    y = _silu(_layer_norm(conv_ref[...], lg_ref[...], lb_ref[...]))
    new_state = buf_ref[tl + CONV_LEAD:tl + CONV_PAD, :]
    buf_ref[0:CONV_PAD, :] = buf_ref[tl:tl + CONV_PAD, :]
    return y, new_state


def _conv_kernel(h_ref, st_ref, wdw_ref, bdw_ref, lg_ref, lb_ref, o_ref, sout_ref, buf_ref, conv_ref, *, nb, tl):
    first = pl.program_id(1) == 0
    for n in range(nb):
        y, new_state = _conv_window_step(buf_ref, h_ref[n * tl:(n + 1) * tl, :], lambda: st_ref[n], wdw_ref,
                                         bdw_ref, lg_ref, lb_ref, conv_ref, first, tl)
        o_ref[n * tl:(n + 1) * tl, :] = y
        sout_ref[n] = new_state


CONV_TILE = 256


def _conv(h, state, wdw, bdw, lg, lb, *, n_batch, seq):
    t, d = h.shape
    tl = min(seq, CONV_TILE)
    n_l = seq // tl
    nb = 8 if n_l == 1 else 1
    rows = pl.BlockSpec((nb * tl, d), lambda b, l: (b * n_l + l, 0))
    state_spec = pl.BlockSpec((nb, D_BUF, d), lambda b, l: (b, 0, 0))
    return pl.pallas_call(
        functools.partial(_conv_kernel, nb=nb, tl=tl),
        grid=(n_batch // nb, n_l),
        in_specs=[rows, state_spec, _full(wdw.shape), _full((1, d)), _full((1, d)), _full((1, d))],
        out_specs=[rows, state_spec],
        out_shape=[jax.ShapeDtypeStruct((t, d), F32), jax.ShapeDtypeStruct((n_batch, D_BUF, d), F32)],
        scratch_shapes=[pltpu.VMEM((CONV_PAD + tl + SUBLANES, d), F32), pltpu.VMEM((tl, d), F32)],
        compiler_params=_params("parallel", "arbitrary"),
        name="conf_conv",
    )(h, state, wdw, bdw, lg, lb)


def _conf_block_kernel(x_ref, win_ref, bin_ref, st_ref, wdw_ref, bdw_ref, lg_ref, lb_ref, wo_ref, bo_ref,
                       g_ref, b_ref, o_ref, sout_ref, buf_ref, conv_ref):
    x = x_ref[...]
    tl, d = x.shape
    xb = x.astype(BF16)
    h = (_dot(xb, win_ref[:, 0:d]) + bin_ref[:, 0:d]) * _sigmoid(_dot(xb, win_ref[:, d:2 * d]) + bin_ref[:, d:2 * d])
    y, new_state = _conv_window_step(buf_ref, h, lambda: st_ref[0], wdw_ref, bdw_ref, lg_ref, lb_ref, conv_ref,
                                     pl.program_id(1) == 0, tl)
    out = _dot(y.astype(BF16), wo_ref[...]) + bo_ref[...]
    o_ref[...] = _layer_norm(ALPHA * x + out, g_ref[...], b_ref[...])
    sout_ref[0] = new_state


def _conf_block(x, w_in, b_in, state, wdw, bdw, lg, lb, w_out, b_out, g, b, *, n_batch, seq):
    t, d = x.shape
    tl = CONV_TILE
    n_l = seq // tl
    rows = pl.BlockSpec((tl, d), lambda bi, l: (bi * n_l + l, 0))
    full = lambda shape: pl.BlockSpec(shape, lambda bi, l: (0,) * len(shape))
    state_spec = pl.BlockSpec((1, D_BUF, d), lambda bi, l: (bi, 0, 0))
    return pl.pallas_call(
        _conf_block_kernel,
        grid=(n_batch, n_l),
        in_specs=[rows, full(w_in.shape), full(b_in.shape), state_spec, full(wdw.shape), full((1, d)), full((1, d)),
                  full((1, d)), full(w_out.shape), full((1, d)), full((1, d)), full((1, d))],
        out_specs=[rows, state_spec],
        out_shape=[jax.ShapeDtypeStruct((t, d), F32), jax.ShapeDtypeStruct((n_batch, D_BUF, d), F32)],
        scratch_shapes=[pltpu.VMEM((CONV_PAD + tl + SUBLANES, d), F32), pltpu.VMEM((tl, d), F32)],
        compiler_params=_params("parallel", "arbitrary"),
        name="conf_block_prompt",
    )(x, w_in, b_in, state, wdw, bdw, lg, lb, w_out, b_out, g, b)


def _attn_heads(q, k_of, v_of, hd):
    outs = []
    for h in range(MEM_HEADS):
        hs = slice(h * hd, (h + 1) * hd)
        s = _dot_nt(q[:, hs].astype(BF16), k_of(hs).astype(BF16)) * (hd ** -0.5)
        p = jnp.exp(s - jnp.max(s, axis=-1, keepdims=True))
        denom = jnp.sum(p, axis=-1, keepdims=True)
        outs.append(_dot(p.astype(BF16), v_of(hs).astype(BF16)) / denom)
    return outs


def _attn_block_kernel(x_ref, wq_ref, k_ref, v_ref, wo_ref, g_ref, b_ref, wr_hi_ref, wr_lo_ref, o_ref, lg_ref):
    x = x_ref[...]
    hd = x.shape[1] // MEM_HEADS
    q = _dot(x.astype(BF16), wq_ref[...])
    att = jnp.concatenate(_attn_heads(q, lambda hs: k_ref[:, hs], lambda hs: v_ref[:, hs], hd), axis=1)
    y = _layer_norm(ALPHA * x + _dot(att.astype(BF16), wo_ref[...]), g_ref[...], b_ref[...])
    lg_ref[...] = _dot_hi(y, wr_hi_ref[...], wr_lo_ref[...])
    o_ref[...] = y


def _attn_block_prompt(x, w_q, mem_k, mem_v, w_o, g, b, router, layer, n_batch, seq):
    t, d = x.shape
    m = mem_k.shape[1] // n_batch
    tl = ROW_TILE
    n_l = seq // tl
    rows = lambda n: pl.BlockSpec((tl, n), lambda bi, l: (bi * n_l + l, 0))
    kv = pl.BlockSpec((None, m, d), lambda bi, l: (layer, bi, 0))
    full = lambda shape: pl.BlockSpec(shape, lambda bi, l: (0,) * len(shape))
    return pl.pallas_call(
        _attn_block_kernel,
        grid=(n_batch, n_l),
        in_specs=[rows(d), full((d, d)), kv, kv, full((d, d)), full((1, d)), full((1, d)),
                  full((d, LANES)), full((d, LANES))],
        out_specs=[rows(d), rows(LANES)],
        out_shape=[jax.ShapeDtypeStruct((t, d), F32), jax.ShapeDtypeStruct((t, LANES), F32)],
        compiler_params=_params("parallel", "parallel"),
        name="attn_block_prompt",
    )(x, w_q, mem_k, mem_v, w_o, g, b, *router)


ATTN_SAMPLE_NB = 4


def _attn_sample_kernel(q_ref, k_ref, v_ref, o_ref, *, nb, seq):
    hd = q_ref.shape[-1] // MEM_HEADS
    n_lt = hd // LANES
    m = k_ref.shape[1] // (MEM_HEADS * n_lt)

    def head(ref, n, h):
        tiles = [ref[n, pl.ds(lt * MEM_HEADS + h, m, stride=MEM_HEADS * n_lt), :] for lt in range(n_lt)]
        return jnp.concatenate(tiles, axis=1).astype(BF16)

    pairs = [(n, h) for n in range(nb) for h in range(MEM_HEADS)]
    s = jnp.concatenate(
        [_dot_nt(q_ref[n * seq:(n + 1) * seq, h * hd:(h + 1) * hd].astype(BF16), head(k_ref, n, h))
         for n, h in pairs], axis=0) * (hd ** -0.5)
    p = jnp.exp(s - jnp.max(s, axis=-1, keepdims=True))
    inv = 1.0 / jnp.sum(p, axis=-1, keepdims=True)
    for idx, (n, h) in enumerate(pairs):
        rs = slice(idx * seq, (idx + 1) * seq)
        o = _dot(p[rs].astype(BF16), head(v_ref, n, h)) * inv[rs]
        o_ref[n * seq:(n + 1) * seq, h * hd:(h + 1) * hd] = o


def _cache_rows(cache):
    nl, nbat, m, heads, hd = cache.shape
    c = cache.reshape(nl, nbat, m, heads, hd // LANES, LANES)
    return c.transpose(0, 1, 2, 4, 3, 5).reshape(nl, nbat, m * (hd // LANES) * heads, LANES)


def _cache_unrows(flat, heads):
    nl, nbat, rows, _ = flat.shape
    n_lt = D_MODEL // (heads * LANES)
    m = rows // (heads * n_lt)
    c = flat.reshape(nl, nbat, m, n_lt, heads, LANES).transpose(0, 1, 2, 4, 3, 5)
    return c.reshape(nl, nbat, m, heads, n_lt * LANES)


def _attn_sample(q, cache_k, cache_v, layer, n_batch, seq):
    t, d = q.shape
    nb = ATTN_SAMPLE_NB
    rows = pl.BlockSpec((nb * seq, d), lambda i: (i, 0))
    kv = pl.BlockSpec((None, nb) + cache_k.shape[2:], lambda i: (layer, i, 0, 0))
    return pl.pallas_call(
        functools.partial(_attn_sample_kernel, nb=nb, seq=seq),
        grid=(n_batch // nb,),
        in_specs=[rows, kv, kv],
        out_specs=rows,
        out_shape=jax.ShapeDtypeStruct((t, d), F32),
        compiler_params=_params("parallel"),
        name="attn_sample",
    )(q, cache_k, cache_v)


def _route_kernel(lg_ref, bias_ref, ltri_ref, route_ref, counts_ref, carry_ref):
    i = pl.program_id(0)

    @pl.when(i == 0)
    def _():
        carry_ref[...] = jnp.zeros(carry_ref.shape, F32)

    z = lg_ref[...] + bias_ref[...]
    lane = lax.broadcasted_iota(jnp.int32, z.shape, 1).astype(F32)
    neg = -jnp.inf
    far = float(LANES)

    def first_max(mask):
        vmax = jnp.max(jnp.where(mask, z, neg), axis=-1, keepdims=True)
        idx = jnp.min(jnp.where(mask & (z == vmax), lane, far), axis=-1, keepdims=True)
        return vmax, idx

    gmask = lane < float(MOE_GROUPS)
    gmax, gidx = first_max(gmask)
    gsum = jnp.sum(jnp.where(gmask, jnp.exp(z - gmax), 0.0), axis=-1, keepdims=True)
    g_w = 1.0 / gsum
    lo = float(ROUTE_LANE0) + float(MOE_PER_GROUP) * gidx
    emask = (lane >= lo) & (lane < lo + float(MOE_PER_GROUP))
    v1, i1 = first_max(emask)
    v2, i2 = first_max(emask & (lane != i1))
    tt = jnp.exp(v2 - v1)
    w0 = g_w / (1.0 + tt)
    w1 = g_w * tt / (1.0 + tt)
    sel1 = lane == i1
    sel2 = lane == i2
    onehot = jnp.where(sel1 | sel2, 1.0, 0.0)
    before = _dot(ltri_ref[...], onehot.astype(BF16)) + carry_ref[...]
    rank0 = jnp.sum(jnp.where(sel1, before, 0.0), axis=-1, keepdims=True)
    rank1 = jnp.sum(jnp.where(sel2, before, 0.0), axis=-1, keepdims=True)
    carry = carry_ref[...] + jnp.sum(onehot, axis=0, keepdims=True)
    carry_ref[...] = carry
    counts_ref[...] = carry
    e_off = float(ROUTE_LANE0)
    out = jnp.zeros(z.shape, F32)
    for ln, val in enumerate((i1 - e_off, i2 - e_off, w0, w1, rank0, rank1)):
        out = jnp.where(lane == float(ln), val, out)
    route_ref[...] = out


def _route(logits, bias, ltri):
    t = logits.shape[0]
    tm = ROW_TILE
    return pl.pallas_call(
        _route_kernel,
        grid=(t // tm,),
        in_specs=[_rows(tm, LANES), _full((1, LANES)), _full((tm, tm))],
        out_specs=[_rows(tm, LANES), _full((1, LANES))],
        out_shape=[jax.ShapeDtypeStruct((t, LANES), F32), jax.ShapeDtypeStruct((1, LANES), F32)],
        scratch_shapes=[pltpu.VMEM((1, LANES), F32)],
        compiler_params=_params("arbitrary"),
        name="moe_route",
    )(logits, bias, ltri)


def _row_copy_wait(src_rows, dst_rows, sem):
    pltpu.make_async_copy(src_rows, dst_rows, sem).wait()


def _dispatch_kernel(tail_ref, dest_ref, x_ref, xs_ref, zeros_ref, sem, *, tm):
    @pl.when(pl.program_id(0) == 0)
    def _():
        zeros_ref[...] = jnp.zeros(zeros_ref.shape, F32)
        bm = zeros_ref.shape[0]
        tails = [xs_ref.at[pl.ds(pl.multiple_of(tail_ref[e], bm), bm)] for e in range(MOE_EXPERTS)]
        for dst in tails:
            pltpu.make_async_copy(zeros_ref, dst, sem).start()
        for dst in tails:
            pltpu.make_async_copy(zeros_ref, dst, sem).wait()

    def issue(t, carry):
        for j in range(2):
            d = dest_ref[2 * t + j]
            pltpu.make_async_copy(x_ref.at[pl.ds(t, 1)], xs_ref.at[pl.ds(d, 1)], sem).start(priority=j)
        return carry

    lax.fori_loop(0, tm, issue, 0, unroll=8)
    for _ in range(2):
        _row_copy_wait(x_ref, xs_ref.at[pl.ds(0, tm)], sem)


def _dispatch(tail_rows, dest_flat, x, n_rows):
    t, d = x.shape
    tm = ROW_TILE
    grid_spec = pltpu.PrefetchScalarGridSpec(
        num_scalar_prefetch=1,
        grid=(t // tm,),
        in_specs=[pl.BlockSpec((2 * tm,), lambda i, tail: (i,), memory_space=pltpu.SMEM),
                  pl.BlockSpec((tm, d), lambda i, tail: (i, 0))],
        out_specs=pl.BlockSpec(memory_space=pl.ANY),
        scratch_shapes=[pltpu.VMEM((MOE_BLOCK_ROWS, d), F32), pltpu.SemaphoreType.DMA(())],
    )
    return pl.pallas_call(
        functools.partial(_dispatch_kernel, tm=tm),
        grid_spec=grid_spec,
        out_shape=jax.ShapeDtypeStruct((n_rows, d), F32),
        compiler_params=_params("arbitrary"),
        name="moe_dispatch",
    )(tail_rows, dest_flat, x)


def _expert_kernel(blk_e_ref, nused_ref, xs_ref, wgu_ref, wd_ref, y_ref, wgu_bf, wd_bf):
    i = pl.program_id(0)
    prev = blk_e_ref[jnp.maximum(i - 1, 0)]
    new_expert = (i == 0) | (blk_e_ref[i] != prev)

    @pl.when(new_expert)
    def _():
        wgu_bf[...] = wgu_ref[...].astype(BF16)
        wd_bf[...] = wd_ref[...].astype(BF16)

    @pl.when(i < nused_ref[0])
    def _():
        hid = wd_bf.shape[0]
        xb = xs_ref[...].astype(BF16)
        gate = _dot(xb, wgu_bf[:, 0:hid])
        up = _dot(xb, wgu_bf[:, hid:2 * hid])
        y_ref[...] = _dot((_silu(gate) * up).astype(BF16), wd_bf[...])

    @pl.when(i >= nused_ref[0])
    def _():
        y_ref[...] = jnp.zeros(y_ref.shape, F32)


def _experts(blk_e, nused, xs, w_gate_up, w_down, layer):
    nr, d = xs.shape
    bm = MOE_BLOCK_ROWS
    hid2 = w_gate_up.shape[-1]
    hid = w_down.shape[-2]
    grid_spec = pltpu.PrefetchScalarGridSpec(
        num_scalar_prefetch=2,
        grid=(nr // bm,),
        in_specs=[pl.BlockSpec((bm, d), lambda i, be, nu: (jnp.minimum(i, nu[0] - 1), 0)),
                  pl.BlockSpec((None, None, d, hid2), lambda i, be, nu: (layer, be[i], 0, 0)),
                  pl.BlockSpec((None, None, hid, d), lambda i, be, nu: (layer, be[i], 0, 0))],
        out_specs=pl.BlockSpec((bm, d), lambda i, be, nu: (i, 0)),
        scratch_shapes=[pltpu.VMEM((d, hid2), BF16), pltpu.VMEM((hid, d), BF16)],
    )
    return pl.pallas_call(
        _expert_kernel,
        grid_spec=grid_spec,
        out_shape=jax.ShapeDtypeStruct((nr, d), F32),
        compiler_params=_params("arbitrary"),
        name="moe_experts",
    )(blk_e, nused, xs, w_gate_up, w_down)


def _combine_kernel(dest_ref, dest_next_ref, x_ref, route_ref, g_ref, b_ref, y_hbm, *rest, tm, n_first):
    if n_first is None:
        o_ref, ybuf, sems = rest
    else:
        o_ref, o2_ref, ybuf, sems = rest
    i = pl.program_id(0)
    n_tiles = pl.num_programs(0)

    def gather(d_ref, slot):
        def issue(t, carry):
            for j in range(2):
                d = d_ref[2 * t + j]
                pltpu.make_async_copy(y_hbm.at[pl.ds(d, 1)], ybuf.at[slot, j, pl.ds(t, 1)],
                                      sems.at[slot]).start(priority=j)
            return carry

        lax.fori_loop(0, tm, issue, 0, unroll=8)

    @pl.when(i == 0)
    def _():
        gather(dest_ref, 0)

    @pl.when(i + 1 < n_tiles)
    def _():
        gather(dest_next_ref, lax.rem(i + 1, 2))

    slot = lax.rem(i, 2)
    for j in range(2):
        _row_copy_wait(y_hbm.at[pl.ds(0, tm)], ybuf.at[slot, j], sems.at[slot])
    moe = route_ref[:, 2:3] * ybuf[slot, 0] + route_ref[:, 3:4] * ybuf[slot, 1]
    y = _layer_norm(ALPHA * x_ref[...] + moe, g_ref[...], b_ref[...])
    if n_first is None:
        o_ref[...] = y
    else:
        @pl.when(i < n_first)
        def _():
            o_ref[...] = y

        @pl.when(i >= n_first)
        def _():
            o2_ref[...] = y


def _combine(dest_flat, x, route, g, b, yb, split_rows=None):
    t, d = x.shape
    tm = ROW_TILE
    n_tiles = t // tm
    out_specs = _rows(tm, d)
    out_shape = jax.ShapeDtypeStruct((t, d), F32)
    n_first = None
    if split_rows is not None:
        n_first = split_rows // tm
        out_specs = [pl.BlockSpec((tm, d), lambda i: (jnp.minimum(i, n_first - 1), 0)),
                     pl.BlockSpec((tm, d), lambda i: (jnp.maximum(i - n_first, 0), 0))]
        out_shape = [jax.ShapeDtypeStruct((split_rows, d), F32), jax.ShapeDtypeStruct((t - split_rows, d), F32)]
    return pl.pallas_call(
        functools.partial(_combine_kernel, tm=tm, n_first=n_first),
        grid=(n_tiles,),
        in_specs=[pl.BlockSpec((2 * tm,), lambda i: (i,), memory_space=pltpu.SMEM),
                  pl.BlockSpec((2 * tm,), lambda i: (jnp.minimum(i + 1, n_tiles - 1),), memory_space=pltpu.SMEM),
                  _rows(tm, d), _rows(tm, LANES), _full((1, d)), _full((1, d)),
                  pl.BlockSpec(memory_space=pl.ANY)],
        out_specs=out_specs,
        out_shape=out_shape,
        scratch_shapes=[pltpu.VMEM((2, 2, tm, d), F32), pltpu.SemaphoreType.DMA((2,))],
        compiler_params=_params("arbitrary"),
        name="moe_combine",
    )(dest_flat, dest_flat, x, route, g, b, yb)


def _moe_plan(route, counts):
    bm = MOE_BLOCK_ROWS
    t = route.shape[0]
    n_blocks = -(-2 * t // bm) + MOE_EXPERTS
    e_idx = route[:, 0:2].astype(jnp.int32)
    rank = route[:, 4:6].astype(jnp.int32)
    cnt = counts[0, ROUTE_LANE0:ROUTE_LANE0 + MOE_EXPERTS].astype(jnp.int32)
    padded = ((cnt + bm - 1) // bm) * bm
    pad_end = jnp.cumsum(padded)
    pad_start = pad_end - padded
    experts = jnp.arange(MOE_EXPERTS, dtype=jnp.int32)
    dest = (rank + jnp.sum(jnp.where(e_idx[..., None] == experts, pad_start, 0), axis=-1)).reshape(-1)
    nused = pad_end[-1] // bm
    blk = jnp.arange(n_blocks, dtype=jnp.int32)
    blk_e = jnp.sum((pad_end[None, :] <= (blk * bm)[:, None]).astype(jnp.int32), axis=1)
    blk_e = jnp.minimum(blk_e, MOE_EXPERTS - 1)
    last_e = jnp.max(jnp.where(cnt > 0, experts, 0))
    blk_e = jnp.where(blk < nused, blk_e, last_e)
    empty = cnt == 0
    tail_rows = jnp.where(empty, (nused + jnp.cumsum(empty.astype(jnp.int32)) - 1) * bm, pad_end - bm)
    return (dest.astype(jnp.int32), blk_e.astype(jnp.int32), nused.reshape(1).astype(jnp.int32),
            tail_rows.astype(jnp.int32), n_blocks)


def _hi_lo(w):
    hi = w.astype(BF16)
    return hi, (w - hi.astype(F32)).astype(BF16)


def kernel(x_prompt, x_sample, mem_prompt, cache_mem_k, cache_mem_v, state_gla, state_hgrn, state_conv,
           ln_g, ln_b, a_w_in, a_b_in, a_ln_g, a_ln_b, a_w_s, a_b_s, a_w_out, a_b_out,
           b_w_in, b_w_g2, b_b_g, b_norm_g, b_w_out, c_lb, c_w_in, c_norm_g, c_w_out,
           d_w_in, d_b_in, d_w_dw, d_b_dw, d_ln_g, d_ln_b, d_w_out, d_b_out,
           m_w_q, m_w_k, m_w_v, m_w_o, r_w_grp, r_b_grp, r_w_exp, r_b_exp, e_w_gate_up, e_w_down):
    bp, lp, d = x_prompt.shape
    bs, ls, _ = x_sample.shape
    tp, ts = bp * lp, bs * ls
    t = tp + ts
    mem_len = mem_prompt.shape[1]

    row = lambda a: a.reshape(1, -1)
    x = None

    mem2d = mem_prompt.reshape(bp * mem_len, d)
    mem_k, mem_k_flat = _mem_proj(mem2d, m_w_k.astype(BF16), mem_len)
    mem_v, mem_v_flat = _mem_proj(mem2d, m_w_v.astype(BF16), mem_len)

    cache_k = _cache_rows(cache_mem_k)
    cache_v = _cache_rows(cache_mem_v)

    lb_all = jnp.cumsum(jax.nn.softmax(c_lb.astype(F32), axis=0), axis=0)
    lb_all = lb_all - lb_all[:1]
    ltri = jnp.tril(jnp.ones((ROW_TILE, ROW_TILE), F32), -1).astype(BF16)
    zero_bias = jnp.zeros((1, d), F32)

    outs = {"v": [], "gla_p": [], "gla_s": [], "hgrn_p": [], "hgrn_s": [], "conv_p": [], "conv_s": []}
    for i in range(DEPTH):
        j = i // N_MIXERS
        kind = i % N_MIXERS
        g1, b1 = row(ln_g[i, 0]), row(ln_b[i, 0])
        if kind == 0:
            tril = jnp.tril(jnp.ones((A_CHUNK, A_CHUNK), bool))
            wc_p = jnp.where(tril, a_w_s[j], 0.0).astype(BF16)
            bc_p = a_b_s[j][:, :, None]
            reps = A_CHUNK // ls
            small = jnp.where(jnp.tril(jnp.ones((ls, ls), bool)), a_w_s[j][:, :ls, :ls], 0.0)
            wc_s = jax.vmap(lambda m: jnp.kron(jnp.eye(reps, dtype=F32), m))(small).astype(BF16)
            bc_s = jnp.tile(a_b_s[j][:, :ls], (1, reps))[:, :, None]
            common = (a_w_in[j].astype(BF16), row(a_b_in[j]), row(a_ln_g[j]), row(a_ln_b[j]))
            tail = (a_w_out[j].astype(BF16), row(a_b_out[j]), g1, b1)
            xp_in = x_prompt.reshape(tp, d) if x is None else x[:tp]
            xs_in = x_sample.reshape(ts, d) if x is None else x[tp:]
            x1 = _gmlp(xp_in, 0, t, *common, wc_p, bc_p, *tail)
            x1, v_s = _gmlp(xs_in, tp, t, *common, wc_s, bc_s, *tail, alias=x1)
            outs["v"].append(v_s.reshape(bs, ls, -1))
        elif kind == 1:
            dk, dv = b_w_g2.shape[-1], b_w_out.shape[1]
            w_in = b_w_in[j]
            w_main = w_in[:, :2 * dk + 2 * dv].astype(BF16)
            w_low = jnp.pad(w_in[:, 2 * dk + 2 * dv:], ((0, 0), (0, LANES - B_GATE_RANK))).astype(BF16)
            g2_hi, g2_lo = _hi_lo(jnp.pad(b_w_g2[j], ((0, LANES - B_GATE_RANK), (0, 0))))
            proj_w = (w_main, w_low, g2_hi, g2_lo, row(b_b_g[j]))
            proj = functools.partial(_gla_proj_kernel, dk=dk, dv=dv, q_scale=(dk // B_HEADS) ** -0.5)
            ng = row(b_norm_g[j])
            w_out = b_w_out[j].astype(BF16)
            s0_p = jnp.zeros((bp,) + state_gla.shape[2:], F32)
            x1, s_p = _linrec_block(x, proj, proj_w, ng, s0_p, w_out, g1, b1, n_batch=bp, seq=lp, heads=B_HEADS,
                                    hk=dk, hv=dv, name="gla_block_prompt")
            q, la, k, v, r = _gla_proj(x, proj_w, dk, dv, tp, ts)
            o_s, s_s = _linrec(q, k, la, v, r, ng, state_gla[j], n_batch=bs, seq=ls, heads=B_HEADS)
            x1 = _mm_ln(o_s, w_out, zero_bias, x, g1, b1, "gla_out_sample", row_off=tp, fill=(x1,))
            outs["gla_p"].append(s_p)
            outs["gla_s"].append(s_s)
        elif kind == 2:
            heads = state_hgrn.shape[2]
            proj_w = (c_w_in[j].astype(BF16), row(lb_all[i]))
            proj = functools.partial(_hgrn_proj_kernel, d=d, q_scale=C_EXPAND ** -0.5)
            ng = row(c_norm_g[j])
            w_out = c_w_out[j].astype(BF16)
            s0_p = jnp.zeros((bp,) + state_hgrn.shape[2:], F32)
            x1, s_p = _linrec_block(x, proj, proj_w, ng, s0_p, w_out, g1, b1, n_batch=bp, seq=lp, heads=heads,
                                    hk=d, hv=d, name="hgrn_block_prompt")
            q, lf, k, v, gt = _hgrn_proj(x, proj_w, tp, ts)
            o_s, s_s = _linrec(q, k, lf, v, gt, ng, state_hgrn[j], n_batch=bs, seq=ls, heads=heads)
            x1 = _mm_ln(o_s, w_out, zero_bias, x, g1, b1, "hgrn_out_sample", row_off=tp, fill=(x1,))
            outs["hgrn_p"].append(s_p)
            outs["hgrn_s"].append(s_s)
        else:
            w_in, b_in = d_w_in[j].astype(BF16), row(d_b_in[j])
            w_out, b_out = d_w_out[j].astype(BF16), row(d_b_out[j])
            cargs = (d_w_dw[j], row(d_b_dw[j]), row(d_ln_g[j]), row(d_ln_b[j]))
            conv0 = jnp.zeros((bp, D_BUF, d), F32)
            x1, s_p = _conf_block(x, w_in, b_in, conv0, *cargs, w_out, b_out, g1, b1, n_batch=bp, seq=lp)
            h_s = _glu(x, w_in, b_in, tp, ts)
            c_s, s_s = _conv(h_s, state_conv[j], *cargs, n_batch=bs, seq=ls)
            x1 = _mm_ln(c_s, w_out, b_out, x, g1, b1, "conf_out_sample", row_off=tp, fill=(x1,))
            outs["conv_p"].append(s_p)
            outs["conv_s"].append(s_s)

        w_route = jnp.concatenate([r_w_grp[i], r_w_exp[i]], axis=1)
        w_route = jnp.pad(w_route, ((0, 0), (0, LANES - w_route.shape[1])))
        b_route = jnp.pad(jnp.concatenate([r_b_grp[i], r_b_exp[i]]), (0, LANES - MOE_GROUPS - MOE_EXPERTS))
        router = _hi_lo(w_route)
        w_q, w_o = m_w_q[i].astype(BF16), m_w_o[i].astype(BF16)
        g2, b2 = row(ln_g[i, 1]), row(ln_b[i, 1])
        x2, logits = _attn_block_prompt(x1, w_q, mem_k, mem_v, w_o, g2, b2, router, i, bp, lp)
        q_s = _mm_rows(x1, w_q, "attn_q_sample", tp, ts)
        att_s = _attn_sample(q_s, cache_k, cache_v, i, bs, ls)
        x2, logits = _mm_ln(att_s, w_o, zero_bias, x1, g2, b2, "attn_out_sample", router=router, row_off=tp,
                            fill=(x2, logits))

        route, counts = _route(logits, row(b_route), ltri)
        dest, blk_e, nused, tail_rows, n_blocks = _moe_plan(route, counts)
        xs = _dispatch(tail_rows, dest, x2, n_blocks * MOE_BLOCK_ROWS)
        yb = _experts(blk_e, nused, xs, e_w_gate_up, e_w_down, i)
        if i + 1 < DEPTH:
            x = _combine(dest, x2, route, row(ln_g[i, 2]), row(ln_b[i, 2]), yb)
        else:
            y_p, y_s = _combine(dest, x2, route, row(ln_g[i, 2]), row(ln_b[i, 2]), yb, split_rows=tp)

    y_prompt = y_p.reshape(bp, lp, d)
    y_sample = y_s.reshape(bs, ls, d)
    mem_k_p = _cache_unrows(mem_k_flat, MEM_HEADS)
    mem_v_p = _cache_unrows(mem_v_flat, MEM_HEADS)
    return (y_prompt, y_sample, mem_k_p, mem_v_p, jnp.stack(outs["gla_p"]), jnp.stack(outs["hgrn_p"]),
            jnp.stack(outs["conv_p"]), jnp.stack(outs["v"]), jnp.stack(outs["gla_s"]),
            jnp.stack(outs["hgrn_s"]), jnp.stack(outs["conv_s"]))
```

```python
import functools
import math

import jax
import jax.numpy as jnp
from jax import lax
from jax.experimental import pallas as pl
from jax.experimental.pallas import tpu as pltpu

F32 = jnp.float32
BF16 = jnp.bfloat16

D_MODEL = 1024
DEPTH = 4
N_MIXERS = 4
ALPHA = (2.0 * DEPTH) ** 0.25
LN_EPS = 1e-5
A_CHUNK = 128
A_GROUPS = 4
B_HEADS = 4
B_GATE_RANK = 16
B_GATE_TAU = 16.0
C_EXPAND = 128
D_CONV_W = 31
D_BUF = D_CONV_W - 1
LIN_CHUNK = 32
MEM_HEADS = 4
MOE_GROUPS = 4
MOE_PER_GROUP = 8
MOE_EXPERTS = MOE_GROUPS * MOE_PER_GROUP
MOE_HIDDEN = 512

LANES = 128
SUBLANES = 8
ROW_TILE = 512
MOE_BLOCK_ROWS = 512
ROUTE_LANE0 = MOE_GROUPS
VMEM_LIMIT = 56 * 1024 * 1024
INV_SQRT2 = 1.0 / math.sqrt(2.0)


def _params(*sem, vmem=VMEM_LIMIT):
    return pltpu.CompilerParams(dimension_semantics=sem, vmem_limit_bytes=vmem)


def _dot(a, b):
    return jnp.dot(a, b, preferred_element_type=F32)


def _dot_nt(a, b):
    return lax.dot_general(a, b, (((1,), (1,)), ((), ())), preferred_element_type=F32)


def _dot_hi(a, w_hi, w_lo):
    a_hi = a.astype(BF16)
    a_lo = (a - a_hi.astype(F32)).astype(BF16)
    return _dot(a_hi, w_hi) + _dot(a_lo, w_hi) + _dot(a_hi, w_lo)


def _layer_norm(x, g, b):
    mu = jnp.mean(x, axis=-1, keepdims=True)
    xc = x - mu
    var = jnp.mean(xc * xc, axis=-1, keepdims=True)
    return xc * lax.rsqrt(var + LN_EPS) * g + b


def _sigmoid(x):
    return 1.0 / (1.0 + jnp.exp(-x))


def _silu(x):
    return x * _sigmoid(x)


def _gelu(x):
    return 0.5 * x * (1.0 + lax.erf(x * INV_SQRT2))


def _log_sigmoid(x):
    return jnp.minimum(x, 0.0) - jnp.log(1.0 + jnp.exp(-jnp.abs(x)))


def _full(shape):
    return pl.BlockSpec(shape, lambda *_: (0,) * len(shape))


def _rows(tm, n, off=0):
    return pl.BlockSpec((tm, n), lambda i: (i + off, 0))


def _mm_kernel(a_ref, w_ref, o_ref):
    o_ref[...] = _dot(a_ref[...].astype(BF16), w_ref[...])


def _mem_proj_kernel(a_ref, w_ref, nat_ref, flat_ref, *, mem_len):
    res = _dot(a_ref[...].astype(BF16), w_ref[...])
    nat_ref[...] = res
    n_lt = res.shape[1] // (MEM_HEADS * LANES)
    for b in range(res.shape[0] // mem_len):
        for h in range(MEM_HEADS):
            for lt in range(n_lt):
                c0 = (h * n_lt + lt) * LANES
                flat_ref[b, pl.ds(lt * MEM_HEADS + h, mem_len, stride=MEM_HEADS * n_lt), :] = (
                    res[b * mem_len:(b + 1) * mem_len, c0:c0 + LANES])


def _mem_proj(a, w, mem_len):
    m, k = a.shape
    nl, _, n = w.shape
    tm = min(m, ROW_TILE)
    nbat = tm // mem_len
    return pl.pallas_call(
        functools.partial(_mem_proj_kernel, mem_len=mem_len),
        grid=(nl, m // tm),
        in_specs=[pl.BlockSpec((tm, k), lambda l, i: (i, 0)),
                  pl.BlockSpec((None, k, n), lambda l, i: (l, 0, 0))],
        out_specs=[pl.BlockSpec((None, tm, n), lambda l, i: (l, i, 0)),
                   pl.BlockSpec((None, nbat, mem_len * n // LANES, LANES), lambda l, i: (l, i, 0, 0))],
        out_shape=[jax.ShapeDtypeStruct((nl, m, n), F32),
                   jax.ShapeDtypeStruct((nl, m // mem_len, mem_len * n // LANES, LANES), F32)],
        compiler_params=_params("parallel", "parallel"),
        name="mem_kv_proj",
    )(a, w)


def _mm_rows(a, w, name, row_off, n_rows):
    k = a.shape[1]
    n = w.shape[1]
    tm = ROW_TILE
    return pl.pallas_call(
        _mm_kernel,
        grid=(n_rows // tm,),
        in_specs=[_rows(tm, k, row_off // tm), _full((k, n))],
        out_specs=_rows(tm, n),
        out_shape=jax.ShapeDtypeStruct((n_rows, n), F32),
        compiler_params=_params("parallel"),
        name=name,
    )(a, w)


def _mm_ln_kernel(a_ref, w_ref, bias_ref, x_ref, g_ref, b_ref, *rest, with_logits, n_alias):
    h = _dot(a_ref[...].astype(BF16), w_ref[...]) + bias_ref[...]
    y = _layer_norm(ALPHA * x_ref[...] + h, g_ref[...], b_ref[...])
    if with_logits:
        wr_hi_ref, wr_lo_ref = rest[:2]
        o_ref, lg_ref = rest[2 + n_alias:]
        lg_ref[...] = _dot_hi(y, wr_hi_ref[...], wr_lo_ref[...])
    else:
        (o_ref,) = rest[n_alias:]
    o_ref[...] = y


def _mm_ln(a, w, bias, x, g, b, name, router=None, row_off=0, fill=None):
    m, k = a.shape
    t, d = x.shape
    tm = ROW_TILE
    off = row_off // tm
    in_specs = [_rows(tm, k), _full((k, d)), _full((1, d)), _rows(tm, d, off), _full((1, d)), _full((1, d))]
    args = [a, w, bias, x, g, b]
    out_specs = [_rows(tm, d, off)]
    out_shape = [jax.ShapeDtypeStruct((t, d), F32)]
    if router is not None:
        in_specs += [_full((d, LANES)), _full((d, LANES))]
        args += list(router)
        out_specs.append(_rows(tm, LANES, off))
        out_shape.append(jax.ShapeDtypeStruct((t, LANES), F32))
    aliases = {}
    for n, arr in enumerate(fill or ()):
        in_specs.append(pl.BlockSpec(memory_space=pl.ANY))
        args.append(arr)
        aliases[len(args) - 1] = n
    res = pl.pallas_call(
        functools.partial(_mm_ln_kernel, with_logits=router is not None, n_alias=len(aliases)),
        grid=(m // tm,),
        in_specs=in_specs,
        out_specs=out_specs,
        out_shape=out_shape,
        input_output_aliases=aliases,
        compiler_params=_params("parallel"),
        name=name,
    )(*args)
    return res if router is not None else res[0]


def _gmlp_kernel(x_ref, w_in_ref, b_in_ref, lng_ref, lnb_ref, wc_ref, bc_ref, w_out_ref, b_out_ref,
                 g_ref, b_ref, *rest, emit_v, n_chunks):
    if emit_v:
        _alias_ref, o_ref, v_ref, vn_ref = rest
    else:
        o_ref, vn_ref = rest
    half = w_out_ref.shape[0]
    gw = half // A_GROUPS
    x = x_ref[...]
    xb = x.astype(BF16)
    v = _gelu(_dot(xb, w_in_ref[:, half:]) + b_in_ref[:, half:])
    vn = _layer_norm(v, lng_ref[...], lnb_ref[...])
    vn_ref[...] = vn
    if emit_v:
        v_ref[...] = vn
    acc = jnp.zeros(x.shape, F32)
    for grp in range(A_GROUPS):
        cols = slice(grp * gw, (grp + 1) * gw)
        u = _gelu(_dot(xb, w_in_ref[:, cols]) + b_in_ref[:, cols])
        mixed = []
        for c in range(n_chunks):
            vc = vn_ref[c * A_CHUNK:(c + 1) * A_CHUNK, cols].astype(BF16)
            mixed.append(_dot(wc_ref[grp], vc) + bc_ref[grp])
        mixed = mixed[0] if n_chunks == 1 else jnp.concatenate(mixed, axis=0)
        acc = acc + _dot((u * mixed).astype(BF16), w_out_ref[cols, :])
    h = acc + b_out_ref[...]
    o_ref[...] = _layer_norm(ALPHA * x + h, g_ref[...], b_ref[...])


GMLP_CHUNKS = 4


def _gmlp(x, row_off, t, w_in, b_in, lng, lnb, wc, bc, w_out, b_out, g, b, alias=None):
    n_rows, d = x.shape
    ffn = w_in.shape[1]
    half = ffn // 2
    tm = GMLP_CHUNKS * A_CHUNK
    emit_v = alias is not None
    off = row_off // tm
    once = lambda shape: pl.BlockSpec(shape, lambda *_: (0,) * len(shape), pipeline_mode=pl.Buffered(1))
    in_specs = [_rows(tm, d), once((d, ffn)), _full((1, ffn)), _full((1, half)), _full((1, half)),
                _full((A_GROUPS, A_CHUNK, A_CHUNK)), _full((A_GROUPS, A_CHUNK, 1)), once((half, d)),
                _full((1, d)), _full((1, d)), _full((1, d))]
    args = [x, w_in, b_in, lng, lnb, wc, bc, w_out, b_out, g, b]
    out_specs = _rows(tm, d, off)
    out_shape = jax.ShapeDtypeStruct((t, d), F32)
    aliases = {}
    if emit_v:
        in_specs.append(pl.BlockSpec(memory_space=pl.ANY))
        args.append(alias)
        aliases = {len(args) - 1: 0}
        out_specs = [out_specs, _rows(tm, half)]
        out_shape = [out_shape, jax.ShapeDtypeStruct((n_rows, half), F32)]
    return pl.pallas_call(
        functools.partial(_gmlp_kernel, emit_v=emit_v, n_chunks=tm // A_CHUNK),
        grid=(n_rows // tm,),
        in_specs=in_specs,
        out_specs=out_specs,
        out_shape=out_shape,
        scratch_shapes=[pltpu.VMEM((tm, half), F32)],
        input_output_aliases=aliases,
        compiler_params=_params("parallel"),
        name="gmlp_sample" if emit_v else "gmlp_prompt",
    )(*args)


def _gla_proj_kernel(x_ref, w_ref, wlow_ref, g2_hi_ref, g2_lo_ref, bg_ref,
                     q_ref, la_ref, k_ref, v_ref, r_ref, *, dk, dv, q_scale):
    xb = x_ref[...].astype(BF16)
    q_ref[...] = _dot(xb, w_ref[:, 0:dk]) * q_scale
    k_ref[...] = _dot(xb, w_ref[:, dk:2 * dk])
    v_ref[...] = _dot(xb, w_ref[:, 2 * dk:2 * dk + dv])
    r_ref[...] = _dot(xb, w_ref[:, 2 * dk + dv:2 * dk + 2 * dv])
    g_low = _dot(xb, wlow_ref[...])
    pre = _dot_hi(g_low, g2_hi_ref[...], g2_lo_ref[...]) + bg_ref[...]
    la_ref[...] = _log_sigmoid(pre) * (1.0 / B_GATE_TAU)


def _gla_proj(x, weights, dk, dv, row_off, n_rows):
    d = x.shape[1]
    tm = ROW_TILE
    shapes = [dk, dk, dk, dv, dv]
    return pl.pallas_call(
        functools.partial(_gla_proj_kernel, dk=dk, dv=dv, q_scale=(dk // B_HEADS) ** -0.5),
        grid=(n_rows // tm,),
        in_specs=[_rows(tm, d, row_off // tm)] + [_full(w.shape) for w in weights],
        out_specs=[_rows(tm, n) for n in shapes],
        out_shape=[jax.ShapeDtypeStruct((n_rows, n), F32) for n in shapes],
        compiler_params=_params("parallel"),
        name="gla_proj",
    )(x, *weights)


def _hgrn_proj_kernel(x_ref, w_ref, lb_ref, q_ref, lf_ref, k_ref, v_ref, gt_ref, *, d, q_scale):
    xb = x_ref[...].astype(BF16)
    lb = lb_ref[...]
    q_ref[...] = _silu(_dot(xb, w_ref[:, 0:d])) * q_scale
    f = _dot(xb, w_ref[:, d:2 * d])
    lf_ref[...] = jnp.log(lb + (1.0 - lb) * _sigmoid(f))
    k_ref[...] = (1.0 - lb) * _sigmoid(-f)
    v_ref[...] = _dot(xb, w_ref[:, 2 * d:3 * d])
    gt_ref[...] = _dot(xb, w_ref[:, 3 * d:4 * d])


def _hgrn_proj(x, weights, row_off, n_rows):
    d = x.shape[1]
    tm = ROW_TILE
    return pl.pallas_call(
        functools.partial(_hgrn_proj_kernel, d=d, q_scale=C_EXPAND ** -0.5),
        grid=(n_rows // tm,),
        in_specs=[_rows(tm, d, row_off // tm)] + [_full(w.shape) for w in weights],
        out_specs=[_rows(tm, d)] * 5,
        out_shape=[jax.ShapeDtypeStruct((n_rows, d), F32)] * 5,
        compiler_params=_params("parallel"),
        name="hgrn_proj",
    )(x, *weights)


def _cumsum_rows(x, chunk):
    pos = lax.broadcasted_iota(jnp.int32, x.shape, 0) & (chunk - 1)
    step = 1
    while step < chunk:
        x = x + jnp.where(pos >= step, pltpu.roll(x, step, axis=0), 0.0)
        step *= 2
    return x


def _chunk_rows(x, row, chunk, n_chunks):
    parts = [jnp.broadcast_to(x[c * chunk + row:c * chunk + row + 1, :], (chunk, x.shape[1]))
             for c in range(n_chunks)]
    return parts[0] if n_chunks == 1 else jnp.concatenate(parts, axis=0)


def _linrec_mask(tl, chunk):
    r_id = lax.broadcasted_iota(jnp.int32, (tl, tl), 0)
    c_id = lax.broadcasted_iota(jnp.int32, (tl, tl), 1)
    return (r_id >= c_id) & ((r_id // chunk) == (c_id // chunk))


def _linrec_heads(q_ref, k_ref, g_ref, v_ref, gate_ref, ng_ref, o_ref, st_ref, rs, *, heads, dk, dv, chunk, n_chunks):
    mask = _linrec_mask(chunk * n_chunks, chunk)
    for h in range(heads):
        ks = slice(h * dk, (h + 1) * dk)
        vs = slice(h * dv, (h + 1) * dv)
        bcum = _cumsum_rows(g_ref[rs, ks], chunk)
        b_mid = _chunk_rows(bcum, chunk // 2 - 1, chunk, n_chunks)
        b_end = _chunk_rows(bcum, chunk - 1, chunk, n_chunks)
        q = q_ref[rs, ks]
        k = k_ref[rs, ks]
        vb = v_ref[rs, vs].astype(BF16)
        q_in = (q * jnp.exp(bcum)).astype(BF16)
        q_a = (q * jnp.exp(bcum - b_mid)).astype(BF16)
        k_a = (k * jnp.exp(b_mid - bcum)).astype(BF16)
        k_end = k * jnp.exp(b_end - bcum)
        a = jnp.where(mask, _dot_nt(q_a, k_a), 0.0).astype(BF16)
        o_intra = _dot(a, vb)
        st = st_ref[h]
        outs = []
        for c in range(n_chunks):
            cr = slice(c * chunk, (c + 1) * chunk)
            outs.append(o_intra[cr] + _dot(q_in[cr], st.astype(BF16)))
            decay = jnp.exp(bcum[cr].T[:, chunk - 1:chunk])
            st = st * decay + _dot(k_end[cr].T.astype(BF16), vb[cr])
        st_ref[h] = st
        o = outs[0] if n_chunks == 1 else jnp.concatenate(outs, axis=0)
        o = o * lax.rsqrt(jnp.mean(o * o, axis=-1, keepdims=True) + LN_EPS) * ng_ref[:, vs]
        o_ref[rs, vs] = o * _silu(gate_ref[rs, vs])


def _linrec_kernel(q_ref, k_ref, g_ref, v_ref, gate_ref, ng_ref, s0_ref, o_ref, sout_ref, st_ref, *, nb, tl, **dims):
    li = pl.program_id(1)
    for n in range(nb):
        @pl.when(li == 0)
        def _():
            st_ref[...] = s0_ref[n]

        _linrec_heads(q_ref, k_ref, g_ref, v_ref, gate_ref, ng_ref, o_ref, st_ref, slice(n * tl, (n + 1) * tl),
                      **dims)
        sout_ref[n] = st_ref[...]


LINREC_CHUNK = 2 * LIN_CHUNK
LINREC_SAMPLE_NB = 4


def _linrec_tiling(seq):
    chunk = LINREC_CHUNK if seq % LINREC_CHUNK == 0 else seq
    tl = min(seq, 4 * chunk)
    return chunk, tl, seq // tl


def _linrec(q, k, g, v, gate, ng, s0, *, n_batch, seq, heads):
    t, hk = q.shape
    hv = v.shape[1]
    dk, dv = hk // heads, hv // heads
    chunk, tl, n_l = _linrec_tiling(seq)
    nb = LINREC_SAMPLE_NB if n_l == 1 else 1

    def rows(n):
        return pl.BlockSpec((nb * tl, n), lambda b, l: (b * n_l + l, 0))

    state_spec = pl.BlockSpec((nb, heads, dk, dv), lambda b, l: (b, 0, 0, 0))
    return pl.pallas_call(
        functools.partial(_linrec_kernel, nb=nb, tl=tl, heads=heads, dk=dk, dv=dv, chunk=chunk,
                          n_chunks=tl // chunk),
        grid=(n_batch // nb, n_l),
        in_specs=[rows(hk), rows(hk), rows(hk), rows(hv), rows(hv), pl.BlockSpec((1, hv), lambda b, l: (0, 0)),
                  state_spec],
        out_specs=[rows(hv), state_spec],
        out_shape=[jax.ShapeDtypeStruct((t, hv), F32), jax.ShapeDtypeStruct((n_batch, heads, dk, dv), F32)],
        scratch_shapes=[pltpu.VMEM((heads, dk, dv), F32)],
        compiler_params=_params("parallel", "arbitrary"),
        name="linrec_h%d" % heads,
    )(q, k, g, v, gate, ng, s0)


def _linrec_block_kernel(*refs, proj, n_proj_w, **dims):
    x_ref = refs[0]
    proj_w = refs[1:1 + n_proj_w]
    ng_ref, s0_ref, wo_ref, g_ref, b_ref, o_ref, sout_ref = refs[1 + n_proj_w:8 + n_proj_w]
    q_s, g_s, k_s, v_s, gate_s, o_s, st_ref = refs[8 + n_proj_w:]

    @pl.when(pl.program_id(1) == 0)
    def _():
        st_ref[...] = s0_ref[0]

    proj(x_ref, *proj_w, q_s, g_s, k_s, v_s, gate_s)
    _linrec_heads(q_s, k_s, g_s, v_s, gate_s, ng_ref, o_s, st_ref, slice(0, x_ref.shape[0]), **dims)
    h = _dot(o_s[...].astype(BF16), wo_ref[...])
    o_ref[...] = _layer_norm(ALPHA * x_ref[...] + h, g_ref[...], b_ref[...])
    sout_ref[0] = st_ref[...]


def _linrec_block(x, proj, proj_w, ng, s0, w_out, g, b, *, n_batch, seq, heads, hk, hv, name):
    t, d = x.shape
    dk, dv = hk // heads, hv // heads
    chunk, tl, n_l = _linrec_tiling(seq)
    rows = pl.BlockSpec((tl, d), lambda bi, l: (bi * n_l + l, 0))
    full = lambda shape: pl.BlockSpec(shape, lambda bi, l: (0,) * len(shape))
    state_spec = pl.BlockSpec((1, heads, dk, dv), lambda bi, l: (bi, 0, 0, 0))
    return pl.pallas_call(
        functools.partial(_linrec_block_kernel, proj=proj, n_proj_w=len(proj_w), heads=heads, dk=dk, dv=dv,
                          chunk=chunk, n_chunks=tl // chunk),
        grid=(n_batch, n_l),
        in_specs=[rows] + [full(w.shape) for w in proj_w] + [full((1, hv)), state_spec, full(w_out.shape),
                                                            full((1, d)), full((1, d))],
        out_specs=[rows, state_spec],
        out_shape=[jax.ShapeDtypeStruct((t, d), F32), jax.ShapeDtypeStruct((n_batch, heads, dk, dv), F32)],
        scratch_shapes=[pltpu.VMEM((tl, hk), F32), pltpu.VMEM((tl, hk), F32), pltpu.VMEM((tl, hk), F32),
                        pltpu.VMEM((tl, hv), F32), pltpu.VMEM((tl, hv), F32), pltpu.VMEM((tl, hv), F32),
                        pltpu.VMEM((heads, dk, dv), F32)],
        compiler_params=_params("parallel", "arbitrary"),
        name=name,
    )(x, *proj_w, ng, s0, w_out, g, b)


def _glu_kernel(x_ref, w_ref, b_ref, o_ref, *, d):
    xb = x_ref[...].astype(BF16)
    a = _dot(xb, w_ref[:, 0:d]) + b_ref[:, 0:d]
    gate = _dot(xb, w_ref[:, d:2 * d]) + b_ref[:, d:2 * d]
    o_ref[...] = a * _sigmoid(gate)


def _glu(x, w, b, row_off, n_rows):
    d = x.shape[1]
    tm = ROW_TILE
    return pl.pallas_call(
        functools.partial(_glu_kernel, d=d),
        grid=(n_rows // tm,),
        in_specs=[_rows(tm, d, row_off // tm), _full(w.shape), _full(b.shape)],
        out_specs=_rows(tm, d),
        out_shape=jax.ShapeDtypeStruct((n_rows, d), F32),
        compiler_params=_params("parallel"),
        name="conf_glu",
    )(x, w, b)


CONV_PAD = 32


CONV_LEAD = CONV_PAD - D_BUF


def _conv_window_step(buf_ref, h, state_of, wdw_ref, bdw_ref, lg_ref, lb_ref, conv_ref, first, tl):
    d = h.shape[1]
    rb = min(tl, 64)
    cw = LANES
    buf_ref[CONV_PAD + tl:CONV_PAD + tl + SUBLANES, :] = jnp.zeros((SUBLANES, d), F32)

    @pl.when(first)
    def _():
        buf_ref[CONV_LEAD:CONV_PAD, :] = state_of()

    buf_ref[CONV_PAD:CONV_PAD + tl, :] = h
    for r0 in range(0, tl, rb):
        for c0 in range(0, d, cw):
            cols = slice(c0, c0 + cw)
            acc = jnp.zeros((rb, cw), F32)
            for s in range(SUBLANES):
                part = None
                for a in range((CONV_PAD + SUBLANES) // SUBLANES):
                    j = SUBLANES * a + s - CONV_LEAD
                    if 0 <= j < D_CONV_W:
                        rows = slice(r0 + SUBLANES * a, r0 + SUBLANES * a + rb + SUBLANES)
                        term = wdw_ref[j:j + 1, cols] * buf_ref[rows, cols]
                        part = term if part is None else part + term
                acc = acc + part[s:s + rb, :]
            conv_ref[r0:r0 + rb, cols] = acc + bdw_ref[:, cols]
    y = _silu(_layer_norm(conv_ref[...], lg_ref[...], lb_ref[...]))
    new_state = buf_ref[tl + CONV_LEAD:tl + CONV_PAD, :]
    buf_ref[0:CONV_PAD, :] = buf_ref[tl:tl + CONV_PAD, :]
    return y, new_state


def _conv_kernel(h_ref, st_ref, wdw_ref, bdw_ref, lg_ref, lb_ref, o_ref, sout_ref, buf_ref, conv_ref, *, nb, tl):
    first = pl.program_id(1) == 0
    for n in range(nb):
        y, new_state = _conv_window_step(buf_ref, h_ref[n * tl:(n + 1) * tl, :], lambda: st_ref[n], wdw_ref,
                                         bdw_ref, lg_ref, lb_ref, conv_ref, first, tl)
        o_ref[n * tl:(n + 1) * tl, :] = y
        sout_ref[n] = new_state


CONV_TILE = 256


def _conv(h, state, wdw, bdw, lg, lb, *, n_batch, seq):
    t, d = h.shape
    tl = min(seq, CONV_TILE)
    n_l = seq // tl
    nb = 8 if n_l == 1 else 1
    rows = pl.BlockSpec((nb * tl, d), lambda b, l: (b * n_l + l, 0))
    state_spec = pl.BlockSpec((nb, D_BUF, d), lambda b, l: (b, 0, 0))
    return pl.pallas_call(
        functools.partial(_conv_kernel, nb=nb, tl=tl),
        grid=(n_batch // nb, n_l),
        in_specs=[rows, state_spec, _full(wdw.shape), _full((1, d)), _full((1, d)), _full((1, d))],
        out_specs=[rows, state_spec],
        out_shape=[jax.ShapeDtypeStruct((t, d), F32), jax.ShapeDtypeStruct((n_batch, D_BUF, d), F32)],
        scratch_shapes=[pltpu.VMEM((CONV_PAD + tl + SUBLANES, d), F32), pltpu.VMEM((tl, d), F32)],
        compiler_params=_params("parallel", "arbitrary"),
        name="conf_conv",
    )(h, state, wdw, bdw, lg, lb)


def _conf_block_kernel(x_ref, win_ref, bin_ref, st_ref, wdw_ref, bdw_ref, lg_ref, lb_ref, wo_ref, bo_ref,
                       g_ref, b_ref, o_ref, sout_ref, buf_ref, conv_ref):
    x = x_ref[...]
    tl, d = x.shape
    xb = x.astype(BF16)
    h = (_dot(xb, win_ref[:, 0:d]) + bin_ref[:, 0:d]) * _sigmoid(_dot(xb, win_ref[:, d:2 * d]) + bin_ref[:, d:2 * d])
    y, new_state = _conv_window_step(buf_ref, h, lambda: st_ref[0], wdw_ref, bdw_ref, lg_ref, lb_ref, conv_ref,
                                     pl.program_id(1) == 0, tl)
    out = _dot(y.astype(BF16), wo_ref[...]) + bo_ref[...]
    o_ref[...] = _layer_norm(ALPHA * x + out, g_ref[...], b_ref[...])
    sout_ref[0] = new_state


def _conf_block(x, w_in, b_in, state, wdw, bdw, lg, lb, w_out, b_out, g, b, *, n_batch, seq):
    t, d = x.shape
    tl = CONV_TILE
    n_l = seq // tl
    rows = pl.BlockSpec((tl, d), lambda bi, l: (bi * n_l + l, 0))
    full = lambda shape: pl.BlockSpec(shape, lambda bi, l: (0,) * len(shape))
    state_spec = pl.BlockSpec((1, D_BUF, d), lambda bi, l: (bi, 0, 0))
    return pl.pallas_call(
        _conf_block_kernel,
        grid=(n_batch, n_l),
        in_specs=[rows, full(w_in.shape), full(b_in.shape), state_spec, full(wdw.shape), full((1, d)), full((1, d)),
                  full((1, d)), full(w_out.shape), full((1, d)), full((1, d)), full((1, d))],
        out_specs=[rows, state_spec],
        out_shape=[jax.ShapeDtypeStruct((t, d), F32), jax.ShapeDtypeStruct((n_batch, D_BUF, d), F32)],
        scratch_shapes=[pltpu.VMEM((CONV_PAD + tl + SUBLANES, d), F32), pltpu.VMEM((tl, d), F32)],
        compiler_params=_params("parallel", "arbitrary"),
        name="conf_block_prompt",
    )(x, w_in, b_in, state, wdw, bdw, lg, lb, w_out, b_out, g, b)


def _attn_heads(q, k_of, v_of, hd):
    outs = []
    for h in range(MEM_HEADS):
        hs = slice(h * hd, (h + 1) * hd)
        s = _dot_nt(q[:, hs].astype(BF16), k_of(hs).astype(BF16)) * (hd ** -0.5)
        p = jnp.exp(s - jnp.max(s, axis=-1, keepdims=True))
        denom = jnp.sum(p, axis=-1, keepdims=True)
        outs.append(_dot(p.astype(BF16), v_of(hs).astype(BF16)) / denom)
    return outs


def _attn_block_kernel(x_ref, wq_ref, k_ref, v_ref, wo_ref, g_ref, b_ref, wr_hi_ref, wr_lo_ref, o_ref, lg_ref):
    x = x_ref[...]
    hd = x.shape[1] // MEM_HEADS
    q = _dot(x.astype(BF16), wq_ref[...])
    att = jnp.concatenate(_attn_heads(q, lambda hs: k_ref[:, hs], lambda hs: v_ref[:, hs], hd), axis=1)
    y = _layer_norm(ALPHA * x + _dot(att.astype(BF16), wo_ref[...]), g_ref[...], b_ref[...])
    lg_ref[...] = _dot_hi(y, wr_hi_ref[...], wr_lo_ref[...])
    o_ref[...] = y


def _attn_block_prompt(x, w_q, mem_k, mem_v, w_o, g, b, router, layer, n_batch, seq):
    t, d = x.shape
    m = mem_k.shape[1] // n_batch
    tl = ROW_TILE
    n_l = seq // tl
    rows = lambda n: pl.BlockSpec((tl, n), lambda bi, l: (bi * n_l + l, 0))
    kv = pl.BlockSpec((None, m, d), lambda bi, l: (layer, bi, 0))
    full = lambda shape: pl.BlockSpec(shape, lambda bi, l: (0,) * len(shape))
    return pl.pallas_call(
        _attn_block_kernel,
        grid=(n_batch, n_l),
        in_specs=[rows(d), full((d, d)), kv, kv, full((d, d)), full((1, d)), full((1, d)),
                  full((d, LANES)), full((d, LANES))],
        out_specs=[rows(d), rows(LANES)],
        out_shape=[jax.ShapeDtypeStruct((t, d), F32), jax.ShapeDtypeStruct((t, LANES), F32)],
        compiler_params=_params("parallel", "parallel"),
        name="attn_block_prompt",
    )(x, w_q, mem_k, mem_v, w_o, g, b, *router)


ATTN_SAMPLE_NB = 4


def _attn_sample_kernel(q_ref, k_ref, v_ref, o_ref, *, nb, seq):
    hd = q_ref.shape[-1] // MEM_HEADS
    n_lt = hd // LANES
    m = k_ref.shape[1] // (MEM_HEADS * n_lt)

    def head(ref, n, h):
        tiles = [ref[n, pl.ds(lt * MEM_HEADS + h, m, stride=MEM_HEADS * n_lt), :] for lt in range(n_lt)]
        return jnp.concatenate(tiles, axis=1).astype(BF16)

    pairs = [(n, h) for n in range(nb) for h in range(MEM_HEADS)]
    s = jnp.concatenate(
        [_dot_nt(q_ref[n * seq:(n + 1) * seq, h * hd:(h + 1) * hd].astype(BF16), head(k_ref, n, h))
         for n, h in pairs], axis=0) * (hd ** -0.5)
    p = jnp.exp(s - jnp.max(s, axis=-1, keepdims=True))
    inv = 1.0 / jnp.sum(p, axis=-1, keepdims=True)
    for idx, (n, h) in enumerate(pairs):
        rs = slice(idx * seq, (idx + 1) * seq)
        o = _dot(p[rs].astype(BF16), head(v_ref, n, h)) * inv[rs]
        o_ref[n * seq:(n + 1) * seq, h * hd:(h + 1) * hd] = o


def _cache_rows(cache):
    nl, nbat, m, heads, hd = cache.shape
    c = cache.reshape(nl, nbat, m, heads, hd // LANES, LANES)
    return c.transpose(0, 1, 2, 4, 3, 5).reshape(nl, nbat, m * (hd // LANES) * heads, LANES)


def _cache_unrows(flat, heads):
    nl, nbat, rows, _ = flat.shape
    n_lt = D_MODEL // (heads * LANES)
    m = rows // (heads * n_lt)
    c = flat.reshape(nl, nbat, m, n_lt, heads, LANES).transpose(0, 1, 2, 4, 3, 5)
    return c.reshape(nl, nbat, m, heads, n_lt * LANES)


def _attn_sample(q, cache_k, cache_v, layer, n_batch, seq):
    t, d = q.shape
    nb = ATTN_SAMPLE_NB
    rows = pl.BlockSpec((nb * seq, d), lambda i: (i, 0))
    kv = pl.BlockSpec((None, nb) + cache_k.shape[2:], lambda i: (layer, i, 0, 0))
    return pl.pallas_call(
        functools.partial(_attn_sample_kernel, nb=nb, seq=seq),
        grid=(n_batch // nb,),
        in_specs=[rows, kv, kv],
        out_specs=rows,
        out_shape=jax.ShapeDtypeStruct((t, d), F32),
        compiler_params=_params("parallel"),
        name="attn_sample",
    )(q, cache_k, cache_v)


def _route_kernel(lg_ref, bias_ref, ltri_ref, route_ref, counts_ref, carry_ref):
    i = pl.program_id(0)

    @pl.when(i == 0)
    def _():
        carry_ref[...] = jnp.zeros(carry_ref.shape, F32)

    z = lg_ref[...] + bias_ref[...]
    lane = lax.broadcasted_iota(jnp.int32, z.shape, 1).astype(F32)
    neg = -jnp.inf
    far = float(LANES)

    def first_max(mask):
        vmax = jnp.max(jnp.where(mask, z, neg), axis=-1, keepdims=True)
        idx = jnp.min(jnp.where(mask & (z == vmax), lane, far), axis=-1, keepdims=True)
        return vmax, idx

    gmask = lane < float(MOE_GROUPS)
    gmax, gidx = first_max(gmask)
    gsum = jnp.sum(jnp.where(gmask, jnp.exp(z - gmax), 0.0), axis=-1, keepdims=True)
    g_w = 1.0 / gsum
    lo = float(ROUTE_LANE0) + float(MOE_PER_GROUP) * gidx
    emask = (lane >= lo) & (lane < lo + float(MOE_PER_GROUP))
    v1, i1 = first_max(emask)
    v2, i2 = first_max(emask & (lane != i1))
    tt = jnp.exp(v2 - v1)
    w0 = g_w / (1.0 + tt)
    w1 = g_w * tt / (1.0 + tt)
    sel1 = lane == i1
    sel2 = lane == i2
    onehot = jnp.where(sel1 | sel2, 1.0, 0.0)
    before = _dot(ltri_ref[...], onehot.astype(BF16)) + carry_ref[...]
    rank0 = jnp.sum(jnp.where(sel1, before, 0.0), axis=-1, keepdims=True)
    rank1 = jnp.sum(jnp.where(sel2, before, 0.0), axis=-1, keepdims=True)
    carry = carry_ref[...] + jnp.sum(onehot, axis=0, keepdims=True)
    carry_ref[...] = carry
    counts_ref[...] = carry
    e_off = float(ROUTE_LANE0)
    out = jnp.zeros(z.shape, F32)
    for ln, val in enumerate((i1 - e_off, i2 - e_off, w0, w1, rank0, rank1)):
        out = jnp.where(lane == float(ln), val, out)
    route_ref[...] = out


def _route(logits, bias, ltri):
    t = logits.shape[0]
    tm = ROW_TILE
    return pl.pallas_call(
        _route_kernel,
        grid=(t // tm,),
        in_specs=[_rows(tm, LANES), _full((1, LANES)), _full((tm, tm))],
        out_specs=[_rows(tm, LANES), _full((1, LANES))],
        out_shape=[jax.ShapeDtypeStruct((t, LANES), F32), jax.ShapeDtypeStruct((1, LANES), F32)],
        scratch_shapes=[pltpu.VMEM((1, LANES), F32)],
        compiler_params=_params("arbitrary"),
        name="moe_route",
    )(logits, bias, ltri)


def _row_copy_wait(src_rows, dst_rows, sem):
    pltpu.make_async_copy(src_rows, dst_rows, sem).wait()


def _dispatch_kernel(tail_ref, dest_ref, x_ref, xs_ref, zeros_ref, sem, *, tm):
    @pl.when(pl.program_id(0) == 0)
    def _():
        zeros_ref[...] = jnp.zeros(zeros_ref.shape, F32)
        bm = zeros_ref.shape[0]
        tails = [xs_ref.at[pl.ds(pl.multiple_of(tail_ref[e], bm), bm)] for e in range(MOE_EXPERTS)]
        for dst in tails:
            pltpu.make_async_copy(zeros_ref, dst, sem).start()
        for dst in tails:
            pltpu.make_async_copy(zeros_ref, dst, sem).wait()

    def issue(t, carry):
        for j in range(2):
            d = dest_ref[2 * t + j]
            pltpu.make_async_copy(x_ref.at[pl.ds(t, 1)], xs_ref.at[pl.ds(d, 1)], sem).start(priority=j)
        return carry

    lax.fori_loop(0, tm, issue, 0, unroll=8)
    for _ in range(2):
        _row_copy_wait(x_ref, xs_ref.at[pl.ds(0, tm)], sem)


def _dispatch(tail_rows, dest_flat, x, n_rows):
    t, d = x.shape
    tm = ROW_TILE
    grid_spec = pltpu.PrefetchScalarGridSpec(
        num_scalar_prefetch=1,
        grid=(t // tm,),
        in_specs=[pl.BlockSpec((2 * tm,), lambda i, tail: (i,), memory_space=pltpu.SMEM),
                  pl.BlockSpec((tm, d), lambda i, tail: (i, 0))],
        out_specs=pl.BlockSpec(memory_space=pl.ANY),
        scratch_shapes=[pltpu.VMEM((MOE_BLOCK_ROWS, d), F32), pltpu.SemaphoreType.DMA(())],
    )
    return pl.pallas_call(
        functools.partial(_dispatch_kernel, tm=tm),
        grid_spec=grid_spec,
        out_shape=jax.ShapeDtypeStruct((n_rows, d), F32),
        compiler_params=_params("arbitrary"),
        name="moe_dispatch",
    )(tail_rows, dest_flat, x)


def _expert_kernel(blk_e_ref, nused_ref, xs_ref, wgu_ref, wd_ref, y_ref, wgu_bf, wd_bf):
    i = pl.program_id(0)
    prev = blk_e_ref[jnp.maximum(i - 1, 0)]
    new_expert = (i == 0) | (blk_e_ref[i] != prev)

    @pl.when(new_expert)
    def _():
        wgu_bf[...] = wgu_ref[...].astype(BF16)
        wd_bf[...] = wd_ref[...].astype(BF16)

    @pl.when(i < nused_ref[0])
    def _():
        hid = wd_bf.shape[0]
        xb = xs_ref[...].astype(BF16)
        gate = _dot(xb, wgu_bf[:, 0:hid])
        up = _dot(xb, wgu_bf[:, hid:2 * hid])
        y_ref[...] = _dot((_silu(gate) * up).astype(BF16), wd_bf[...])

    @pl.when(i >= nused_ref[0])
    def _():
        y_ref[...] = jnp.zeros(y_ref.shape, F32)


def _experts(blk_e, nused, xs, w_gate_up, w_down, layer):
    nr, d = xs.shape
    bm = MOE_BLOCK_ROWS
    hid2 = w_gate_up.shape[-1]
    hid = w_down.shape[-2]
    grid_spec = pltpu.PrefetchScalarGridSpec(
        num_scalar_prefetch=2,
        grid=(nr // bm,),
        in_specs=[pl.BlockSpec((bm, d), lambda i, be, nu: (jnp.minimum(i, nu[0] - 1), 0)),
                  pl.BlockSpec((None, None, d, hid2), lambda i, be, nu: (layer, be[i], 0, 0)),
                  pl.BlockSpec((None, None, hid, d), lambda i, be, nu: (layer, be[i], 0, 0))],
        out_specs=pl.BlockSpec((bm, d), lambda i, be, nu: (i, 0)),
        scratch_shapes=[pltpu.VMEM((d, hid2), BF16), pltpu.VMEM((hid, d), BF16)],
    )
    return pl.pallas_call(
        _expert_kernel,
        grid_spec=grid_spec,
        out_shape=jax.ShapeDtypeStruct((nr, d), F32),
        compiler_params=_params("arbitrary"),
        name="moe_experts",
    )(blk_e, nused, xs, w_gate_up, w_down)


def _combine_kernel(dest_ref, dest_next_ref, x_ref, route_ref, g_ref, b_ref, y_hbm, *rest, tm, n_first):
    if n_first is None:
        o_ref, ybuf, sems = rest
    else:
        o_ref, o2_ref, ybuf, sems = rest
    i = pl.program_id(0)
    n_tiles = pl.num_programs(0)

    def gather(d_ref, slot):
        def issue(t, carry):
            for j in range(2):
                d = d_ref[2 * t + j]
                pltpu.make_async_copy(y_hbm.at[pl.ds(d, 1)], ybuf.at[slot, j, pl.ds(t, 1)],
                                      sems.at[slot]).start(priority=j)
            return carry

        lax.fori_loop(0, tm, issue, 0, unroll=8)

    @pl.when(i == 0)
    def _():
        gather(dest_ref, 0)

    @pl.when(i + 1 < n_tiles)
    def _():
        gather(dest_next_ref, lax.rem(i + 1, 2))

    slot = lax.rem(i, 2)
    for j in range(2):
        _row_copy_wait(y_hbm.at[pl.ds(0, tm)], ybuf.at[slot, j], sems.at[slot])
    moe = route_ref[:, 2:3] * ybuf[slot, 0] + route_ref[:, 3:4] * ybuf[slot, 1]
    y = _layer_norm(ALPHA * x_ref[...] + moe, g_ref[...], b_ref[...])
    if n_first is None:
        o_ref[...] = y
    else:
        @pl.when(i < n_first)
        def _():
            o_ref[...] = y

        @pl.when(i >= n_first)
        def _():
            o2_ref[...] = y


def _combine(dest_flat, x, route, g, b, yb, split_rows=None):
    t, d = x.shape
    tm = ROW_TILE
    n_tiles = t // tm
    out_specs = _rows(tm, d)
    out_shape = jax.ShapeDtypeStruct((t, d), F32)
    n_first = None
    if split_rows is not None:
        n_first = split_rows // tm
        out_specs = [pl.BlockSpec((tm, d), lambda i: (jnp.minimum(i, n_first - 1), 0)),
                     pl.BlockSpec((tm, d), lambda i: (jnp.maximum(i - n_first, 0), 0))]
        out_shape = [jax.ShapeDtypeStruct((split_rows, d), F32), jax.ShapeDtypeStruct((t - split_rows, d), F32)]
    return pl.pallas_call(
        functools.partial(_combine_kernel, tm=tm, n_first=n_first),
        grid=(n_tiles,),
        in_specs=[pl.BlockSpec((2 * tm,), lambda i: (i,), memory_space=pltpu.SMEM),
                  pl.BlockSpec((2 * tm,), lambda i: (jnp.minimum(i + 1, n_tiles - 1),), memory_space=pltpu.SMEM),
                  _rows(tm, d), _rows(tm, LANES), _full((1, d)), _full((1, d)),
                  pl.BlockSpec(memory_space=pl.ANY)],
        out_specs=out_specs,
        out_shape=out_shape,
        scratch_shapes=[pltpu.VMEM((2, 2, tm, d), F32), pltpu.SemaphoreType.DMA((2,))],
        compiler_params=_params("arbitrary"),
        name="moe_combine",
    )(dest_flat, dest_flat, x, route, g, b, yb)


def _moe_plan(route, counts):
    bm = MOE_BLOCK_ROWS
    t = route.shape[0]
    n_blocks = -(-2 * t // bm) + MOE_EXPERTS
    e_idx = route[:, 0:2].astype(jnp.int32)
    rank = route[:, 4:6].astype(jnp.int32)
    cnt = counts[0, ROUTE_LANE0:ROUTE_LANE0 + MOE_EXPERTS].astype(jnp.int32)
    padded = ((cnt + bm - 1) // bm) * bm
    pad_end = jnp.cumsum(padded)
    pad_start = pad_end - padded
    experts = jnp.arange(MOE_EXPERTS, dtype=jnp.int32)
    dest = (rank + jnp.sum(jnp.where(e_idx[..., None] == experts, pad_start, 0), axis=-1)).reshape(-1)
    nused = pad_end[-1] // bm
    blk = jnp.arange(n_blocks, dtype=jnp.int32)
    blk_e = jnp.sum((pad_end[None, :] <= (blk * bm)[:, None]).astype(jnp.int32), axis=1)
    blk_e = jnp.minimum(blk_e, MOE_EXPERTS - 1)
    last_e = jnp.max(jnp.where(cnt > 0, experts, 0))
    blk_e = jnp.where(blk < nused, blk_e, last_e)
    empty = cnt == 0
    tail_rows = jnp.where(empty, (nused + jnp.cumsum(empty.astype(jnp.int32)) - 1) * bm, pad_end - bm)
    return (dest.astype(jnp.int32), blk_e.astype(jnp.int32), nused.reshape(1).astype(jnp.int32),
            tail_rows.astype(jnp.int32), n_blocks)


def _hi_lo(w):
    hi = w.astype(BF16)
    return hi, (w - hi.astype(F32)).astype(BF16)


def kernel(x_prompt, x_sample, mem_prompt, cache_mem_k, cache_mem_v, state_gla, state_hgrn, state_conv,
           ln_g, ln_b, a_w_in, a_b_in, a_ln_g, a_ln_b, a_w_s, a_b_s, a_w_out, a_b_out,
           b_w_in, b_w_g2, b_b_g, b_norm_g, b_w_out, c_lb, c_w_in, c_norm_g, c_w_out,
           d_w_in, d_b_in, d_w_dw, d_b_dw, d_ln_g, d_ln_b, d_w_out, d_b_out,
           m_w_q, m_w_k, m_w_v, m_w_o, r_w_grp, r_b_grp, r_w_exp, r_b_exp, e_w_gate_up, e_w_down):
    bp, lp, d = x_prompt.shape
    bs, ls, _ = x_sample.shape
    tp, ts = bp * lp, bs * ls
    t = tp + ts
    mem_len = mem_prompt.shape[1]

    row = lambda a: a.reshape(1, -1)
    x = None

    mem2d = mem_prompt.reshape(bp * mem_len, d)
    mem_k, mem_k_flat = _mem_proj(mem2d, m_w_k.astype(BF16), mem_len)
    mem_v, mem_v_flat = _mem_proj(mem2d, m_w_v.astype(BF16), mem_len)

    cache_k = _cache_rows(cache_mem_k)
    cache_v = _cache_rows(cache_mem_v)

    lb_all = jnp.cumsum(jax.nn.softmax(c_lb.astype(F32), axis=0), axis=0)
    lb_all = lb_all - lb_all[:1]
    ltri = jnp.tril(jnp.ones((ROW_TILE, ROW_TILE), F32), -1).astype(BF16)
    zero_bias = jnp.zeros((1, d), F32)

    outs = {"v": [], "gla_p": [], "gla_s": [], "hgrn_p": [], "hgrn_s": [], "conv_p": [], "conv_s": []}
    for i in range(DEPTH):
        j = i // N_MIXERS
        kind = i % N_MIXERS
        g1, b1 = row(ln_g[i, 0]), row(ln_b[i, 0])
        if kind == 0:
            tril = jnp.tril(jnp.ones((A_CHUNK, A_CHUNK), bool))
            wc_p = jnp.where(tril, a_w_s[j], 0.0).astype(BF16)
            bc_p = a_b_s[j][:, :, None]
            reps = A_CHUNK // ls
            small = jnp.where(jnp.tril(jnp.ones((ls, ls), bool)), a_w_s[j][:, :ls, :ls], 0.0)
            wc_s = jax.vmap(lambda m: jnp.kron(jnp.eye(reps, dtype=F32), m))(small).astype(BF16)
            bc_s = jnp.tile(a_b_s[j][:, :ls], (1, reps))[:, :, None]
            common = (a_w_in[j].astype(BF16), row(a_b_in[j]), row(a_ln_g[j]), row(a_ln_b[j]))
            tail = (a_w_out[j].astype(BF16), row(a_b_out[j]), g1, b1)
            xp_in = x_prompt.reshape(tp, d) if x is None else x[:tp]
            xs_in = x_sample.reshape(ts, d) if x is None else x[tp:]
            x1 = _gmlp(xp_in, 0, t, *common, wc_p, bc_p, *tail)
            x1, v_s = _gmlp(xs_in, tp, t, *common, wc_s, bc_s, *tail, alias=x1)
            outs["v"].append(v_s.reshape(bs, ls, -1))
        elif kind == 1:
            dk, dv = b_w_g2.shape[-1], b_w_out.shape[1]
            w_in = b_w_in[j]
            w_main = w_in[:, :2 * dk + 2 * dv].astype(BF16)
            w_low = jnp.pad(w_in[:, 2 * dk + 2 * dv:], ((0, 0), (0, LANES - B_GATE_RANK))).astype(BF16)
            g2_hi, g2_lo = _hi_lo(jnp.pad(b_w_g2[j], ((0, LANES - B_GATE_RANK), (0, 0))))
            proj_w = (w_main, w_low, g2_hi, g2_lo, row(b_b_g[j]))
            proj = functools.partial(_gla_proj_kernel, dk=dk, dv=dv, q_scale=(dk // B_HEADS) ** -0.5)
            ng = row(b_norm_g[j])
            w_out = b_w_out[j].astype(BF16)
            s0_p = jnp.zeros((bp,) + state_gla.shape[2:], F32)
            x1, s_p = _linrec_block(x, proj, proj_w, ng, s0_p, w_out, g1, b1, n_batch=bp, seq=lp, heads=B_HEADS,
                                    hk=dk, hv=dv, name="gla_block_prompt")
            q, la, k, v, r = _gla_proj(x, proj_w, dk, dv, tp, ts)
            o_s, s_s = _linrec(q, k, la, v, r, ng, state_gla[j], n_batch=bs, seq=ls, heads=B_HEADS)
            x1 = _mm_ln(o_s, w_out, zero_bias, x, g1, b1, "gla_out_sample", row_off=tp, fill=(x1,))
            outs["gla_p"].append(s_p)
            outs["gla_s"].append(s_s)
        elif kind == 2:
            heads = state_hgrn.shape[2]
            proj_w = (c_w_in[j].astype(BF16), row(lb_all[i]))
            proj = functools.partial(_hgrn_proj_kernel, d=d, q_scale=C_EXPAND ** -0.5)
            ng = row(c_norm_g[j])
            w_out = c_w_out[j].astype(BF16)
            s0_p = jnp.zeros((bp,) + state_hgrn.shape[2:], F32)
            x1, s_p = _linrec_block(x, proj, proj_w, ng, s0_p, w_out, g1, b1, n_batch=bp, seq=lp, heads=heads,
                                    hk=d, hv=d, name="hgrn_block_prompt")
            q, lf, k, v, gt = _hgrn_proj(x, proj_w, tp, ts)
            o_s, s_s = _linrec(q, k, lf, v, gt, ng, state_hgrn[j], n_batch=bs, seq=ls, heads=heads)
            x1 = _mm_ln(o_s, w_out, zero_bias, x, g1, b1, "hgrn_out_sample", row_off=tp, fill=(x1,))
            outs["hgrn_p"].append(s_p)
            outs["hgrn_s"].append(s_s)
        else:
            w_in, b_in = d_w_in[j].astype(BF16), row(d_b_in[j])
            w_out, b_out = d_w_out[j].astype(BF16), row(d_b_out[j])
            cargs = (d_w_dw[j], row(d_b_dw[j]), row(d_ln_g[j]), row(d_ln_b[j]))
            conv0 = jnp.zeros((bp, D_BUF, d), F32)
            x1, s_p = _conf_block(x, w_in, b_in, conv0, *cargs, w_out, b_out, g1, b1, n_batch=bp, seq=lp)
            h_s = _glu(x, w_in, b_in, tp, ts)
            c_s, s_s = _conv(h_s, state_conv[j], *cargs, n_batch=bs, seq=ls)
            x1 = _mm_ln(c_s, w_out, b_out, x, g1, b1, "conf_out_sample", row_off=tp, fill=(x1,))
            outs["conv_p"].append(s_p)
            outs["conv_s"].append(s_s)

        w_route = jnp.concatenate([r_w_grp[i], r_w_exp[i]], axis=1)
        w_route = jnp.pad(w_route, ((0, 0), (0, LANES - w_route.shape[1])))
        b_route = jnp.pad(jnp.concatenate([r_b_grp[i], r_b_exp[i]]), (0, LANES - MOE_GROUPS - MOE_EXPERTS))
        router = _hi_lo(w_route)
        w_q, w_o = m_w_q[i].astype(BF16), m_w_o[i].astype(BF16)
        g2, b2 = row(ln_g[i, 1]), row(ln_b[i, 1])
        x2, logits = _attn_block_prompt(x1, w_q, mem_k, mem_v, w_o, g2, b2, router, i, bp, lp)
        q_s = _mm_rows(x1, w_q, "attn_q_sample", tp, ts)
        att_s = _attn_sample(q_s, cache_k, cache_v, i, bs, ls)
        x2, logits = _mm_ln(att_s, w_o, zero_bias, x1, g2, b2, "attn_out_sample", router=router, row_off=tp,
                            fill=(x2, logits))

        route, counts = _route(logits, row(b_route), ltri)
        dest, blk_e, nused, tail_rows, n_blocks = _moe_plan(route, counts)
        xs = _dispatch(tail_rows, dest, x2, n_blocks * MOE_BLOCK_ROWS)
        yb = _experts(blk_e, nused, xs, e_w_gate_up, e_w_down, i)
        if i + 1 < DEPTH:
            x = _combine(dest, x2, route, row(ln_g[i, 2]), row(ln_b[i, 2]), yb)
        else:
            y_p, y_s = _combine(dest, x2, route, row(ln_g[i, 2]), row(ln_b[i, 2]), yb, split_rows=tp)

    y_prompt = y_p.reshape(bp, lp, d)
    y_sample = y_s.reshape(bs, ls, d)
    mem_k_p = _cache_unrows(mem_k_flat, MEM_HEADS)
    mem_v_p = _cache_unrows(mem_v_flat, MEM_HEADS)
    return (y_prompt, y_sample, mem_k_p, mem_v_p, jnp.stack(outs["gla_p"]), jnp.stack(outs["hgrn_p"]),
            jnp.stack(outs["conv_p"]), jnp.stack(outs["v"]), jnp.stack(outs["gla_s"]),
            jnp.stack(outs["hgrn_s"]), jnp.stack(outs["conv_s"]))
```

```python
import functools
import math

import jax
import jax.numpy as jnp
from jax import lax
from jax.experimental import pallas as pl
from jax.experimental.pallas import tpu as pltpu

F32 = jnp.float32
BF16 = jnp.bfloat16

D_MODEL = 1024
DEPTH = 4
N_MIXERS = 4
ALPHA = (2.0 * DEPTH) ** 0.25
LN_EPS = 1e-5
A_CHUNK = 128
A_GROUPS = 4
B_HEADS = 4
B_GATE_RANK = 16
B_GATE_TAU = 16.0
C_EXPAND = 128
D_CONV_W = 31
D_BUF = D_CONV_W - 1
LIN_CHUNK = 32
MEM_HEADS = 4
MOE_GROUPS = 4
MOE_PER_GROUP = 8
MOE_EXPERTS = MOE_GROUPS * MOE_PER_GROUP
MOE_HIDDEN = 512

LANES = 128
SUBLANES = 8
ROW_TILE = 512
MOE_BLOCK_ROWS = 512
ROUTE_LANE0 = MOE_GROUPS
VMEM_LIMIT = 56 * 1024 * 1024
INV_SQRT2 = 1.0 / math.sqrt(2.0)


def _params(*sem, vmem=VMEM_LIMIT):
    return pltpu.CompilerParams(dimension_semantics=sem, vmem_limit_bytes=vmem)


def _dot(a, b):
    return jnp.dot(a, b, preferred_element_type=F32)


def _dot_nt(a, b):
    return lax.dot_general(a, b, (((1,), (1,)), ((), ())), preferred_element_type=F32)


def _dot_hi(a, w_hi, w_lo):
    a_hi = a.astype(BF16)
    a_lo = (a - a_hi.astype(F32)).astype(BF16)
    return _dot(a_hi, w_hi) + _dot(a_lo, w_hi) + _dot(a_hi, w_lo)


def _layer_norm(x, g, b):
    mu = jnp.mean(x, axis=-1, keepdims=True)
    xc = x - mu
    var = jnp.mean(xc * xc, axis=-1, keepdims=True)
    return xc * lax.rsqrt(var + LN_EPS) * g + b


def _sigmoid(x):
    return 1.0 / (1.0 + jnp.exp(-x))


def _silu(x):
    return x * _sigmoid(x)


def _gelu(x):
    return 0.5 * x * (1.0 + lax.erf(x * INV_SQRT2))


def _log_sigmoid(x):
    return jnp.minimum(x, 0.0) - jnp.log(1.0 + jnp.exp(-jnp.abs(x)))


def _full(shape):
    return pl.BlockSpec(shape, lambda *_: (0,) * len(shape))


def _rows(tm, n, off=0):
    return pl.BlockSpec((tm, n), lambda i: (i + off, 0))


def _mm_kernel(a_ref, w_ref, o_ref):
    o_ref[...] = _dot(a_ref[...].astype(BF16), w_ref[...])


def _mem_proj_kernel(a_ref, w_ref, nat_ref, flat_ref, *, mem_len):
    res = _dot(a_ref[...].astype(BF16), w_ref[...])
    nat_ref[...] = res
    n_lt = res.shape[1] // (MEM_HEADS * LANES)
    for b in range(res.shape[0] // mem_len):
        for h in range(MEM_HEADS):
            for lt in range(n_lt):
                c0 = (h * n_lt + lt) * LANES
                flat_ref[b, pl.ds(lt * MEM_HEADS + h, mem_len, stride=MEM_HEADS * n_lt), :] = (
                    res[b * mem_len:(b + 1) * mem_len, c0:c0 + LANES])


def _mem_proj(a, w, mem_len):
    m, k = a.shape
    nl, _, n = w.shape
    tm = min(m, ROW_TILE)
    nbat = tm // mem_len
    return pl.pallas_call(
        functools.partial(_mem_proj_kernel, mem_len=mem_len),
        grid=(nl, m // tm),
        in_specs=[pl.BlockSpec((tm, k), lambda l, i: (i, 0)),
                  pl.BlockSpec((None, k, n), lambda l, i: (l, 0, 0))],
        out_specs=[pl.BlockSpec((None, tm, n), lambda l, i: (l, i, 0)),
                   pl.BlockSpec((None, nbat, mem_len * n // LANES, LANES), lambda l, i: (l, i, 0, 0))],
        out_shape=[jax.ShapeDtypeStruct((nl, m, n), F32),
                   jax.ShapeDtypeStruct((nl, m // mem_len, mem_len * n // LANES, LANES), F32)],
        compiler_params=_params("parallel", "parallel"),
        name="mem_kv_proj",
    )(a, w)


def _mm_rows(a, w, name, row_off, n_rows):
    k = a.shape[1]
    n = w.shape[1]
    tm = ROW_TILE
    return pl.pallas_call(
        _mm_kernel,
        grid=(n_rows // tm,),
        in_specs=[_rows(tm, k, row_off // tm), _full((k, n))],
        out_specs=_rows(tm, n),
        out_shape=jax.ShapeDtypeStruct((n_rows, n), F32),
        compiler_params=_params("parallel"),
        name=name,
    )(a, w)


def _mm_ln_kernel(a_ref, w_ref, bias_ref, x_ref, g_ref, b_ref, *rest, with_logits, n_alias):
    h = _dot(a_ref[...].astype(BF16), w_ref[...]) + bias_ref[...]
    y = _layer_norm(ALPHA * x_ref[...] + h, g_ref[...], b_ref[...])
    if with_logits:
        wr_hi_ref, wr_lo_ref = rest[:2]
        o_ref, lg_ref = rest[2 + n_alias:]
        lg_ref[...] = _dot_hi(y, wr_hi_ref[...], wr_lo_ref[...])
    else:
        (o_ref,) = rest[n_alias:]
    o_ref[...] = y


def _mm_ln(a, w, bias, x, g, b, name, router=None, row_off=0, fill=None):
    m, k = a.shape
    t, d = x.shape
    tm = ROW_TILE
    off = row_off // tm
    in_specs = [_rows(tm, k), _full((k, d)), _full((1, d)), _rows(tm, d, off), _full((1, d)), _full((1, d))]
    args = [a, w, bias, x, g, b]
    out_specs = [_rows(tm, d, off)]
    out_shape = [jax.ShapeDtypeStruct((t, d), F32)]
    if router is not None:
        in_specs += [_full((d, LANES)), _full((d, LANES))]
        args += list(router)
        out_specs.append(_rows(tm, LANES, off))
        out_shape.append(jax.ShapeDtypeStruct((t, LANES), F32))
    aliases = {}
    for n, arr in enumerate(fill or ()):
        in_specs.append(pl.BlockSpec(memory_space=pl.ANY))
        args.append(arr)
        aliases[len(args) - 1] = n
    res = pl.pallas_call(
        functools.partial(_mm_ln_kernel, with_logits=router is not None, n_alias=len(aliases)),
        grid=(m // tm,),
        in_specs=in_specs,
        out_specs=out_specs,
        out_shape=out_shape,
        input_output_aliases=aliases,
        compiler_params=_params("parallel"),
        name=name,
    )(*args)
    return res if router is not None else res[0]


def _gmlp_kernel(x_ref, w_in_ref, b_in_ref, lng_ref, lnb_ref, wc_ref, bc_ref, w_out_ref, b_out_ref,
                 g_ref, b_ref, *rest, emit_v, n_chunks):
    if emit_v:
        _alias_ref, o_ref, v_ref, vn_ref = rest
    else:
        o_ref, vn_ref = rest
    half = w_out_ref.shape[0]
    gw = half // A_GROUPS
    x = x_ref[...]
    xb = x.astype(BF16)
    v = _gelu(_dot(xb, w_in_ref[:, half:]) + b_in_ref[:, half:])
    vn = _layer_norm(v, lng_ref[...], lnb_ref[...])
    vn_ref[...] = vn
    if emit_v:
        v_ref[...] = vn
    acc = jnp.zeros(x.shape, F32)
    for grp in range(A_GROUPS):
        cols = slice(grp * gw, (grp + 1) * gw)
        u = _gelu(_dot(xb, w_in_ref[:, cols]) + b_in_ref[:, cols])
        mixed = []
        for c in range(n_chunks):
            vc = vn_ref[c * A_CHUNK:(c + 1) * A_CHUNK, cols].astype(BF16)
            mixed.append(_dot(wc_ref[grp], vc) + bc_ref[grp])
        mixed = mixed[0] if n_chunks == 1 else jnp.concatenate(mixed, axis=0)
        acc = acc + _dot((u * mixed).astype(BF16), w_out_ref[cols, :])
    h = acc + b_out_ref[...]
    o_ref[...] = _layer_norm(ALPHA * x + h, g_ref[...], b_ref[...])


GMLP_CHUNKS = 4


def _gmlp(x, row_off, t, w_in, b_in, lng, lnb, wc, bc, w_out, b_out, g, b, alias=None):
    n_rows, d = x.shape
    ffn = w_in.shape[1]
    half = ffn // 2
    tm = GMLP_CHUNKS * A_CHUNK
    emit_v = alias is not None
    off = row_off // tm
    once = lambda shape: pl.BlockSpec(shape, lambda *_: (0,) * len(shape), pipeline_mode=pl.Buffered(1))
    in_specs = [_rows(tm, d), once((d, ffn)), _full((1, ffn)), _full((1, half)), _full((1, half)),
                _full((A_GROUPS, A_CHUNK, A_CHUNK)), _full((A_GROUPS, A_CHUNK, 1)), once((half, d)),
                _full((1, d)), _full((1, d)), _full((1, d))]
    args = [x, w_in, b_in, lng, lnb, wc, bc, w_out, b_out, g, b]
    out_specs = _rows(tm, d, off)
    out_shape = jax.ShapeDtypeStruct((t, d), F32)
    aliases = {}
    if emit_v:
        in_specs.append(pl.BlockSpec(memory_space=pl.ANY))
        args.append(alias)
        aliases = {len(args) - 1: 0}
        out_specs = [out_specs, _rows(tm, half)]
        out_shape = [out_shape, jax.ShapeDtypeStruct((n_rows, half), F32)]
    return pl.pallas_call(
        functools.partial(_gmlp_kernel, emit_v=emit_v, n_chunks=tm // A_CHUNK),
        grid=(n_rows // tm,),
        in_specs=in_specs,
        out_specs=out_specs,
        out_shape=out_shape,
        scratch_shapes=[pltpu.VMEM((tm, half), F32)],
        input_output_aliases=aliases,
        compiler_params=_params("parallel"),
        name="gmlp_sample" if emit_v else "gmlp_prompt",
    )(*args)


def _gla_proj_kernel(x_ref, w_ref, wlow_ref, g2_hi_ref, g2_lo_ref, bg_ref,
                     q_ref, la_ref, k_ref, v_ref, r_ref, *, dk, dv, q_scale):
    xb = x_ref[...].astype(BF16)
    q_ref[...] = _dot(xb, w_ref[:, 0:dk]) * q_scale
    k_ref[...] = _dot(xb, w_ref[:, dk:2 * dk])
    v_ref[...] = _dot(xb, w_ref[:, 2 * dk:2 * dk + dv])
    r_ref[...] = _dot(xb, w_ref[:, 2 * dk + dv:2 * dk + 2 * dv])
    g_low = _dot(xb, wlow_ref[...])
    pre = _dot_hi(g_low, g2_hi_ref[...], g2_lo_ref[...]) + bg_ref[...]
    la_ref[...] = _log_sigmoid(pre) * (1.0 / B_GATE_TAU)


def _gla_proj(x, weights, dk, dv, row_off, n_rows):
    d = x.shape[1]
    tm = ROW_TILE
    shapes = [dk, dk, dk, dv, dv]
    return pl.pallas_call(
        functools.partial(_gla_proj_kernel, dk=dk, dv=dv, q_scale=(dk // B_HEADS) ** -0.5),
        grid=(n_rows // tm,),
        in_specs=[_rows(tm, d, row_off // tm)] + [_full(w.shape) for w in weights],
        out_specs=[_rows(tm, n) for n in shapes],
        out_shape=[jax.ShapeDtypeStruct((n_rows, n), F32) for n in shapes],
        compiler_params=_params("parallel"),
        name="gla_proj",
    )(x, *weights)


def _hgrn_proj_kernel(x_ref, w_ref, lb_ref, q_ref, lf_ref, k_ref, v_ref, gt_ref, *, d, q_scale):
    xb = x_ref[...].astype(BF16)
    lb = lb_ref[...]
    q_ref[...] = _silu(_dot(xb, w_ref[:, 0:d])) * q_scale
    f = _dot(xb, w_ref[:, d:2 * d])
    lf_ref[...] = jnp.log(lb + (1.0 - lb) * _sigmoid(f))
    k_ref[...] = (1.0 - lb) * _sigmoid(-f)
    v_ref[...] = _dot(xb, w_ref[:, 2 * d:3 * d])
    gt_ref[...] = _dot(xb, w_ref[:, 3 * d:4 * d])


def _hgrn_proj(x, weights, row_off, n_rows):
    d = x.shape[1]
    tm = ROW_TILE
    return pl.pallas_call(
        functools.partial(_hgrn_proj_kernel, d=d, q_scale=C_EXPAND ** -0.5),
        grid=(n_rows // tm,),
        in_specs=[_rows(tm, d, row_off // tm)] + [_full(w.shape) for w in weights],
        out_specs=[_rows(tm, d)] * 5,
        out_shape=[jax.ShapeDtypeStruct((n_rows, d), F32)] * 5,
        compiler_params=_params("parallel"),
        name="hgrn_proj",
    )(x, *weights)


def _cumsum_rows(x, chunk):
    pos = lax.broadcasted_iota(jnp.int32, x.shape, 0) & (chunk - 1)
    step = 1
    while step < chunk:
        x = x + jnp.where(pos >= step, pltpu.roll(x, step, axis=0), 0.0)
        step *= 2
    return x


def _chunk_rows(x, row, chunk, n_chunks):
    parts = [jnp.broadcast_to(x[c * chunk + row:c * chunk + row + 1, :], (chunk, x.shape[1]))
             for c in range(n_chunks)]
    return parts[0] if n_chunks == 1 else jnp.concatenate(parts, axis=0)


def _linrec_mask(tl, chunk):
    r_id = lax.broadcasted_iota(jnp.int32, (tl, tl), 0)
    c_id = lax.broadcasted_iota(jnp.int32, (tl, tl), 1)
    return (r_id >= c_id) & ((r_id // chunk) == (c_id // chunk))


def _linrec_heads(q_ref, k_ref, g_ref, v_ref, gate_ref, ng_ref, o_ref, st_ref, rs, *, heads, dk, dv, chunk, n_chunks):
    mask = _linrec_mask(chunk * n_chunks, chunk)
    for h in range(heads):
        ks = slice(h * dk, (h + 1) * dk)
        vs = slice(h * dv, (h + 1) * dv)
        bcum = _cumsum_rows(g_ref[rs, ks], chunk)
        b_mid = _chunk_rows(bcum, chunk // 2 - 1, chunk, n_chunks)
        b_end = _chunk_rows(bcum, chunk - 1, chunk, n_chunks)
        q = q_ref[rs, ks]
        k = k_ref[rs, ks]
        vb = v_ref[rs, vs].astype(BF16)
        q_in = (q * jnp.exp(bcum)).astype(BF16)
        q_a = (q * jnp.exp(bcum - b_mid)).astype(BF16)
        k_a = (k * jnp.exp(b_mid - bcum)).astype(BF16)
        k_end = k * jnp.exp(b_end - bcum)
        a = jnp.where(mask, _dot_nt(q_a, k_a), 0.0).astype(BF16)
        o_intra = _dot(a, vb)
        st = st_ref[h]
        outs = []
        for c in range(n_chunks):
            cr = slice(c * chunk, (c + 1) * chunk)
            outs.append(o_intra[cr] + _dot(q_in[cr], st.astype(BF16)))
            decay = jnp.exp(bcum[cr].T[:, chunk - 1:chunk])
            st = st * decay + _dot(k_end[cr].T.astype(BF16), vb[cr])
        st_ref[h] = st
        o = outs[0] if n_chunks == 1 else jnp.concatenate(outs, axis=0)
        o = o * lax.rsqrt(jnp.mean(o * o, axis=-1, keepdims=True) + LN_EPS) * ng_ref[:, vs]
        o_ref[rs, vs] = o * _silu(gate_ref[rs, vs])


def _linrec_kernel(q_ref, k_ref, g_ref, v_ref, gate_ref, ng_ref, s0_ref, o_ref, sout_ref, st_ref, *, nb, tl, **dims):
    li = pl.program_id(1)
    for n in range(nb):
        @pl.when(li == 0)
        def _():
            st_ref[...] = s0_ref[n]

        _linrec_heads(q_ref, k_ref, g_ref, v_ref, gate_ref, ng_ref, o_ref, st_ref, slice(n * tl, (n + 1) * tl),
                      **dims)
        sout_ref[n] = st_ref[...]


LINREC_CHUNK = 2 * LIN_CHUNK
LINREC_SAMPLE_NB = 4
LINREC_BLOCK_SUBTILES = 2


def _linrec_tiling(seq):
    chunk = LINREC_CHUNK if seq % LINREC_CHUNK == 0 else seq
    tl = min(seq, 4 * chunk)
    return chunk, tl, seq // tl


def _linrec(q, k, g, v, gate, ng, s0, *, n_batch, seq, heads):
    t, hk = q.shape
    hv = v.shape[1]
    dk, dv = hk // heads, hv // heads
    chunk, tl, n_l = _linrec_tiling(seq)
    nb = LINREC_SAMPLE_NB if n_l == 1 else 1

    def rows(n):
        return pl.BlockSpec((nb * tl, n), lambda b, l: (b * n_l + l, 0))

    state_spec = pl.BlockSpec((nb, heads, dk, dv), lambda b, l: (b, 0, 0, 0))
    return pl.pallas_call(
        functools.partial(_linrec_kernel, nb=nb, tl=tl, heads=heads, dk=dk, dv=dv, chunk=chunk,
                          n_chunks=tl // chunk),
        grid=(n_batch // nb, n_l),
        in_specs=[rows(hk), rows(hk), rows(hk), rows(hv), rows(hv), pl.BlockSpec((1, hv), lambda b, l: (0, 0)),
                  state_spec],
        out_specs=[rows(hv), state_spec],
        out_shape=[jax.ShapeDtypeStruct((t, hv), F32), jax.ShapeDtypeStruct((n_batch, heads, dk, dv), F32)],
        scratch_shapes=[pltpu.VMEM((heads, dk, dv), F32)],
        compiler_params=_params("parallel", "arbitrary"),
        name="linrec_h%d" % heads,
    )(q, k, g, v, gate, ng, s0)


def _linrec_block_kernel(*refs, proj, n_proj_w, **dims):
    x_ref = refs[0]
    proj_w = refs[1:1 + n_proj_w]
    ng_ref, s0_ref, wo_ref, g_ref, b_ref, o_ref, sout_ref = refs[1 + n_proj_w:8 + n_proj_w]
    q_s, g_s, k_s, v_s, gate_s, o_s, st_ref = refs[8 + n_proj_w:]

    @pl.when(pl.program_id(1) == 0)
    def _():
        st_ref[...] = s0_ref[0]

    proj(x_ref, *proj_w, q_s, g_s, k_s, v_s, gate_s)
    sub = dims["chunk"] * dims["n_chunks"]
    for r0 in range(0, x_ref.shape[0], sub):
        _linrec_heads(q_s, k_s, g_s, v_s, gate_s, ng_ref, o_s, st_ref, slice(r0, r0 + sub), **dims)
    h = _dot(o_s[...].astype(BF16), wo_ref[...])
    o_ref[...] = _layer_norm(ALPHA * x_ref[...] + h, g_ref[...], b_ref[...])
    sout_ref[0] = st_ref[...]


def _linrec_block(x, proj, proj_w, ng, s0, w_out, g, b, *, n_batch, seq, heads, hk, hv, name):
    t, d = x.shape
    dk, dv = hk // heads, hv // heads
    chunk, sub, n_sub = _linrec_tiling(seq)
    per_step = LINREC_BLOCK_SUBTILES if n_sub % LINREC_BLOCK_SUBTILES == 0 else 1
    tl, n_l = sub * per_step, n_sub // per_step
    rows = pl.BlockSpec((tl, d), lambda bi, l: (bi * n_l + l, 0))
    full = lambda shape: pl.BlockSpec(shape, lambda bi, l: (0,) * len(shape))
    state_spec = pl.BlockSpec((1, heads, dk, dv), lambda bi, l: (bi, 0, 0, 0))
    return pl.pallas_call(
        functools.partial(_linrec_block_kernel, proj=proj, n_proj_w=len(proj_w), heads=heads, dk=dk, dv=dv,
                          chunk=chunk, n_chunks=sub // chunk),
        grid=(n_batch, n_l),
        in_specs=[rows] + [full(w.shape) for w in proj_w] + [full((1, hv)), state_spec, full(w_out.shape),
                                                            full((1, d)), full((1, d))],
        out_specs=[rows, state_spec],
        out_shape=[jax.ShapeDtypeStruct((t, d), F32), jax.ShapeDtypeStruct((n_batch, heads, dk, dv), F32)],
        scratch_shapes=[pltpu.VMEM((tl, hk), F32), pltpu.VMEM((tl, hk), F32), pltpu.VMEM((tl, hk), F32),
                        pltpu.VMEM((tl, hv), F32), pltpu.VMEM((tl, hv), F32), pltpu.VMEM((tl, hv), F32),
                        pltpu.VMEM((heads, dk, dv), F32)],
        compiler_params=_params("parallel", "arbitrary"),
        name=name,
    )(x, *proj_w, ng, s0, w_out, g, b)


def _glu_kernel(x_ref, w_ref, b_ref, o_ref, *, d):
    xb = x_ref[...].astype(BF16)
    a = _dot(xb, w_ref[:, 0:d]) + b_ref[:, 0:d]
    gate = _dot(xb, w_ref[:, d:2 * d]) + b_ref[:, d:2 * d]
    o_ref[...] = a * _sigmoid(gate)


def _glu(x, w, b, row_off, n_rows):
    d = x.shape[1]
    tm = ROW_TILE
    return pl.pallas_call(
        functools.partial(_glu_kernel, d=d),
        grid=(n_rows // tm,),
        in_specs=[_rows(tm, d, row_off // tm), _full(w.shape), _full(b.shape)],
        out_specs=_rows(tm, d),
        out_shape=jax.ShapeDtypeStruct((n_rows, d), F32),
        compiler_params=_params("parallel"),
        name="conf_glu",
    )(x, w, b)


CONV_PAD = 32


CONV_LEAD = CONV_PAD - D_BUF


def _conv_window_step(buf_ref, h, state_of, wdw_ref, bdw_ref, lg_ref, lb_ref, conv_ref, first, tl):
    d = h.shape[1]
    rb = min(tl, 64)
    cw = LANES
    buf_ref[CONV_PAD + tl:CONV_PAD + tl + SUBLANES, :] = jnp.zeros((SUBLANES, d), F32)

    @pl.when(first)
    def _():
        buf_ref[CONV_LEAD:CONV_PAD, :] = state_of()

    buf_ref[CONV_PAD:CONV_PAD + tl, :] = h
    for r0 in range(0, tl, rb):
        for c0 in range(0, d, cw):
            cols = slice(c0, c0 + cw)
            acc = jnp.zeros((rb, cw), F32)
            for s in range(SUBLANES):
                part = None
                for a in range((CONV_PAD + SUBLANES) // SUBLANES):
                    j = SUBLANES * a + s - CONV_LEAD
                    if 0 <= j < D_CONV_W:
                        rows = slice(r0 + SUBLANES * a, r0 + SUBLANES * a + rb + SUBLANES)
                        term = wdw_ref[j:j + 1, cols] * buf_ref[rows, cols]
                        part = term if part is None else part + term
                acc = acc + part[s:s + rb, :]
            conv_ref[r0:r0 + rb, cols] = acc + bdw_ref[:, cols]
    y = _silu(_layer_norm(conv_ref[...], lg_ref[...], lb_ref[...]))
    new_state = buf_ref[tl + CONV_LEAD:tl + CONV_PAD, :]
    buf_ref[0:CONV_PAD, :] = buf_ref[tl:tl + CONV_PAD, :]
    return y, new_state


def _conv_kernel(h_ref, st_ref, wdw_ref, bdw_ref, lg_ref, lb_ref, o_ref, sout_ref, buf_ref, conv_ref, *, nb, tl):
    first = pl.program_id(1) == 0
    for n in range(nb):
        y, new_state = _conv_window_step(buf_ref, h_ref[n * tl:(n + 1) * tl, :], lambda: st_ref[n], wdw_ref,
                                         bdw_ref, lg_ref, lb_ref, conv_ref, first, tl)
        o_ref[n * tl:(n + 1) * tl, :] = y
        sout_ref[n] = new_state


CONV_TILE = 256


def _conv(h, state, wdw, bdw, lg, lb, *, n_batch, seq):
    t, d = h.shape
    tl = min(seq, CONV_TILE)
    n_l = seq // tl
    nb = 8 if n_l == 1 else 1
    rows = pl.BlockSpec((nb * tl, d), lambda b, l: (b * n_l + l, 0))
    state_spec = pl.BlockSpec((nb, D_BUF, d), lambda b, l: (b, 0, 0))
    return pl.pallas_call(
        functools.partial(_conv_kernel, nb=nb, tl=tl),
        grid=(n_batch // nb, n_l),
        in_specs=[rows, state_spec, _full(wdw.shape), _full((1, d)), _full((1, d)), _full((1, d))],
        out_specs=[rows, state_spec],
        out_shape=[jax.ShapeDtypeStruct((t, d), F32), jax.ShapeDtypeStruct((n_batch, D_BUF, d), F32)],
        scratch_shapes=[pltpu.VMEM((CONV_PAD + tl + SUBLANES, d), F32), pltpu.VMEM((tl, d), F32)],
        compiler_params=_params("parallel", "arbitrary"),
        name="conf_conv",
    )(h, state, wdw, bdw, lg, lb)


def _conf_block_kernel(x_ref, win_ref, bin_ref, st_ref, wdw_ref, bdw_ref, lg_ref, lb_ref, wo_ref, bo_ref,
                       g_ref, b_ref, o_ref, sout_ref, buf_ref, conv_ref):
    x = x_ref[...]
    tl, d = x.shape
    xb = x.astype(BF16)
    h = (_dot(xb, win_ref[:, 0:d]) + bin_ref[:, 0:d]) * _sigmoid(_dot(xb, win_ref[:, d:2 * d]) + bin_ref[:, d:2 * d])
    y, new_state = _conv_window_step(buf_ref, h, lambda: st_ref[0], wdw_ref, bdw_ref, lg_ref, lb_ref, conv_ref,
                                     pl.program_id(1) == 0, tl)
    out = _dot(y.astype(BF16), wo_ref[...]) + bo_ref[...]
    o_ref[...] = _layer_norm(ALPHA * x + out, g_ref[...], b_ref[...])
    sout_ref[0] = new_state


def _conf_block(x, w_in, b_in, state, wdw, bdw, lg, lb, w_out, b_out, g, b, *, n_batch, seq):
    t, d = x.shape
    tl = CONV_TILE
    n_l = seq // tl
    rows = pl.BlockSpec((tl, d), lambda bi, l: (bi * n_l + l, 0))
    full = lambda shape: pl.BlockSpec(shape, lambda bi, l: (0,) * len(shape))
    state_spec = pl.BlockSpec((1, D_BUF, d), lambda bi, l: (bi, 0, 0))
    return pl.pallas_call(
        _conf_block_kernel,
        grid=(n_batch, n_l),
        in_specs=[rows, full(w_in.shape), full(b_in.shape), state_spec, full(wdw.shape), full((1, d)), full((1, d)),
                  full((1, d)), full(w_out.shape), full((1, d)), full((1, d)), full((1, d))],
        out_specs=[rows, state_spec],
        out_shape=[jax.ShapeDtypeStruct((t, d), F32), jax.ShapeDtypeStruct((n_batch, D_BUF, d), F32)],
        scratch_shapes=[pltpu.VMEM((CONV_PAD + tl + SUBLANES, d), F32), pltpu.VMEM((tl, d), F32)],
        compiler_params=_params("parallel", "arbitrary"),
        name="conf_block_prompt",
    )(x, w_in, b_in, state, wdw, bdw, lg, lb, w_out, b_out, g, b)


def _attn_heads(q, k_of, v_of, hd):
    outs = []
    for h in range(MEM_HEADS):
        hs = slice(h * hd, (h + 1) * hd)
        s = _dot_nt(q[:, hs].astype(BF16), k_of(hs).astype(BF16)) * (hd ** -0.5)
        p = jnp.exp(s - jnp.max(s, axis=-1, keepdims=True))
        denom = jnp.sum(p, axis=-1, keepdims=True)
        outs.append(_dot(p.astype(BF16), v_of(hs).astype(BF16)) / denom)
    return outs


def _attn_block_kernel(x_ref, wq_ref, k_ref, v_ref, wo_ref, g_ref, b_ref, wr_hi_ref, wr_lo_ref, o_ref, lg_ref):
    x = x_ref[...]
    hd = x.shape[1] // MEM_HEADS
    q = _dot(x.astype(BF16), wq_ref[...])
    att = jnp.concatenate(_attn_heads(q, lambda hs: k_ref[:, hs], lambda hs: v_ref[:, hs], hd), axis=1)
    y = _layer_norm(ALPHA * x + _dot(att.astype(BF16), wo_ref[...]), g_ref[...], b_ref[...])
    lg_ref[...] = _dot_hi(y, wr_hi_ref[...], wr_lo_ref[...])
    o_ref[...] = y


ATTN_BLOCK_ROWS = 1024


def _attn_block_prompt(x, w_q, mem_k, mem_v, w_o, g, b, router, layer, n_batch, seq):
    t, d = x.shape
    m = mem_k.shape[1] // n_batch
    tl = min(seq, ATTN_BLOCK_ROWS)
    n_l = seq // tl
    rows = lambda n: pl.BlockSpec((tl, n), lambda bi, l: (bi * n_l + l, 0))
    kv = pl.BlockSpec((None, m, d), lambda bi, l: (layer, bi, 0))
    full = lambda shape: pl.BlockSpec(shape, lambda bi, l: (0,) * len(shape))
    return pl.pallas_call(
        _attn_block_kernel,
        grid=(n_batch, n_l),
        in_specs=[rows(d), full((d, d)), kv, kv, full((d, d)), full((1, d)), full((1, d)),
                  full((d, LANES)), full((d, LANES))],
        out_specs=[rows(d), rows(LANES)],
        out_shape=[jax.ShapeDtypeStruct((t, d), F32), jax.ShapeDtypeStruct((t, LANES), F32)],
        compiler_params=_params("parallel", "parallel"),
        name="attn_block_prompt",
    )(x, w_q, mem_k, mem_v, w_o, g, b, *router)


ATTN_SAMPLE_NB = 4


def _attn_sample_kernel(q_ref, k_ref, v_ref, o_ref, *, nb, seq):
    hd = q_ref.shape[-1] // MEM_HEADS
    n_lt = hd // LANES
    m = k_ref.shape[1] // (MEM_HEADS * n_lt)

    def head(ref, n, h):
        tiles = [ref[n, pl.ds(lt * MEM_HEADS + h, m, stride=MEM_HEADS * n_lt), :] for lt in range(n_lt)]
        return jnp.concatenate(tiles, axis=1).astype(BF16)

    pairs = [(n, h) for n in range(nb) for h in range(MEM_HEADS)]
    s = jnp.concatenate(
        [_dot_nt(q_ref[n * seq:(n + 1) * seq, h * hd:(h + 1) * hd].astype(BF16), head(k_ref, n, h))
         for n, h in pairs], axis=0) * (hd ** -0.5)
    p = jnp.exp(s - jnp.max(s, axis=-1, keepdims=True))
    inv = 1.0 / jnp.sum(p, axis=-1, keepdims=True)
    for idx, (n, h) in enumerate(pairs):
        rs = slice(idx * seq, (idx + 1) * seq)
        o = _dot(p[rs].astype(BF16), head(v_ref, n, h)) * inv[rs]
        o_ref[n * seq:(n + 1) * seq, h * hd:(h + 1) * hd] = o


def _cache_rows(cache):
    nl, nbat, m, heads, hd = cache.shape
    c = cache.reshape(nl, nbat, m, heads, hd // LANES, LANES)
    return c.transpose(0, 1, 2, 4, 3, 5).reshape(nl, nbat, m * (hd // LANES) * heads, LANES)


def _cache_unrows(flat, heads):
    nl, nbat, rows, _ = flat.shape
    n_lt = D_MODEL // (heads * LANES)
    m = rows // (heads * n_lt)
    c = flat.reshape(nl, nbat, m, n_lt, heads, LANES).transpose(0, 1, 2, 4, 3, 5)
    return c.reshape(nl, nbat, m, heads, n_lt * LANES)


def _attn_sample(q, cache_k, cache_v, layer, n_batch, seq):
    t, d = q.shape
    nb = ATTN_SAMPLE_NB
    rows = pl.BlockSpec((nb * seq, d), lambda i: (i, 0))
    kv = pl.BlockSpec((None, nb) + cache_k.shape[2:], lambda i: (layer, i, 0, 0))
    return pl.pallas_call(
        functools.partial(_attn_sample_kernel, nb=nb, seq=seq),
        grid=(n_batch // nb,),
        in_specs=[rows, kv, kv],
        out_specs=rows,
        out_shape=jax.ShapeDtypeStruct((t, d), F32),
        compiler_params=_params("parallel"),
        name="attn_sample",
    )(q, cache_k, cache_v)


def _route_kernel(lg_ref, bias_ref, ltri_ref, route_ref, counts_ref, carry_ref):
    i = pl.program_id(0)

    @pl.when(i == 0)
    def _():
        carry_ref[...] = jnp.zeros(carry_ref.shape, F32)

    z = lg_ref[...] + bias_ref[...]
    lane = lax.broadcasted_iota(jnp.int32, z.shape, 1).astype(F32)
    neg = -jnp.inf
    far = float(LANES)

    def first_max(mask):
        vmax = jnp.max(jnp.where(mask, z, neg), axis=-1, keepdims=True)
        idx = jnp.min(jnp.where(mask & (z == vmax), lane, far), axis=-1, keepdims=True)
        return vmax, idx

    gmask = lane < float(MOE_GROUPS)
    gmax, gidx = first_max(gmask)
    gsum = jnp.sum(jnp.where(gmask, jnp.exp(z - gmax), 0.0), axis=-1, keepdims=True)
    g_w = 1.0 / gsum
    lo = float(ROUTE_LANE0) + float(MOE_PER_GROUP) * gidx
    emask = (lane >= lo) & (lane < lo + float(MOE_PER_GROUP))
    v1, i1 = first_max(emask)
    v2, i2 = first_max(emask & (lane != i1))
    tt = jnp.exp(v2 - v1)
    w0 = g_w / (1.0 + tt)
    w1 = g_w * tt / (1.0 + tt)
    sel1 = lane == i1
    sel2 = lane == i2
    onehot = jnp.where(sel1 | sel2, 1.0, 0.0)
    before = _dot(ltri_ref[...], onehot.astype(BF16)) + carry_ref[...]
    rank0 = jnp.sum(jnp.where(sel1, before, 0.0), axis=-1, keepdims=True)
    rank1 = jnp.sum(jnp.where(sel2, before, 0.0), axis=-1, keepdims=True)
    carry = carry_ref[...] + jnp.sum(onehot, axis=0, keepdims=True)
    carry_ref[...] = carry
    counts_ref[...] = carry
    e_off = float(ROUTE_LANE0)
    out = jnp.zeros(z.shape, F32)
    for ln, val in enumerate((i1 - e_off, i2 - e_off, w0, w1, rank0, rank1)):
        out = jnp.where(lane == float(ln), val, out)
    route_ref[...] = out


def _route(logits, bias, ltri):
    t = logits.shape[0]
    tm = ROW_TILE
    return pl.pallas_call(
        _route_kernel,
        grid=(t // tm,),
        in_specs=[_rows(tm, LANES), _full((1, LANES)), _full((tm, tm))],
        out_specs=[_rows(tm, LANES), _full((1, LANES))],
        out_shape=[jax.ShapeDtypeStruct((t, LANES), F32), jax.ShapeDtypeStruct((1, LANES), F32)],
        scratch_shapes=[pltpu.VMEM((1, LANES), F32)],
        compiler_params=_params("arbitrary"),
        name="moe_route",
    )(logits, bias, ltri)


def _row_copy_wait(src_rows, dst_rows, sem):
    pltpu.make_async_copy(src_rows, dst_rows, sem).wait()


def _dispatch_kernel(tail_ref, dest_ref, x_ref, xs_ref, zeros_ref, sem, *, tm):
    @pl.when(pl.program_id(0) == 0)
    def _():
        zeros_ref[...] = jnp.zeros(zeros_ref.shape, F32)
        bm = zeros_ref.shape[0]
        tails = [xs_ref.at[pl.ds(pl.multiple_of(tail_ref[e], bm), bm)] for e in range(MOE_EXPERTS)]
        for dst in tails:
            pltpu.make_async_copy(zeros_ref, dst, sem).start()
        for dst in tails:
            pltpu.make_async_copy(zeros_ref, dst, sem).wait()

    def issue(t, carry):
        for j in range(2):
            d = dest_ref[2 * t + j]
            pltpu.make_async_copy(x_ref.at[pl.ds(t, 1)], xs_ref.at[pl.ds(d, 1)], sem).start(priority=j)
        return carry

    lax.fori_loop(0, tm, issue, 0, unroll=8)
    for _ in range(2):
        _row_copy_wait(x_ref, xs_ref.at[pl.ds(0, tm)], sem)


def _dispatch(tail_rows, dest_flat, x, n_rows):
    t, d = x.shape
    tm = ROW_TILE
    grid_spec = pltpu.PrefetchScalarGridSpec(
        num_scalar_prefetch=1,
        grid=(t // tm,),
        in_specs=[pl.BlockSpec((2 * tm,), lambda i, tail: (i,), memory_space=pltpu.SMEM),
                  pl.BlockSpec((tm, d), lambda i, tail: (i, 0))],
        out_specs=pl.BlockSpec(memory_space=pl.ANY),
        scratch_shapes=[pltpu.VMEM((MOE_BLOCK_ROWS, d), F32), pltpu.SemaphoreType.DMA(())],
    )
    return pl.pallas_call(
        functools.partial(_dispatch_kernel, tm=tm),
        grid_spec=grid_spec,
        out_shape=jax.ShapeDtypeStruct((n_rows, d), F32),
        compiler_params=_params("arbitrary"),
        name="moe_dispatch",
    )(tail_rows, dest_flat, x)


def _expert_kernel(blk_e_ref, nused_ref, xs_ref, wgu_ref, wd_ref, y_ref, wgu_bf, wd_bf):
    i = pl.program_id(0)
    prev = blk_e_ref[jnp.maximum(i - 1, 0)]
    new_expert = (i == 0) | (blk_e_ref[i] != prev)

    @pl.when(new_expert)
    def _():
        wgu_bf[...] = wgu_ref[...].astype(BF16)
        wd_bf[...] = wd_ref[...].astype(BF16)

    @pl.when(i < nused_ref[0])
    def _():
        hid = wd_bf.shape[0]
        xb = xs_ref[...].astype(BF16)
        gate = _dot(xb, wgu_bf[:, 0:hid])
        up = _dot(xb, wgu_bf[:, hid:2 * hid])
        y_ref[...] = _dot((_silu(gate) * up).astype(BF16), wd_bf[...])

    @pl.when(i >= nused_ref[0])
    def _():
        y_ref[...] = jnp.zeros(y_ref.shape, F32)


def _experts(blk_e, nused, xs, w_gate_up, w_down, layer):
    nr, d = xs.shape
    bm = MOE_BLOCK_ROWS
    hid2 = w_gate_up.shape[-1]
    hid = w_down.shape[-2]
    grid_spec = pltpu.PrefetchScalarGridSpec(
        num_scalar_prefetch=2,
        grid=(nr // bm,),
        in_specs=[pl.BlockSpec((bm, d), lambda i, be, nu: (jnp.minimum(i, nu[0] - 1), 0)),
                  pl.BlockSpec((None, None, d, hid2), lambda i, be, nu: (layer, be[i], 0, 0)),
                  pl.BlockSpec((None, None, hid, d), lambda i, be, nu: (layer, be[i], 0, 0))],
        out_specs=pl.BlockSpec((bm, d), lambda i, be, nu: (i, 0)),
        scratch_shapes=[pltpu.VMEM((d, hid2), BF16), pltpu.VMEM((hid, d), BF16)],
    )
    return pl.pallas_call(
        _expert_kernel,
        grid_spec=grid_spec,
        out_shape=jax.ShapeDtypeStruct((nr, d), F32),
        compiler_params=_params("arbitrary"),
        name="moe_experts",
    )(blk_e, nused, xs, w_gate_up, w_down)


def _combine_kernel(dest_ref, dest_next_ref, x_ref, route_ref, g_ref, b_ref, y_hbm, *rest, tm, n_first):
    if n_first is None:
        o_ref, ybuf, sems = rest
    else:
        o_ref, o2_ref, ybuf, sems = rest
    i = pl.program_id(0)
    n_tiles = pl.num_programs(0)

    def gather(d_ref, slot):
        def issue(t, carry):
            for j in range(2):
                d = d_ref[2 * t + j]
                pltpu.make_async_copy(y_hbm.at[pl.ds(d, 1)], ybuf.at[slot, j, pl.ds(t, 1)],
                                      sems.at[slot]).start(priority=j)
            return carry

        lax.fori_loop(0, tm, issue, 0, unroll=8)

    @pl.when(i == 0)
    def _():
        gather(dest_ref, 0)

    @pl.when(i + 1 < n_tiles)
    def _():
        gather(dest_next_ref, lax.rem(i + 1, 2))

    slot = lax.rem(i, 2)
    for j in range(2):
        _row_copy_wait(y_hbm.at[pl.ds(0, tm)], ybuf.at[slot, j], sems.at[slot])
    moe = route_ref[:, 2:3] * ybuf[slot, 0] + route_ref[:, 3:4] * ybuf[slot, 1]
    y = _layer_norm(ALPHA * x_ref[...] + moe, g_ref[...], b_ref[...])
    if n_first is None:
        o_ref[...] = y
    else:
        @pl.when(i < n_first)
        def _():
            o_ref[...] = y

        @pl.when(i >= n_first)
        def _():
            o2_ref[...] = y


def _combine(dest_flat, x, route, g, b, yb, split_rows=None):
    t, d = x.shape
    tm = ROW_TILE
    n_tiles = t // tm
    out_specs = _rows(tm, d)
    out_shape = jax.ShapeDtypeStruct((t, d), F32)
    n_first = None
    if split_rows is not None:
        n_first = split_rows // tm
        out_specs = [pl.BlockSpec((tm, d), lambda i: (jnp.minimum(i, n_first - 1), 0)),
                     pl.BlockSpec((tm, d), lambda i: (jnp.maximum(i - n_first, 0), 0))]
        out_shape = [jax.ShapeDtypeStruct((split_rows, d), F32), jax.ShapeDtypeStruct((t - split_rows, d), F32)]
    return pl.pallas_call(
        functools.partial(_combine_kernel, tm=tm, n_first=n_first),
        grid=(n_tiles,),
        in_specs=[pl.BlockSpec((2 * tm,), lambda i: (i,), memory_space=pltpu.SMEM),
                  pl.BlockSpec((2 * tm,), lambda i: (jnp.minimum(i + 1, n_tiles - 1),), memory_space=pltpu.SMEM),
                  _rows(tm, d), _rows(tm, LANES), _full((1, d)), _full((1, d)),
                  pl.BlockSpec(memory_space=pl.ANY)],
        out_specs=out_specs,
        out_shape=out_shape,
        scratch_shapes=[pltpu.VMEM((2, 2, tm, d), F32), pltpu.SemaphoreType.DMA((2,))],
        compiler_params=_params("arbitrary"),
        name="moe_combine",
    )(dest_flat, dest_flat, x, route, g, b, yb)


def _moe_plan(route, counts):
    bm = MOE_BLOCK_ROWS
    t = route.shape[0]
    n_blocks = -(-2 * t // bm) + MOE_EXPERTS
    e_idx = route[:, 0:2].astype(jnp.int32)
    rank = route[:, 4:6].astype(jnp.int32)
    cnt = counts[0, ROUTE_LANE0:ROUTE_LANE0 + MOE_EXPERTS].astype(jnp.int32)
    padded = ((cnt + bm - 1) // bm) * bm
    pad_end = jnp.cumsum(padded)
    pad_start = pad_end - padded
    experts = jnp.arange(MOE_EXPERTS, dtype=jnp.int32)
    dest = (rank + jnp.sum(jnp.where(e_idx[..., None] == experts, pad_start, 0), axis=-1)).reshape(-1)
    nused = pad_end[-1] // bm
    blk = jnp.arange(n_blocks, dtype=jnp.int32)
    blk_e = jnp.sum((pad_end[None, :] <= (blk * bm)[:, None]).astype(jnp.int32), axis=1)
    blk_e = jnp.minimum(blk_e, MOE_EXPERTS - 1)
    last_e = jnp.max(jnp.where(cnt > 0, experts, 0))
    blk_e = jnp.where(blk < nused, blk_e, last_e)
    empty = cnt == 0
    tail_rows = jnp.where(empty, (nused + jnp.cumsum(empty.astype(jnp.int32)) - 1) * bm, pad_end - bm)
    return (dest.astype(jnp.int32), blk_e.astype(jnp.int32), nused.reshape(1).astype(jnp.int32),
            tail_rows.astype(jnp.int32), n_blocks)


def _hi_lo(w):
    hi = w.astype(BF16)
    return hi, (w - hi.astype(F32)).astype(BF16)


def kernel(x_prompt, x_sample, mem_prompt, cache_mem_k, cache_mem_v, state_gla, state_hgrn, state_conv,
           ln_g, ln_b, a_w_in, a_b_in, a_ln_g, a_ln_b, a_w_s, a_b_s, a_w_out, a_b_out,
           b_w_in, b_w_g2, b_b_g, b_norm_g, b_w_out, c_lb, c_w_in, c_norm_g, c_w_out,
           d_w_in, d_b_in, d_w_dw, d_b_dw, d_ln_g, d_ln_b, d_w_out, d_b_out,
           m_w_q, m_w_k, m_w_v, m_w_o, r_w_grp, r_b_grp, r_w_exp, r_b_exp, e_w_gate_up, e_w_down):
    bp, lp, d = x_prompt.shape
    bs, ls, _ = x_sample.shape
    tp, ts = bp * lp, bs * ls
    t = tp + ts
    mem_len = mem_prompt.shape[1]

    row = lambda a: a.reshape(1, -1)
    x = None

    mem2d = mem_prompt.reshape(bp * mem_len, d)
    mem_k, mem_k_flat = _mem_proj(mem2d, m_w_k.astype(BF16), mem_len)
    mem_v, mem_v_flat = _mem_proj(mem2d, m_w_v.astype(BF16), mem_len)

    cache_k = _cache_rows(cache_mem_k)
    cache_v = _cache_rows(cache_mem_v)

    lb_all = jnp.cumsum(jax.nn.softmax(c_lb.astype(F32), axis=0), axis=0)
    lb_all = lb_all - lb_all[:1]
    ltri = jnp.tril(jnp.ones((ROW_TILE, ROW_TILE), F32), -1).astype(BF16)
    zero_bias = jnp.zeros((1, d), F32)

    outs = {"v": [], "gla_p": [], "gla_s": [], "hgrn_p": [], "hgrn_s": [], "conv_p": [], "conv_s": []}
    for i in range(DEPTH):
        j = i // N_MIXERS
        kind = i % N_MIXERS
        g1, b1 = row(ln_g[i, 0]), row(ln_b[i, 0])
        if kind == 0:
            tril = jnp.tril(jnp.ones((A_CHUNK, A_CHUNK), bool))
            wc_p = jnp.where(tril, a_w_s[j], 0.0).astype(BF16)
            bc_p = a_b_s[j][:, :, None]
            reps = A_CHUNK // ls
            small = jnp.where(jnp.tril(jnp.ones((ls, ls), bool)), a_w_s[j][:, :ls, :ls], 0.0)
            wc_s = jax.vmap(lambda m: jnp.kron(jnp.eye(reps, dtype=F32), m))(small).astype(BF16)
            bc_s = jnp.tile(a_b_s[j][:, :ls], (1, reps))[:, :, None]
            common = (a_w_in[j].astype(BF16), row(a_b_in[j]), row(a_ln_g[j]), row(a_ln_b[j]))
            tail = (a_w_out[j].astype(BF16), row(a_b_out[j]), g1, b1)
            xp_in = x_prompt.reshape(tp, d) if x is None else x[:tp]
            xs_in = x_sample.reshape(ts, d) if x is None else x[tp:]
            x1 = _gmlp(xp_in, 0, t, *common, wc_p, bc_p, *tail)
            x1, v_s = _gmlp(xs_in, tp, t, *common, wc_s, bc_s, *tail, alias=x1)
            outs["v"].append(v_s.reshape(bs, ls, -1))
        elif kind == 1:
            dk, dv = b_w_g2.shape[-1], b_w_out.shape[1]
            w_in = b_w_in[j]
            w_main = w_in[:, :2 * dk + 2 * dv].astype(BF16)
            w_low = jnp.pad(w_in[:, 2 * dk + 2 * dv:], ((0, 0), (0, LANES - B_GATE_RANK))).astype(BF16)
            g2_hi, g2_lo = _hi_lo(jnp.pad(b_w_g2[j], ((0, LANES - B_GATE_RANK), (0, 0))))
            proj_w = (w_main, w_low, g2_hi, g2_lo, row(b_b_g[j]))
            proj = functools.partial(_gla_proj_kernel, dk=dk, dv=dv, q_scale=(dk // B_HEADS) ** -0.5)
            ng = row(b_norm_g[j])
            w_out = b_w_out[j].astype(BF16)
            s0_p = jnp.zeros((bp,) + state_gla.shape[2:], F32)
            x1, s_p = _linrec_block(x, proj, proj_w, ng, s0_p, w_out, g1, b1, n_batch=bp, seq=lp, heads=B_HEADS,
                                    hk=dk, hv=dv, name="gla_block_prompt")
            q, la, k, v, r = _gla_proj(x, proj_w, dk, dv, tp, ts)
            o_s, s_s = _linrec(q, k, la, v, r, ng, state_gla[j], n_batch=bs, seq=ls, heads=B_HEADS)
            x1 = _mm_ln(o_s, w_out, zero_bias, x, g1, b1, "gla_out_sample", row_off=tp, fill=(x1,))
            outs["gla_p"].append(s_p)
            outs["gla_s"].append(s_s)
        elif kind == 2:
            heads = state_hgrn.shape[2]
            proj_w = (c_w_in[j].astype(BF16), row(lb_all[i]))
            proj = functools.partial(_hgrn_proj_kernel, d=d, q_scale=C_EXPAND ** -0.5)
            ng = row(c_norm_g[j])
            w_out = c_w_out[j].astype(BF16)
            s0_p = jnp.zeros((bp,) + state_hgrn.shape[2:], F32)
            x1, s_p = _linrec_block(x, proj, proj_w, ng, s0_p, w_out, g1, b1, n_batch=bp, seq=lp, heads=heads,
                                    hk=d, hv=d, name="hgrn_block_prompt")
            q, lf, k, v, gt = _hgrn_proj(x, proj_w, tp, ts)
            o_s, s_s = _linrec(q, k, lf, v, gt, ng, state_hgrn[j], n_batch=bs, seq=ls, heads=heads)
            x1 = _mm_ln(o_s, w_out, zero_bias, x, g1, b1, "hgrn_out_sample", row_off=tp, fill=(x1,))
            outs["hgrn_p"].append(s_p)
            outs["hgrn_s"].append(s_s)
        else:
            w_in, b_in = d_w_in[j].astype(BF16), row(d_b_in[j])
            w_out, b_out = d_w_out[j].astype(BF16), row(d_b_out[j])
            cargs = (d_w_dw[j], row(d_b_dw[j]), row(d_ln_g[j]), row(d_ln_b[j]))
            conv0 = jnp.zeros((bp, D_BUF, d), F32)
            x1, s_p = _conf_block(x, w_in, b_in, conv0, *cargs, w_out, b_out, g1, b1, n_batch=bp, seq=lp)
            h_s = _glu(x, w_in, b_in, tp, ts)
            c_s, s_s = _conv(h_s, state_conv[j], *cargs, n_batch=bs, seq=ls)
            x1 = _mm_ln(c_s, w_out, b_out, x, g1, b1, "conf_out_sample", row_off=tp, fill=(x1,))
            outs["conv_p"].append(s_p)
            outs["conv_s"].append(s_s)

        w_route = jnp.concatenate([r_w_grp[i], r_w_exp[i]], axis=1)
        w_route = jnp.pad(w_route, ((0, 0), (0, LANES - w_route.shape[1])))
        b_route = jnp.pad(jnp.concatenate([r_b_grp[i], r_b_exp[i]]), (0, LANES - MOE_GROUPS - MOE_EXPERTS))
        router = _hi_lo(w_route)
        w_q, w_o = m_w_q[i].astype(BF16), m_w_o[i].astype(BF16)
        g2, b2 = row(ln_g[i, 1]), row(ln_b[i, 1])
        x2, logits = _attn_block_prompt(x1, w_q, mem_k, mem_v, w_o, g2, b2, router, i, bp, lp)
        q_s = _mm_rows(x1, w_q, "attn_q_sample", tp, ts)
        att_s = _attn_sample(q_s, cache_k, cache_v, i, bs, ls)
        x2, logits = _mm_ln(att_s, w_o, zero_bias, x1, g2, b2, "attn_out_sample", router=router, row_off=tp,
                            fill=(x2, logits))

        route, counts = _route(logits, row(b_route), ltri)
        dest, blk_e, nused, tail_rows, n_blocks = _moe_plan(route, counts)
        xs = _dispatch(tail_rows, dest, x2, n_blocks * MOE_BLOCK_ROWS)
        yb = _experts(blk_e, nused, xs, e_w_gate_up, e_w_down, i)
        if i + 1 < DEPTH:
            x = _combine(dest, x2, route, row(ln_g[i, 2]), row(ln_b[i, 2]), yb)
        else:
            y_p, y_s = _combine(dest, x2, route, row(ln_g[i, 2]), row(ln_b[i, 2]), yb, split_rows=tp)

    y_prompt = y_p.reshape(bp, lp, d)
    y_sample = y_s.reshape(bs, ls, d)
    mem_k_p = _cache_unrows(mem_k_flat, MEM_HEADS)
    mem_v_p = _cache_unrows(mem_v_flat, MEM_HEADS)
    return (y_prompt, y_sample, mem_k_p, mem_v_p, jnp.stack(outs["gla_p"]), jnp.stack(outs["hgrn_p"]),
            jnp.stack(outs["conv_p"]), jnp.stack(outs["v"]), jnp.stack(outs["gla_s"]),
            jnp.stack(outs["hgrn_s"]), jnp.stack(outs["conv_s"]))
```

```python
import functools
import math

import jax
import jax.numpy as jnp
from jax import lax
from jax.experimental import pallas as pl
from jax.experimental.pallas import tpu as pltpu

F32 = jnp.float32
BF16 = jnp.bfloat16

D_MODEL = 1024
DEPTH = 4
N_MIXERS = 4
ALPHA = (2.0 * DEPTH) ** 0.25
LN_EPS = 1e-5
A_CHUNK = 128
A_GROUPS = 4
B_HEADS = 4
B_GATE_RANK = 16
B_GATE_TAU = 16.0
C_EXPAND = 128
D_CONV_W = 31
D_BUF = D_CONV_W - 1
LIN_CHUNK = 32
MEM_HEADS = 4
MOE_GROUPS = 4
MOE_PER_GROUP = 8
MOE_EXPERTS = MOE_GROUPS * MOE_PER_GROUP
MOE_HIDDEN = 512

LANES = 128
SUBLANES = 8
ROW_TILE = 512
MOE_BLOCK_ROWS = 512
ROUTE_LANE0 = MOE_GROUPS
VMEM_LIMIT = 56 * 1024 * 1024
INV_SQRT2 = 1.0 / math.sqrt(2.0)


def _params(*sem, vmem=VMEM_LIMIT):
    return pltpu.CompilerParams(dimension_semantics=sem, vmem_limit_bytes=vmem)


def _dot(a, b):
    return jnp.dot(a, b, preferred_element_type=F32)


def _dot_nt(a, b):
    return lax.dot_general(a, b, (((1,), (1,)), ((), ())), preferred_element_type=F32)


def _dot_hi(a, w_hi, w_lo):
    a_hi = a.astype(BF16)
    a_lo = (a - a_hi.astype(F32)).astype(BF16)
    return _dot(a_hi, w_hi) + _dot(a_lo, w_hi) + _dot(a_hi, w_lo)


def _layer_norm(x, g, b):
    mu = jnp.mean(x, axis=-1, keepdims=True)
    xc = x - mu
    var = jnp.mean(xc * xc, axis=-1, keepdims=True)
    return xc * lax.rsqrt(var + LN_EPS) * g + b


def _sigmoid(x):
    return 1.0 / (1.0 + jnp.exp(-x))


def _silu(x):
    return x * _sigmoid(x)


def _gelu(x):
    return 0.5 * x * (1.0 + lax.erf(x * INV_SQRT2))


def _log_sigmoid(x):
    return jnp.minimum(x, 0.0) - jnp.log(1.0 + jnp.exp(-jnp.abs(x)))


def _full(shape):
    return pl.BlockSpec(shape, lambda *_: (0,) * len(shape))


def _rows(tm, n, off=0):
    return pl.BlockSpec((tm, n), lambda i: (i + off, 0))


def _mm_kernel(a_ref, w_ref, o_ref):
    o_ref[...] = _dot(a_ref[...].astype(BF16), w_ref[...])


def _mem_proj_kernel(a_ref, w_ref, nat_ref, flat_ref, *, mem_len):
    res = _dot(a_ref[...].astype(BF16), w_ref[...])
    nat_ref[...] = res
    n_lt = res.shape[1] // (MEM_HEADS * LANES)
    for b in range(res.shape[0] // mem_len):
        for h in range(MEM_HEADS):
            for lt in range(n_lt):
                c0 = (h * n_lt + lt) * LANES
                flat_ref[b, pl.ds(lt * MEM_HEADS + h, mem_len, stride=MEM_HEADS * n_lt), :] = (
                    res[b * mem_len:(b + 1) * mem_len, c0:c0 + LANES])


def _mem_proj(a, w, mem_len):
    m, k = a.shape
    nl, _, n = w.shape
    tm = min(m, ROW_TILE)
    nbat = tm // mem_len
    return pl.pallas_call(
        functools.partial(_mem_proj_kernel, mem_len=mem_len),
        grid=(nl, m // tm),
        in_specs=[pl.BlockSpec((tm, k), lambda l, i: (i, 0)),
                  pl.BlockSpec((None, k, n), lambda l, i: (l, 0, 0))],
        out_specs=[pl.BlockSpec((None, tm, n), lambda l, i: (l, i, 0)),
                   pl.BlockSpec((None, nbat, mem_len * n // LANES, LANES), lambda l, i: (l, i, 0, 0))],
        out_shape=[jax.ShapeDtypeStruct((nl, m, n), F32),
                   jax.ShapeDtypeStruct((nl, m // mem_len, mem_len * n // LANES, LANES), F32)],
        compiler_params=_params("parallel", "parallel"),
        name="mem_kv_proj",
    )(a, w)


def _mm_rows(a, w, name, row_off, n_rows):
    k = a.shape[1]
    n = w.shape[1]
    tm = ROW_TILE
    return pl.pallas_call(
        _mm_kernel,
        grid=(n_rows // tm,),
        in_specs=[_rows(tm, k, row_off // tm), _full((k, n))],
        out_specs=_rows(tm, n),
        out_shape=jax.ShapeDtypeStruct((n_rows, n), F32),
        compiler_params=_params("parallel"),
        name=name,
    )(a, w)


def _mm_ln_kernel(a_ref, w_ref, bias_ref, x_ref, g_ref, b_ref, *rest, with_logits, n_alias):
    h = _dot(a_ref[...].astype(BF16), w_ref[...]) + bias_ref[...]
    y = _layer_norm(ALPHA * x_ref[...] + h, g_ref[...], b_ref[...])
    if with_logits:
        wr_hi_ref, wr_lo_ref = rest[:2]
        o_ref, lg_ref = rest[2 + n_alias:]
        lg_ref[...] = _dot_hi(y, wr_hi_ref[...], wr_lo_ref[...])
    else:
        (o_ref,) = rest[n_alias:]
    o_ref[...] = y


def _mm_ln(a, w, bias, x, g, b, name, router=None, row_off=0, fill=None):
    m, k = a.shape
    t, d = x.shape
    tm = ROW_TILE
    off = row_off // tm
    in_specs = [_rows(tm, k), _full((k, d)), _full((1, d)), _rows(tm, d, off), _full((1, d)), _full((1, d))]
    args = [a, w, bias, x, g, b]
    out_specs = [_rows(tm, d, off)]
    out_shape = [jax.ShapeDtypeStruct((t, d), F32)]
    if router is not None:
        in_specs += [_full((d, LANES)), _full((d, LANES))]
        args += list(router)
        out_specs.append(_rows(tm, LANES, off))
        out_shape.append(jax.ShapeDtypeStruct((t, LANES), F32))
    aliases = {}
    for n, arr in enumerate(fill or ()):
        in_specs.append(pl.BlockSpec(memory_space=pl.ANY))
        args.append(arr)
        aliases[len(args) - 1] = n
    res = pl.pallas_call(
        functools.partial(_mm_ln_kernel, with_logits=router is not None, n_alias=len(aliases)),
        grid=(m // tm,),
        in_specs=in_specs,
        out_specs=out_specs,
        out_shape=out_shape,
        input_output_aliases=aliases,
        compiler_params=_params("parallel"),
        name=name,
    )(*args)
    return res if router is not None else res[0]


def _gmlp_kernel(x_ref, w_in_ref, b_in_ref, lng_ref, lnb_ref, wc_ref, bc_ref, w_out_ref, b_out_ref,
                 g_ref, b_ref, *rest, emit_v, n_chunks):
    if emit_v:
        _alias_ref, o_ref, v_ref, vn_ref = rest
    else:
        o_ref, vn_ref = rest
    half = w_out_ref.shape[0]
    gw = half // A_GROUPS
    x = x_ref[...]
    xb = x.astype(BF16)
    v = _gelu(_dot(xb, w_in_ref[:, half:]) + b_in_ref[:, half:])
    vn = _layer_norm(v, lng_ref[...], lnb_ref[...])
    vn_ref[...] = vn
    if emit_v:
        v_ref[...] = vn
    acc = jnp.zeros(x.shape, F32)
    for grp in range(A_GROUPS):
        cols = slice(grp * gw, (grp + 1) * gw)
        u = _gelu(_dot(xb, w_in_ref[:, cols]) + b_in_ref[:, cols])
        mixed = []
        for c in range(n_chunks):
            vc = vn_ref[c * A_CHUNK:(c + 1) * A_CHUNK, cols].astype(BF16)
            mixed.append(_dot(wc_ref[grp], vc) + bc_ref[grp])
        mixed = mixed[0] if n_chunks == 1 else jnp.concatenate(mixed, axis=0)
        acc = acc + _dot((u * mixed).astype(BF16), w_out_ref[cols, :])
    h = acc + b_out_ref[...]
    o_ref[...] = _layer_norm(ALPHA * x + h, g_ref[...], b_ref[...])


GMLP_CHUNKS = 4


def _gmlp(x, row_off, t, w_in, b_in, lng, lnb, wc, bc, w_out, b_out, g, b, alias=None):
    n_rows, d = x.shape
    ffn = w_in.shape[1]
    half = ffn // 2
    tm = GMLP_CHUNKS * A_CHUNK
    emit_v = alias is not None
    off = row_off // tm
    once = lambda shape: pl.BlockSpec(shape, lambda *_: (0,) * len(shape), pipeline_mode=pl.Buffered(1))
    in_specs = [_rows(tm, d), once((d, ffn)), _full((1, ffn)), _full((1, half)), _full((1, half)),
                _full((A_GROUPS, A_CHUNK, A_CHUNK)), _full((A_GROUPS, A_CHUNK, 1)), once((half, d)),
                _full((1, d)), _full((1, d)), _full((1, d))]
    args = [x, w_in, b_in, lng, lnb, wc, bc, w_out, b_out, g, b]
    out_specs = _rows(tm, d, off)
    out_shape = jax.ShapeDtypeStruct((t, d), F32)
    aliases = {}
    if emit_v:
        in_specs.append(pl.BlockSpec(memory_space=pl.ANY))
        args.append(alias)
        aliases = {len(args) - 1: 0}
        out_specs = [out_specs, _rows(tm, half)]
        out_shape = [out_shape, jax.ShapeDtypeStruct((n_rows, half), F32)]
    return pl.pallas_call(
        functools.partial(_gmlp_kernel, emit_v=emit_v, n_chunks=tm // A_CHUNK),
        grid=(n_rows // tm,),
        in_specs=in_specs,
        out_specs=out_specs,
        out_shape=out_shape,
        scratch_shapes=[pltpu.VMEM((tm, half), F32)],
        input_output_aliases=aliases,
        compiler_params=_params("parallel"),
        name="gmlp_sample" if emit_v else "gmlp_prompt",
    )(*args)


def _gla_proj_kernel(x_ref, w_ref, wlow_ref, g2_hi_ref, g2_lo_ref, bg_ref,
                     q_ref, la_ref, k_ref, v_ref, r_ref, *, dk, dv, q_scale):
    xb = x_ref[...].astype(BF16)
    q_ref[...] = _dot(xb, w_ref[:, 0:dk]) * q_scale
    k_ref[...] = _dot(xb, w_ref[:, dk:2 * dk])
    v_ref[...] = _dot(xb, w_ref[:, 2 * dk:2 * dk + dv])
    r_ref[...] = _dot(xb, w_ref[:, 2 * dk + dv:2 * dk + 2 * dv])
    g_low = _dot(xb, wlow_ref[...])
    pre = _dot_hi(g_low, g2_hi_ref[...], g2_lo_ref[...]) + bg_ref[...]
    la_ref[...] = _log_sigmoid(pre) * (1.0 / B_GATE_TAU)


def _gla_proj(x, weights, dk, dv, row_off, n_rows):
    d = x.shape[1]
    tm = ROW_TILE
    shapes = [dk, dk, dk, dv, dv]
    return pl.pallas_call(
        functools.partial(_gla_proj_kernel, dk=dk, dv=dv, q_scale=(dk // B_HEADS) ** -0.5),
        grid=(n_rows // tm,),
        in_specs=[_rows(tm, d, row_off // tm)] + [_full(w.shape) for w in weights],
        out_specs=[_rows(tm, n) for n in shapes],
        out_shape=[jax.ShapeDtypeStruct((n_rows, n), F32) for n in shapes],
        compiler_params=_params("parallel"),
        name="gla_proj",
    )(x, *weights)


def _hgrn_proj_kernel(x_ref, w_ref, lb_ref, q_ref, lf_ref, k_ref, v_ref, gt_ref, *, d, q_scale):
    xb = x_ref[...].astype(BF16)
    lb = lb_ref[...]
    q_ref[...] = _silu(_dot(xb, w_ref[:, 0:d])) * q_scale
    f = _dot(xb, w_ref[:, d:2 * d])
    lf_ref[...] = jnp.log(lb + (1.0 - lb) * _sigmoid(f))
    k_ref[...] = (1.0 - lb) * _sigmoid(-f)
    v_ref[...] = _dot(xb, w_ref[:, 2 * d:3 * d])
    gt_ref[...] = _dot(xb, w_ref[:, 3 * d:4 * d])


def _hgrn_proj(x, weights, row_off, n_rows):
    d = x.shape[1]
    tm = ROW_TILE
    return pl.pallas_call(
        functools.partial(_hgrn_proj_kernel, d=d, q_scale=C_EXPAND ** -0.5),
        grid=(n_rows // tm,),
        in_specs=[_rows(tm, d, row_off // tm)] + [_full(w.shape) for w in weights],
        out_specs=[_rows(tm, d)] * 5,
        out_shape=[jax.ShapeDtypeStruct((n_rows, d), F32)] * 5,
        compiler_params=_params("parallel"),
        name="hgrn_proj",
    )(x, *weights)


def _cumsum_rows(x, chunk):
    pos = lax.broadcasted_iota(jnp.int32, x.shape, 0) & (chunk - 1)
    step = 1
    while step < chunk:
        x = x + jnp.where(pos >= step, pltpu.roll(x, step, axis=0), 0.0)
        step *= 2
    return x


def _chunk_rows(x, row, chunk, n_chunks):
    parts = [jnp.broadcast_to(x[c * chunk + row:c * chunk + row + 1, :], (chunk, x.shape[1]))
             for c in range(n_chunks)]
    return parts[0] if n_chunks == 1 else jnp.concatenate(parts, axis=0)


def _linrec_mask(tl, chunk):
    r_id = lax.broadcasted_iota(jnp.int32, (tl, tl), 0)
    c_id = lax.broadcasted_iota(jnp.int32, (tl, tl), 1)
    return (r_id >= c_id) & ((r_id // chunk) == (c_id // chunk))


def _linrec_heads(q_ref, k_ref, g_ref, v_ref, gate_ref, ng_ref, o_ref, st_ref, rs, *, heads, dk, dv, chunk, n_chunks):
    mask = _linrec_mask(chunk * n_chunks, chunk)
    for h in range(heads):
        ks = slice(h * dk, (h + 1) * dk)
        vs = slice(h * dv, (h + 1) * dv)
        bcum = _cumsum_rows(g_ref[rs, ks], chunk)
        b_mid = _chunk_rows(bcum, chunk // 2 - 1, chunk, n_chunks)
        b_end = _chunk_rows(bcum, chunk - 1, chunk, n_chunks)
        q = q_ref[rs, ks]
        k = k_ref[rs, ks]
        vb = v_ref[rs, vs].astype(BF16)
        q_in = (q * jnp.exp(bcum)).astype(BF16)
        q_a = (q * jnp.exp(bcum - b_mid)).astype(BF16)
        k_a = (k * jnp.exp(b_mid - bcum)).astype(BF16)
        k_end = k * jnp.exp(b_end - bcum)
        a = jnp.where(mask, _dot_nt(q_a, k_a), 0.0).astype(BF16)
        o_intra = _dot(a, vb)
        st = st_ref[h]
        outs = []
        for c in range(n_chunks):
            cr = slice(c * chunk, (c + 1) * chunk)
            outs.append(o_intra[cr] + _dot(q_in[cr], st.astype(BF16)))
            decay = jnp.exp(bcum[cr].T[:, chunk - 1:chunk])
            st = st * decay + _dot(k_end[cr].T.astype(BF16), vb[cr])
        st_ref[h] = st
        o = outs[0] if n_chunks == 1 else jnp.concatenate(outs, axis=0)
        o = o * lax.rsqrt(jnp.mean(o * o, axis=-1, keepdims=True) + LN_EPS) * ng_ref[:, vs]
        o_ref[rs, vs] = o * _silu(gate_ref[rs, vs])


def _linrec_kernel(q_ref, k_ref, g_ref, v_ref, gate_ref, ng_ref, s0_ref, o_ref, sout_ref, st_ref, *, nb, tl, **dims):
    li = pl.program_id(1)
    for n in range(nb):
        @pl.when(li == 0)
        def _():
            st_ref[...] = s0_ref[n]

        _linrec_heads(q_ref, k_ref, g_ref, v_ref, gate_ref, ng_ref, o_ref, st_ref, slice(n * tl, (n + 1) * tl),
                      **dims)
        sout_ref[n] = st_ref[...]


LINREC_CHUNK = 2 * LIN_CHUNK
LINREC_SAMPLE_NB = 4
LINREC_BLOCK_SUBTILES = 2


def _linrec_tiling(seq):
    chunk = LINREC_CHUNK if seq % LINREC_CHUNK == 0 else seq
    tl = min(seq, 4 * chunk)
    return chunk, tl, seq // tl


def _linrec(q, k, g, v, gate, ng, s0, *, n_batch, seq, heads):
    t, hk = q.shape
    hv = v.shape[1]
    dk, dv = hk // heads, hv // heads
    chunk, tl, n_l = _linrec_tiling(seq)
    nb = LINREC_SAMPLE_NB if n_l == 1 else 1

    def rows(n):
        return pl.BlockSpec((nb * tl, n), lambda b, l: (b * n_l + l, 0))

    state_spec = pl.BlockSpec((nb, heads, dk, dv), lambda b, l: (b, 0, 0, 0))
    return pl.pallas_call(
        functools.partial(_linrec_kernel, nb=nb, tl=tl, heads=heads, dk=dk, dv=dv, chunk=chunk,
                          n_chunks=tl // chunk),
        grid=(n_batch // nb, n_l),
        in_specs=[rows(hk), rows(hk), rows(hk), rows(hv), rows(hv), pl.BlockSpec((1, hv), lambda b, l: (0, 0)),
                  state_spec],
        out_specs=[rows(hv), state_spec],
        out_shape=[jax.ShapeDtypeStruct((t, hv), F32), jax.ShapeDtypeStruct((n_batch, heads, dk, dv), F32)],
        scratch_shapes=[pltpu.VMEM((heads, dk, dv), F32)],
        compiler_params=_params("parallel", "arbitrary"),
        name="linrec_h%d" % heads,
    )(q, k, g, v, gate, ng, s0)


def _linrec_block_kernel(*refs, proj, n_proj_w, **dims):
    x_ref = refs[0]
    proj_w = refs[1:1 + n_proj_w]
    ng_ref, s0_ref, wo_ref, g_ref, b_ref, o_ref, sout_ref = refs[1 + n_proj_w:8 + n_proj_w]
    q_s, g_s, k_s, v_s, gate_s, o_s, st_ref = refs[8 + n_proj_w:]

    @pl.when(pl.program_id(1) == 0)
    def _():
        st_ref[...] = s0_ref[0]

    proj(x_ref, *proj_w, q_s, g_s, k_s, v_s, gate_s)
    sub = dims["chunk"] * dims["n_chunks"]
    for r0 in range(0, x_ref.shape[0], sub):
        _linrec_heads(q_s, k_s, g_s, v_s, gate_s, ng_ref, o_s, st_ref, slice(r0, r0 + sub), **dims)
    h = _dot(o_s[...].astype(BF16), wo_ref[...])
    o_ref[...] = _layer_norm(ALPHA * x_ref[...] + h, g_ref[...], b_ref[...])
    sout_ref[0] = st_ref[...]


def _linrec_block(x, proj, proj_w, ng, s0, w_out, g, b, *, n_batch, seq, heads, hk, hv, name):
    t, d = x.shape
    dk, dv = hk // heads, hv // heads
    chunk, sub, n_sub = _linrec_tiling(seq)
    per_step = LINREC_BLOCK_SUBTILES if n_sub % LINREC_BLOCK_SUBTILES == 0 else 1
    tl, n_l = sub * per_step, n_sub // per_step
    rows = pl.BlockSpec((tl, d), lambda bi, l: (bi * n_l + l, 0))
    full = lambda shape: pl.BlockSpec(shape, lambda bi, l: (0,) * len(shape))
    state_spec = pl.BlockSpec((1, heads, dk, dv), lambda bi, l: (bi, 0, 0, 0))
    return pl.pallas_call(
        functools.partial(_linrec_block_kernel, proj=proj, n_proj_w=len(proj_w), heads=heads, dk=dk, dv=dv,
                          chunk=chunk, n_chunks=sub // chunk),
        grid=(n_batch, n_l),
        in_specs=[rows] + [full(w.shape) for w in proj_w] + [full((1, hv)), state_spec, full(w_out.shape),
                                                            full((1, d)), full((1, d))],
        out_specs=[rows, state_spec],
        out_shape=[jax.ShapeDtypeStruct((t, d), F32), jax.ShapeDtypeStruct((n_batch, heads, dk, dv), F32)],
        scratch_shapes=[pltpu.VMEM((tl, hk), F32), pltpu.VMEM((tl, hk), F32), pltpu.VMEM((tl, hk), F32),
                        pltpu.VMEM((tl, hv), F32), pltpu.VMEM((tl, hv), F32), pltpu.VMEM((tl, hv), F32),
                        pltpu.VMEM((heads, dk, dv), F32)],
        compiler_params=_params("parallel", "arbitrary"),
        name=name,
    )(x, *proj_w, ng, s0, w_out, g, b)


def _glu_kernel(x_ref, w_ref, b_ref, o_ref, *, d):
    xb = x_ref[...].astype(BF16)
    a = _dot(xb, w_ref[:, 0:d]) + b_ref[:, 0:d]
    gate = _dot(xb, w_ref[:, d:2 * d]) + b_ref[:, d:2 * d]
    o_ref[...] = a * _sigmoid(gate)


def _glu(x, w, b, row_off, n_rows):
    d = x.shape[1]
    tm = ROW_TILE
    return pl.pallas_call(
        functools.partial(_glu_kernel, d=d),
        grid=(n_rows // tm,),
        in_specs=[_rows(tm, d, row_off // tm), _full(w.shape), _full(b.shape)],
        out_specs=_rows(tm, d),
        out_shape=jax.ShapeDtypeStruct((n_rows, d), F32),
        compiler_params=_params("parallel"),
        name="conf_glu",
    )(x, w, b)


CONV_PAD = 32


CONV_LEAD = CONV_PAD - D_BUF


def _conv_window_step(buf_ref, h, state_of, wdw_ref, bdw_ref, lg_ref, lb_ref, conv_ref, first, tl):
    d = h.shape[1]
    rb = min(tl, 64)
    cw = LANES
    buf_ref[CONV_PAD + tl:CONV_PAD + tl + SUBLANES, :] = jnp.zeros((SUBLANES, d), F32)

    @pl.when(first)
    def _():
        buf_ref[CONV_LEAD:CONV_PAD, :] = state_of()

    buf_ref[CONV_PAD:CONV_PAD + tl, :] = h
    for r0 in range(0, tl, rb):
        for c0 in range(0, d, cw):
            cols = slice(c0, c0 + cw)
            acc = jnp.zeros((rb, cw), F32)
            for s in range(SUBLANES):
                part = None
                for a in range((CONV_PAD + SUBLANES) // SUBLANES):
                    j = SUBLANES * a + s - CONV_LEAD
                    if 0 <= j < D_CONV_W:
                        rows = slice(r0 + SUBLANES * a, r0 + SUBLANES * a + rb + SUBLANES)
                        term = wdw_ref[j:j + 1, cols] * buf_ref[rows, cols]
                        part = term if part is None else part + term
                acc = acc + part[s:s + rb, :]
            conv_ref[r0:r0 + rb, cols] = acc + bdw_ref[:, cols]
    y = _silu(_layer_norm(conv_ref[...], lg_ref[...], lb_ref[...]))
    new_state = buf_ref[tl + CONV_LEAD:tl + CONV_PAD, :]
    buf_ref[0:CONV_PAD, :] = buf_ref[tl:tl + CONV_PAD, :]
    return y, new_state


def _conv_kernel(h_ref, st_ref, wdw_ref, bdw_ref, lg_ref, lb_ref, o_ref, sout_ref, buf_ref, conv_ref, *, nb, tl):
    first = pl.program_id(1) == 0
    for n in range(nb):
        y, new_state = _conv_window_step(buf_ref, h_ref[n * tl:(n + 1) * tl, :], lambda: st_ref[n], wdw_ref,
                                         bdw_ref, lg_ref, lb_ref, conv_ref, first, tl)
        o_ref[n * tl:(n + 1) * tl, :] = y
        sout_ref[n] = new_state


CONV_TILE = 512


def _conv(h, state, wdw, bdw, lg, lb, *, n_batch, seq):
    t, d = h.shape
    tl = min(seq, CONV_TILE)
    n_l = seq // tl
    nb = 8 if n_l == 1 else 1
    rows = pl.BlockSpec((nb * tl, d), lambda b, l: (b * n_l + l, 0))
    state_spec = pl.BlockSpec((nb, D_BUF, d), lambda b, l: (b, 0, 0))
    return pl.pallas_call(
        functools.partial(_conv_kernel, nb=nb, tl=tl),
        grid=(n_batch // nb, n_l),
        in_specs=[rows, state_spec, _full(wdw.shape), _full((1, d)), _full((1, d)), _full((1, d))],
        out_specs=[rows, state_spec],
        out_shape=[jax.ShapeDtypeStruct((t, d), F32), jax.ShapeDtypeStruct((n_batch, D_BUF, d), F32)],
        scratch_shapes=[pltpu.VMEM((CONV_PAD + tl + SUBLANES, d), F32), pltpu.VMEM((tl, d), F32)],
        compiler_params=_params("parallel", "arbitrary"),
        name="conf_conv",
    )(h, state, wdw, bdw, lg, lb)


def _conf_block_kernel(x_ref, win_ref, bin_ref, st_ref, wdw_ref, bdw_ref, lg_ref, lb_ref, wo_ref, bo_ref,
                       g_ref, b_ref, o_ref, sout_ref, buf_ref, conv_ref):
    x = x_ref[...]
    tl, d = x.shape
    xb = x.astype(BF16)
    h = (_dot(xb, win_ref[:, 0:d]) + bin_ref[:, 0:d]) * _sigmoid(_dot(xb, win_ref[:, d:2 * d]) + bin_ref[:, d:2 * d])
    y, new_state = _conv_window_step(buf_ref, h, lambda: st_ref[0], wdw_ref, bdw_ref, lg_ref, lb_ref, conv_ref,
                                     pl.program_id(1) == 0, tl)
    out = _dot(y.astype(BF16), wo_ref[...]) + bo_ref[...]
    o_ref[...] = _layer_norm(ALPHA * x + out, g_ref[...], b_ref[...])
    sout_ref[0] = new_state


def _conf_block(x, w_in, b_in, state, wdw, bdw, lg, lb, w_out, b_out, g, b, *, n_batch, seq):
    t, d = x.shape
    tl = CONV_TILE
    n_l = seq // tl
    rows = pl.BlockSpec((tl, d), lambda bi, l: (bi * n_l + l, 0))
    full = lambda shape: pl.BlockSpec(shape, lambda bi, l: (0,) * len(shape))
    state_spec = pl.BlockSpec((1, D_BUF, d), lambda bi, l: (bi, 0, 0))
    return pl.pallas_call(
        _conf_block_kernel,
        grid=(n_batch, n_l),
        in_specs=[rows, full(w_in.shape), full(b_in.shape), state_spec, full(wdw.shape), full((1, d)), full((1, d)),
                  full((1, d)), full(w_out.shape), full((1, d)), full((1, d)), full((1, d))],
        out_specs=[rows, state_spec],
        out_shape=[jax.ShapeDtypeStruct((t, d), F32), jax.ShapeDtypeStruct((n_batch, D_BUF, d), F32)],
        scratch_shapes=[pltpu.VMEM((CONV_PAD + tl + SUBLANES, d), F32), pltpu.VMEM((tl, d), F32)],
        compiler_params=_params("parallel", "arbitrary"),
        name="conf_block_prompt",
    )(x, w_in, b_in, state, wdw, bdw, lg, lb, w_out, b_out, g, b)


def _attn_heads(q, k_of, v_of, hd):
    outs = []
    for h in range(MEM_HEADS):
        hs = slice(h * hd, (h + 1) * hd)
        s = _dot_nt(q[:, hs].astype(BF16), k_of(hs).astype(BF16)) * (hd ** -0.5)
        p = jnp.exp(s - jnp.max(s, axis=-1, keepdims=True))
        denom = jnp.sum(p, axis=-1, keepdims=True)
        outs.append(_dot(p.astype(BF16), v_of(hs).astype(BF16)) / denom)
    return outs


def _attn_block_kernel(x_ref, wq_ref, k_ref, v_ref, wo_ref, g_ref, b_ref, wr_hi_ref, wr_lo_ref, o_ref, lg_ref):
    x = x_ref[...]
    hd = x.shape[1] // MEM_HEADS
    q = _dot(x.astype(BF16), wq_ref[...])
    att = jnp.concatenate(_attn_heads(q, lambda hs: k_ref[:, hs], lambda hs: v_ref[:, hs], hd), axis=1)
    y = _layer_norm(ALPHA * x + _dot(att.astype(BF16), wo_ref[...]), g_ref[...], b_ref[...])
    lg_ref[...] = _dot_hi(y, wr_hi_ref[...], wr_lo_ref[...])
    o_ref[...] = y


ATTN_BLOCK_ROWS = 1024


def _attn_block_prompt(x, w_q, mem_k, mem_v, w_o, g, b, router, layer, n_batch, seq):
    t, d = x.shape
    m = mem_k.shape[1] // n_batch
    tl = min(seq, ATTN_BLOCK_ROWS)
    n_l = seq // tl
    rows = lambda n: pl.BlockSpec((tl, n), lambda bi, l: (bi * n_l + l, 0))
    kv = pl.BlockSpec((None, m, d), lambda bi, l: (layer, bi, 0))
    full = lambda shape: pl.BlockSpec(shape, lambda bi, l: (0,) * len(shape))
    return pl.pallas_call(
        _attn_block_kernel,
        grid=(n_batch, n_l),
        in_specs=[rows(d), full((d, d)), kv, kv, full((d, d)), full((1, d)), full((1, d)),
                  full((d, LANES)), full((d, LANES))],
        out_specs=[rows(d), rows(LANES)],
        out_shape=[jax.ShapeDtypeStruct((t, d), F32), jax.ShapeDtypeStruct((t, LANES), F32)],
        compiler_params=_params("parallel", "parallel"),
        name="attn_block_prompt",
    )(x, w_q, mem_k, mem_v, w_o, g, b, *router)


ATTN_SAMPLE_NB = 4


def _attn_sample_kernel(q_ref, k_ref, v_ref, o_ref, *, nb, seq):
    hd = q_ref.shape[-1] // MEM_HEADS
    n_lt = hd // LANES
    m = k_ref.shape[1] // (MEM_HEADS * n_lt)

    def head(ref, n, h):
        tiles = [ref[n, pl.ds(lt * MEM_HEADS + h, m, stride=MEM_HEADS * n_lt), :] for lt in range(n_lt)]
        return jnp.concatenate(tiles, axis=1).astype(BF16)

    pairs = [(n, h) for n in range(nb) for h in range(MEM_HEADS)]
    s = jnp.concatenate(
        [_dot_nt(q_ref[n * seq:(n + 1) * seq, h * hd:(h + 1) * hd].astype(BF16), head(k_ref, n, h))
         for n, h in pairs], axis=0) * (hd ** -0.5)
    p = jnp.exp(s - jnp.max(s, axis=-1, keepdims=True))
    inv = 1.0 / jnp.sum(p, axis=-1, keepdims=True)
    for idx, (n, h) in enumerate(pairs):
        rs = slice(idx * seq, (idx + 1) * seq)
        o = _dot(p[rs].astype(BF16), head(v_ref, n, h)) * inv[rs]
        o_ref[n * seq:(n + 1) * seq, h * hd:(h + 1) * hd] = o


def _cache_rows(cache):
    nl, nbat, m, heads, hd = cache.shape
    c = cache.reshape(nl, nbat, m, heads, hd // LANES, LANES)
    return c.transpose(0, 1, 2, 4, 3, 5).reshape(nl, nbat, m * (hd // LANES) * heads, LANES)


def _cache_unrows(flat, heads):
    nl, nbat, rows, _ = flat.shape
    n_lt = D_MODEL // (heads * LANES)
    m = rows // (heads * n_lt)
    c = flat.reshape(nl, nbat, m, n_lt, heads, LANES).transpose(0, 1, 2, 4, 3, 5)
    return c.reshape(nl, nbat, m, heads, n_lt * LANES)


def _attn_sample(q, cache_k, cache_v, layer, n_batch, seq):
    t, d = q.shape
    nb = ATTN_SAMPLE_NB
    rows = pl.BlockSpec((nb * seq, d), lambda i: (i, 0))
    kv = pl.BlockSpec((None, nb) + cache_k.shape[2:], lambda i: (layer, i, 0, 0))
    return pl.pallas_call(
        functools.partial(_attn_sample_kernel, nb=nb, seq=seq),
        grid=(n_batch // nb,),
        in_specs=[rows, kv, kv],
        out_specs=rows,
        out_shape=jax.ShapeDtypeStruct((t, d), F32),
        compiler_params=_params("parallel"),
        name="attn_sample",
    )(q, cache_k, cache_v)


def _route_kernel(lg_ref, bias_ref, ltri_ref, route_ref, counts_ref, carry_ref):
    i = pl.program_id(0)

    @pl.when(i == 0)
    def _():
        carry_ref[...] = jnp.zeros(carry_ref.shape, F32)

    z = lg_ref[...] + bias_ref[...]
    lane = lax.broadcasted_iota(jnp.int32, z.shape, 1).astype(F32)
    neg = -jnp.inf
    far = float(LANES)

    def first_max(mask):
        vmax = jnp.max(jnp.where(mask, z, neg), axis=-1, keepdims=True)
        idx = jnp.min(jnp.where(mask & (z == vmax), lane, far), axis=-1, keepdims=True)
        return vmax, idx

    gmask = lane < float(MOE_GROUPS)
    gmax, gidx = first_max(gmask)
    gsum = jnp.sum(jnp.where(gmask, jnp.exp(z - gmax), 0.0), axis=-1, keepdims=True)
    g_w = 1.0 / gsum
    lo = float(ROUTE_LANE0) + float(MOE_PER_GROUP) * gidx
    emask = (lane >= lo) & (lane < lo + float(MOE_PER_GROUP))
    v1, i1 = first_max(emask)
    v2, i2 = first_max(emask & (lane != i1))
    tt = jnp.exp(v2 - v1)
    w0 = g_w / (1.0 + tt)
    w1 = g_w * tt / (1.0 + tt)
    sel1 = lane == i1
    sel2 = lane == i2
    onehot = jnp.where(sel1 | sel2, 1.0, 0.0)
    before = _dot(ltri_ref[...], onehot.astype(BF16)) + carry_ref[...]
    rank0 = jnp.sum(jnp.where(sel1, before, 0.0), axis=-1, keepdims=True)
    rank1 = jnp.sum(jnp.where(sel2, before, 0.0), axis=-1, keepdims=True)
    carry = carry_ref[...] + jnp.sum(onehot, axis=0, keepdims=True)
    carry_ref[...] = carry
    counts_ref[...] = carry
    e_off = float(ROUTE_LANE0)
    out = jnp.zeros(z.shape, F32)
    for ln, val in enumerate((i1 - e_off, i2 - e_off, w0, w1, rank0, rank1)):
        out = jnp.where(lane == float(ln), val, out)
    route_ref[...] = out


def _route(logits, bias, ltri):
    t = logits.shape[0]
    tm = ROW_TILE
    return pl.pallas_call(
        _route_kernel,
        grid=(t // tm,),
        in_specs=[_rows(tm, LANES), _full((1, LANES)), _full((tm, tm))],
        out_specs=[_rows(tm, LANES), _full((1, LANES))],
        out_shape=[jax.ShapeDtypeStruct((t, LANES), F32), jax.ShapeDtypeStruct((1, LANES), F32)],
        scratch_shapes=[pltpu.VMEM((1, LANES), F32)],
        compiler_params=_params("arbitrary"),
        name="moe_route",
    )(logits, bias, ltri)


def _row_copy_wait(src_rows, dst_rows, sem):
    pltpu.make_async_copy(src_rows, dst_rows, sem).wait()


def _dispatch_kernel(tail_ref, dest_ref, x_ref, xs_ref, zeros_ref, sem, *, tm):
    @pl.when(pl.program_id(0) == 0)
    def _():
        zeros_ref[...] = jnp.zeros(zeros_ref.shape, F32)
        bm = zeros_ref.shape[0]
        tails = [xs_ref.at[pl.ds(pl.multiple_of(tail_ref[e], bm), bm)] for e in range(MOE_EXPERTS)]
        for dst in tails:
            pltpu.make_async_copy(zeros_ref, dst, sem).start()
        for dst in tails:
            pltpu.make_async_copy(zeros_ref, dst, sem).wait()

    def issue(t, carry):
        for j in range(2):
            d = dest_ref[2 * t + j]
            pltpu.make_async_copy(x_ref.at[pl.ds(t, 1)], xs_ref.at[pl.ds(d, 1)], sem).start(priority=j)
        return carry

    lax.fori_loop(0, tm, issue, 0, unroll=8)
    for _ in range(2):
        _row_copy_wait(x_ref, xs_ref.at[pl.ds(0, tm)], sem)


def _dispatch(tail_rows, dest_flat, x, n_rows):
    t, d = x.shape
    tm = ROW_TILE
    grid_spec = pltpu.PrefetchScalarGridSpec(
        num_scalar_prefetch=1,
        grid=(t // tm,),
        in_specs=[pl.BlockSpec((2 * tm,), lambda i, tail: (i,), memory_space=pltpu.SMEM),
                  pl.BlockSpec((tm, d), lambda i, tail: (i, 0))],
        out_specs=pl.BlockSpec(memory_space=pl.ANY),
        scratch_shapes=[pltpu.VMEM((MOE_BLOCK_ROWS, d), F32), pltpu.SemaphoreType.DMA(())],
    )
    return pl.pallas_call(
        functools.partial(_dispatch_kernel, tm=tm),
        grid_spec=grid_spec,
        out_shape=jax.ShapeDtypeStruct((n_rows, d), F32),
        compiler_params=_params("arbitrary"),
        name="moe_dispatch",
    )(tail_rows, dest_flat, x)


def _expert_kernel(blk_e_ref, nused_ref, xs_ref, wgu_ref, wd_ref, y_ref, wgu_bf, wd_bf):
    i = pl.program_id(0)
    prev = blk_e_ref[jnp.maximum(i - 1, 0)]
    new_expert = (i == 0) | (blk_e_ref[i] != prev)

    @pl.when(new_expert)
    def _():
        wgu_bf[...] = wgu_ref[...].astype(BF16)
        wd_bf[...] = wd_ref[...].astype(BF16)

    @pl.when(i < nused_ref[0])
    def _():
        hid = wd_bf.shape[0]
        xb = xs_ref[...].astype(BF16)
        gate = _dot(xb, wgu_bf[:, 0:hid])
        up = _dot(xb, wgu_bf[:, hid:2 * hid])
        y_ref[...] = _dot((_silu(gate) * up).astype(BF16), wd_bf[...])


def _experts(blk_e, nused, xs, w_gate_up, w_down, layer):
    nr, d = xs.shape
    bm = MOE_BLOCK_ROWS
    hid2 = w_gate_up.shape[-1]
    hid = w_down.shape[-2]
    grid_spec = pltpu.PrefetchScalarGridSpec(
        num_scalar_prefetch=2,
        grid=(nr // bm,),
        in_specs=[pl.BlockSpec((bm, d), lambda i, be, nu: (jnp.minimum(i, nu[0] - 1), 0)),
                  pl.BlockSpec((None, None, d, hid2), lambda i, be, nu: (layer, be[i], 0, 0)),
                  pl.BlockSpec((None, None, hid, d), lambda i, be, nu: (layer, be[i], 0, 0))],
        out_specs=pl.BlockSpec((bm, d), lambda i, be, nu: (jnp.minimum(i, nu[0] - 1), 0)),
        scratch_shapes=[pltpu.VMEM((d, hid2), BF16), pltpu.VMEM((hid, d), BF16)],
    )
    return pl.pallas_call(
        _expert_kernel,
        grid_spec=grid_spec,
        out_shape=jax.ShapeDtypeStruct((nr, d), F32),
        compiler_params=_params("arbitrary"),
        name="moe_experts",
    )(blk_e, nused, xs, w_gate_up, w_down)


def _combine_kernel(dest_ref, dest_next_ref, x_ref, route_ref, g_ref, b_ref, y_hbm, *rest, tm, n_first):
    if n_first is None:
        o_ref, ybuf, sems = rest
    else:
        o_ref, o2_ref, ybuf, sems = rest
    i = pl.program_id(0)
    n_tiles = pl.num_programs(0)

    def gather(d_ref, slot):
        def issue(t, carry):
            for j in range(2):
                d = d_ref[2 * t + j]
                pltpu.make_async_copy(y_hbm.at[pl.ds(d, 1)], ybuf.at[slot, j, pl.ds(t, 1)],
                                      sems.at[slot]).start(priority=j)
            return carry

        lax.fori_loop(0, tm, issue, 0, unroll=8)

    @pl.when(i == 0)
    def _():
        gather(dest_ref, 0)

    @pl.when(i + 1 < n_tiles)
    def _():
        gather(dest_next_ref, lax.rem(i + 1, 2))

    slot = lax.rem(i, 2)
    for j in range(2):
        _row_copy_wait(y_hbm.at[pl.ds(0, tm)], ybuf.at[slot, j], sems.at[slot])
    moe = route_ref[:, 2:3] * ybuf[slot, 0] + route_ref[:, 3:4] * ybuf[slot, 1]
    y = _layer_norm(ALPHA * x_ref[...] + moe, g_ref[...], b_ref[...])
    if n_first is None:
        o_ref[...] = y
    else:
        @pl.when(i < n_first)
        def _():
            o_ref[...] = y

        @pl.when(i >= n_first)
        def _():
            o2_ref[...] = y


def _combine(dest_flat, x, route, g, b, yb, split_rows=None):
    t, d = x.shape
    tm = ROW_TILE
    n_tiles = t // tm
    out_specs = _rows(tm, d)
    out_shape = jax.ShapeDtypeStruct((t, d), F32)
    n_first = None
    if split_rows is not None:
        n_first = split_rows // tm
        out_specs = [pl.BlockSpec((tm, d), lambda i: (jnp.minimum(i, n_first - 1), 0)),
                     pl.BlockSpec((tm, d), lambda i: (jnp.maximum(i - n_first, 0), 0))]
        out_shape = [jax.ShapeDtypeStruct((split_rows, d), F32), jax.ShapeDtypeStruct((t - split_rows, d), F32)]
    return pl.pallas_call(
        functools.partial(_combine_kernel, tm=tm, n_first=n_first),
        grid=(n_tiles,),
        in_specs=[pl.BlockSpec((2 * tm,), lambda i: (i,), memory_space=pltpu.SMEM),
                  pl.BlockSpec((2 * tm,), lambda i: (jnp.minimum(i + 1, n_tiles - 1),), memory_space=pltpu.SMEM),
                  _rows(tm, d), _rows(tm, LANES), _full((1, d)), _full((1, d)),
                  pl.BlockSpec(memory_space=pl.ANY)],
        out_specs=out_specs,
        out_shape=out_shape,
        scratch_shapes=[pltpu.VMEM((2, 2, tm, d), F32), pltpu.SemaphoreType.DMA((2,))],
        compiler_params=_params("arbitrary"),
        name="moe_combine",
    )(dest_flat, dest_flat, x, route, g, b, yb)


def _moe_plan(route, counts):
    bm = MOE_BLOCK_ROWS
    t = route.shape[0]
    n_blocks = -(-2 * t // bm) + MOE_EXPERTS
    e_idx = route[:, 0:2].astype(jnp.int32)
    rank = route[:, 4:6].astype(jnp.int32)
    cnt = counts[0, ROUTE_LANE0:ROUTE_LANE0 + MOE_EXPERTS].astype(jnp.int32)
    padded = ((cnt + bm - 1) // bm) * bm
    pad_end = jnp.cumsum(padded)
    pad_start = pad_end - padded
    experts = jnp.arange(MOE_EXPERTS, dtype=jnp.int32)
    dest = (rank + jnp.sum(jnp.where(e_idx[..., None] == experts, pad_start, 0), axis=-1)).reshape(-1)
    nused = pad_end[-1] // bm
    blk = jnp.arange(n_blocks, dtype=jnp.int32)
    blk_e = jnp.sum((pad_end[None, :] <= (blk * bm)[:, None]).astype(jnp.int32), axis=1)
    blk_e = jnp.minimum(blk_e, MOE_EXPERTS - 1)
    last_e = jnp.max(jnp.where(cnt > 0, experts, 0))
    blk_e = jnp.where(blk < nused, blk_e, last_e)
    empty = cnt == 0
    tail_rows = jnp.where(empty, (nused + jnp.cumsum(empty.astype(jnp.int32)) - 1) * bm, pad_end - bm)
    return (dest.astype(jnp.int32), blk_e.astype(jnp.int32), nused.reshape(1).astype(jnp.int32),
            tail_rows.astype(jnp.int32), n_blocks)


def _hi_lo(w):
    hi = w.astype(BF16)
    return hi, (w - hi.astype(F32)).astype(BF16)


def kernel(x_prompt, x_sample, mem_prompt, cache_mem_k, cache_mem_v, state_gla, state_hgrn, state_conv,
           ln_g, ln_b, a_w_in, a_b_in, a_ln_g, a_ln_b, a_w_s, a_b_s, a_w_out, a_b_out,
           b_w_in, b_w_g2, b_b_g, b_norm_g, b_w_out, c_lb, c_w_in, c_norm_g, c_w_out,
           d_w_in, d_b_in, d_w_dw, d_b_dw, d_ln_g, d_ln_b, d_w_out, d_b_out,
           m_w_q, m_w_k, m_w_v, m_w_o, r_w_grp, r_b_grp, r_w_exp, r_b_exp, e_w_gate_up, e_w_down):
    bp, lp, d = x_prompt.shape
    bs, ls, _ = x_sample.shape
    tp, ts = bp * lp, bs * ls
    t = tp + ts
    mem_len = mem_prompt.shape[1]

    row = lambda a: a.reshape(1, -1)
    x = None

    mem2d = mem_prompt.reshape(bp * mem_len, d)
    mem_k, mem_k_flat = _mem_proj(mem2d, m_w_k.astype(BF16), mem_len)
    mem_v, mem_v_flat = _mem_proj(mem2d, m_w_v.astype(BF16), mem_len)

    cache_k = _cache_rows(cache_mem_k)
    cache_v = _cache_rows(cache_mem_v)

    lb_all = jnp.cumsum(jax.nn.softmax(c_lb.astype(F32), axis=0), axis=0)
    lb_all = lb_all - lb_all[:1]
    ltri = jnp.tril(jnp.ones((ROW_TILE, ROW_TILE), F32), -1).astype(BF16)
    zero_bias = jnp.zeros((1, d), F32)

    outs = {"v": [], "gla_p": [], "gla_s": [], "hgrn_p": [], "hgrn_s": [], "conv_p": [], "conv_s": []}
    for i in range(DEPTH):
        j = i // N_MIXERS
        kind = i % N_MIXERS
        g1, b1 = row(ln_g[i, 0]), row(ln_b[i, 0])
        if kind == 0:
            tril = jnp.tril(jnp.ones((A_CHUNK, A_CHUNK), bool))
            wc_p = jnp.where(tril, a_w_s[j], 0.0).astype(BF16)
            bc_p = a_b_s[j][:, :, None]
            reps = A_CHUNK // ls
            small = jnp.where(jnp.tril(jnp.ones((ls, ls), bool)), a_w_s[j][:, :ls, :ls], 0.0)
            wc_s = jax.vmap(lambda m: jnp.kron(jnp.eye(reps, dtype=F32), m))(small).astype(BF16)
            bc_s = jnp.tile(a_b_s[j][:, :ls], (1, reps))[:, :, None]
            common = (a_w_in[j].astype(BF16), row(a_b_in[j]), row(a_ln_g[j]), row(a_ln_b[j]))
            tail = (a_w_out[j].astype(BF16), row(a_b_out[j]), g1, b1)
            xp_in = x_prompt.reshape(tp, d) if x is None else x[:tp]
            xs_in = x_sample.reshape(ts, d) if x is None else x[tp:]
            x1 = _gmlp(xp_in, 0, t, *common, wc_p, bc_p, *tail)
            x1, v_s = _gmlp(xs_in, tp, t, *common, wc_s, bc_s, *tail, alias=x1)
            outs["v"].append(v_s.reshape(bs, ls, -1))
        elif kind == 1:
            dk, dv = b_w_g2.shape[-1], b_w_out.shape[1]
            w_in = b_w_in[j]
            w_main = w_in[:, :2 * dk + 2 * dv].astype(BF16)
            w_low = jnp.pad(w_in[:, 2 * dk + 2 * dv:], ((0, 0), (0, LANES - B_GATE_RANK))).astype(BF16)
            g2_hi, g2_lo = _hi_lo(jnp.pad(b_w_g2[j], ((0, LANES - B_GATE_RANK), (0, 0))))
            proj_w = (w_main, w_low, g2_hi, g2_lo, row(b_b_g[j]))
            proj = functools.partial(_gla_proj_kernel, dk=dk, dv=dv, q_scale=(dk // B_HEADS) ** -0.5)
            ng = row(b_norm_g[j])
            w_out = b_w_out[j].astype(BF16)
            s0_p = jnp.zeros((bp,) + state_gla.shape[2:], F32)
            x1, s_p = _linrec_block(x, proj, proj_w, ng, s0_p, w_out, g1, b1, n_batch=bp, seq=lp, heads=B_HEADS,
                                    hk=dk, hv=dv, name="gla_block_prompt")
            q, la, k, v, r = _gla_proj(x, proj_w, dk, dv, tp, ts)
            o_s, s_s = _linrec(q, k, la, v, r, ng, state_gla[j], n_batch=bs, seq=ls, heads=B_HEADS)
            x1 = _mm_ln(o_s, w_out, zero_bias, x, g1, b1, "gla_out_sample", row_off=tp, fill=(x1,))
            outs["gla_p"].append(s_p)
            outs["gla_s"].append(s_s)
        elif kind == 2:
            heads = state_hgrn.shape[2]
            proj_w = (c_w_in[j].astype(BF16), row(lb_all[i]))
            proj = functools.partial(_hgrn_proj_kernel, d=d, q_scale=C_EXPAND ** -0.5)
            ng = row(c_norm_g[j])
            w_out = c_w_out[j].astype(BF16)
            s0_p = jnp.zeros((bp,) + state_hgrn.shape[2:], F32)
            x1, s_p = _linrec_block(x, proj, proj_w, ng, s0_p, w_out, g1, b1, n_batch=bp, seq=lp, heads=heads,
                                    hk=d, hv=d, name="hgrn_block_prompt")
            q, lf, k, v, gt = _hgrn_proj(x, proj_w, tp, ts)
            o_s, s_s = _linrec(q, k, lf, v, gt, ng, state_hgrn[j], n_batch=bs, seq=ls, heads=heads)
            x1 = _mm_ln(o_s, w_out, zero_bias, x, g1, b1, "hgrn_out_sample", row_off=tp, fill=(x1,))
            outs["hgrn_p"].append(s_p)
            outs["hgrn_s"].append(s_s)
        else:
            w_in, b_in = d_w_in[j].astype(BF16), row(d_b_in[j])
            w_out, b_out = d_w_out[j].astype(BF16), row(d_b_out[j])
            cargs = (d_w_dw[j], row(d_b_dw[j]), row(d_ln_g[j]), row(d_ln_b[j]))
            conv0 = jnp.zeros((bp, D_BUF, d), F32)
            x1, s_p = _conf_block(x, w_in, b_in, conv0, *cargs, w_out, b_out, g1, b1, n_batch=bp, seq=lp)
            h_s = _glu(x, w_in, b_in, tp, ts)
            c_s, s_s = _conv(h_s, state_conv[j], *cargs, n_batch=bs, seq=ls)
            x1 = _mm_ln(c_s, w_out, b_out, x, g1, b1, "conf_out_sample", row_off=tp, fill=(x1,))
            outs["conv_p"].append(s_p)
            outs["conv_s"].append(s_s)

        w_route = jnp.concatenate([r_w_grp[i], r_w_exp[i]], axis=1)
        w_route = jnp.pad(w_route, ((0, 0), (0, LANES - w_route.shape[1])))
        b_route = jnp.pad(jnp.concatenate([r_b_grp[i], r_b_exp[i]]), (0, LANES - MOE_GROUPS - MOE_EXPERTS))
        router = _hi_lo(w_route)
        w_q, w_o = m_w_q[i].astype(BF16), m_w_o[i].astype(BF16)
        g2, b2 = row(ln_g[i, 1]), row(ln_b[i, 1])
        x2, logits = _attn_block_prompt(x1, w_q, mem_k, mem_v, w_o, g2, b2, router, i, bp, lp)
        q_s = _mm_rows(x1, w_q, "attn_q_sample", tp, ts)
        att_s = _attn_sample(q_s, cache_k, cache_v, i, bs, ls)
        x2, logits = _mm_ln(att_s, w_o, zero_bias, x1, g2, b2, "attn_out_sample", router=router, row_off=tp,
                            fill=(x2, logits))

        route, counts = _route(logits, row(b_route), ltri)
        dest, blk_e, nused, tail_rows, n_blocks = _moe_plan(route, counts)
        xs = _dispatch(tail_rows, dest, x2, n_blocks * MOE_BLOCK_ROWS)
        yb = _experts(blk_e, nused, xs, e_w_gate_up, e_w_down, i)
        if i + 1 < DEPTH:
            x = _combine(dest, x2, route, row(ln_g[i, 2]), row(ln_b[i, 2]), yb)
        else:
            y_p, y_s = _combine(dest, x2, route, row(ln_g[i, 2]), row(ln_b[i, 2]), yb, split_rows=tp)

    y_prompt = y_p.reshape(bp, lp, d)
    y_sample = y_s.reshape(bs, ls, d)
    mem_k_p = _cache_unrows(mem_k_flat, MEM_HEADS)
    mem_v_p = _cache_unrows(mem_v_flat, MEM_HEADS)
    return (y_prompt, y_sample, mem_k_p, mem_v_p, jnp.stack(outs["gla_p"]), jnp.stack(outs["hgrn_p"]),
            jnp.stack(outs["conv_p"]), jnp.stack(outs["v"]), jnp.stack(outs["gla_s"]),
            jnp.stack(outs["hgrn_s"]), jnp.stack(outs["conv_s"]))
```
